```python
import jax, jax.numpy as jnp
from jax import lax
import numpy as np

D_MODEL = 1024
BATCH = 8
SEQ = 8192
DEPTH = 2

PLE_DIM = 256
D_FF = 2816
RMS_EPS = 1e-6
N_NORMS = 8

LRU_HEADS = 6
LRU_HEAD_DIM = 64
LRU_WIDTH = LRU_HEADS * LRU_HEAD_DIM
CONV_WIDTH = 4
LRU_C = 8.0
MLSTM_HEADS = 4
MLSTM_HEAD_DIM = 64
MLSTM_WIDTH = MLSTM_HEADS * MLSTM_HEAD_DIM
MLSTM_CHUNK = 64
RWKV_HEADS = 6
RWKV_HEAD_DIM = 64
RWKV_WIDTH = RWKV_HEADS * RWKV_HEAD_DIM
DECAY_LORA = 64
ICLR_LORA = 64
GATE_LORA = 128
GN_EPS = 64e-5

D_MIX = LRU_WIDTH + MLSTM_WIDTH + RWKV_WIDTH
RWKV_SPLITS = (RWKV_WIDTH, RWKV_WIDTH, RWKV_WIDTH, DECAY_LORA, ICLR_LORA, GATE_LORA)
N_RWKV_COLS = 3 * RWKV_WIDTH + DECAY_LORA + ICLR_LORA + GATE_LORA
MIXER_SPLITS = (LRU_WIDTH, LRU_WIDTH,
                MLSTM_WIDTH, MLSTM_WIDTH, MLSTM_WIDTH, MLSTM_WIDTH, MLSTM_HEADS, MLSTM_HEADS,
                N_RWKV_COLS)
N_IN = 2 * LRU_WIDTH + 4 * MLSTM_WIDTH + 2 * MLSTM_HEADS + N_RWKV_COLS

kernel_name = "hymba_style_lru_mlstm_rwkv7_trunk"


def _split_cols(z, sizes):
    idx = [int(s) for s in np.cumsum(sizes)[:-1]]
    return jnp.split(z, idx, axis=-1)


def _rmsnorm(x, g):
    xf = x.astype(jnp.float32)
    y = xf * lax.rsqrt(jnp.mean(xf * xf, axis=-1, keepdims=True) + RMS_EPS)
    return (y * g.astype(jnp.float32)).astype(x.dtype)


def _swiglu(h, w_in, w_out):
    gate, up = jnp.split(h @ w_in, 2, axis=-1)
    return (jax.nn.silu(gate) * up) @ w_out


def _causal_depthwise_conv(x, w, b):
    c = x.shape[-1]
    y = lax.conv_general_dilated(
        x, w[:, None, :].astype(x.dtype), window_strides=(1,),
        padding=((CONV_WIDTH - 1, 0),), dimension_numbers=("NWC", "WIO", "NWC"),
        feature_group_count=c)
    return y + b.astype(x.dtype)


def _rglru(x, w_a, b_a, w_x, b_x, lam):
    bsz, t, _ = x.shape
    f32 = jnp.float32
    xb = x.reshape(bsz, t, LRU_HEADS, LRU_HEAD_DIM)
    r = jax.nn.sigmoid((jnp.einsum("bthi,hij->bthj", xb, w_a).reshape(bsz, t, LRU_WIDTH) + b_a).astype(f32))
    i = jax.nn.sigmoid((jnp.einsum("bthi,hij->bthj", xb, w_x).reshape(bsz, t, LRU_WIDTH) + b_x).astype(f32))
    log_a = -LRU_C * r * jax.nn.softplus(-lam.astype(f32))
    a = jnp.exp(log_a)
    u = jnp.sqrt(-jnp.expm1(2.0 * log_a)) * (i * x.astype(f32))

    def combine(c1, c2):
        a1, b1 = c1
        a2, b2 = c2
        return a1 * a2, a2 * b1 + b2

    _, h = lax.associative_scan(combine, (a, u), axis=1)
    return h.astype(x.dtype)


def _mlstm_chunkwise(q, k, v, i_pre, f_pre):
    bsz, t, nh, d = q.shape
    nc = t // MLSTM_CHUNK
    f32 = jnp.float32
    def chunks(z):
        return z.astype(f32).reshape(bsz, nc, MLSTM_CHUNK, nh, d).transpose(1, 0, 3, 2, 4)
    def gchunks(z):
        return z.astype(f32).reshape(bsz, nc, MLSTM_CHUNK, nh).transpose(1, 0, 3, 2)
    qc, kc, vc = chunks(q) * (d ** -0.5), chunks(k), chunks(v)
    ic = gchunks(i_pre)
    bc = jnp.cumsum(jax.nn.log_sigmoid(gchunks(f_pre)), axis=-1)
    mask = jnp.tril(jnp.ones((MLSTM_CHUNK, MLSTM_CHUNK), bool))

    def step(carry, inp):
        c_st, n_st, m_st = carry
        qj, kj, vj, bj, ij = inp
        g = bj[..., -1]
        dmat = jnp.where(mask, bj[..., :, None] - bj[..., None, :] + ij[..., None, :], -jnp.inf)
        inter = bj + m_st[..., None]
        mj = jnp.maximum(jnp.max(dmat, axis=-1), inter)
        s = jnp.exp(dmat - mj[..., None]) * jnp.einsum("bhld,bhsd->bhls", qj, kj)
        sc = jnp.exp(inter - mj)
        num = jnp.einsum("bhls,bhsd->bhld", s, vj) + sc[..., None] * jnp.einsum("bhld,bhde->bhle", qj, c_st)
        den = jnp.sum(s, axis=-1) + sc * jnp.einsum("bhld,bhd->bhl", qj, n_st)
        h = num / jnp.maximum(jnp.abs(den), jnp.exp(-mj))[..., None]
        w_log = g[..., None] - bj + ij
        m_new = jnp.maximum(g + m_st, jnp.max(w_log, axis=-1))
        dec = jnp.exp(g + m_st - m_new)
        wk = jnp.exp(w_log - m_new[..., None])
        c_new = dec[..., None, None] * c_st + jnp.einsum("bhl,bhld,bhle->bhde", wk, kj, vj)
        n_new = dec[..., None] * n_st + jnp.einsum("bhl,bhld->bhd", wk, kj)
        return (c_new, n_new, m_new), h

    init = (jnp.zeros((bsz, nh, d, d), f32), jnp.zeros((bsz, nh, d), f32), jnp.zeros((bsz, nh), f32))
    _, h = lax.scan(step, init, (qc, kc, vc, bc, ic))
    return h.transpose(1, 0, 3, 2, 4).reshape(bsz, t, nh, d).astype(q.dtype)


def _rwkv7(zr, mu, w0, w_up, a0, a_up, g_up, k_k, k_a, r_k, ln_w, ln_b):
    bsz, t, _ = zr.shape
    f32 = jnp.float32
    prev = jnp.pad(zr, ((0, 0), (1, 0), (0, 0)))[:, :-1]
    zs = zr + (prev - zr) * mu
    r, k, v, wd, ad, gd = _split_cols(zs, RWKV_SPLITS)
    log_w = -jax.nn.softplus(-(w0 + jnp.tanh(wd) @ w_up).astype(f32)) - 0.5
    decay = jnp.exp(-jnp.exp(log_w))
    iclr = jax.nn.sigmoid((a0 + ad @ a_up).astype(f32))
    g = jax.nn.sigmoid(gd) @ g_up

    def heads(z):
        return z.astype(f32).reshape(bsz, t, RWKV_HEADS, RWKV_HEAD_DIM)

    kk = heads(k * k_k)
    kk = kk / jnp.maximum(jnp.sqrt(jnp.sum(kk * kk, axis=-1, keepdims=True)), 1e-12)
    k_mod = k.astype(f32) * (1.0 + (iclr - 1.0) * k_a.astype(f32))
    r_h, k_h, v_h, w_h, a_h = heads(r), heads(k_mod), heads(v), heads(decay), heads(iclr)
    a_vec = -kk
    b_vec = kk * a_h

    def step(s_st, inp):
        r_t, w_t, k_t, v_t, a_t, b_t = inp
        sa = jnp.einsum("bhij,bhj->bhi", s_st, a_t)
        s_st = s_st * w_t[:, :, None, :] + sa[..., None] * b_t[:, :, None, :] + v_t[..., None] * k_t[:, :, None, :]
        return s_st, jnp.einsum("bhij,bhj->bhi", s_st, r_t)

    tm = lambda z: jnp.moveaxis(z, 1, 0)
    s0 = jnp.zeros((bsz, RWKV_HEADS, RWKV_HEAD_DIM, RWKV_HEAD_DIM), f32)
    _, y = lax.scan(step, s0, (tm(r_h), tm(w_h), tm(k_h), tm(v_h), tm(a_vec), tm(b_vec)))
    y = jnp.moveaxis(y, 0, 1)
    mean = jnp.mean(y, axis=-1, keepdims=True)
    var = jnp.mean(jnp.square(y - mean), axis=-1, keepdims=True)
    yn = ((y - mean) * lax.rsqrt(var + GN_EPS)).reshape(bsz, t, RWKV_WIDTH) * ln_w.astype(f32) + ln_b.astype(f32)
    bonus = (jnp.sum(r_h * k_h * r_k.astype(f32), axis=-1, keepdims=True) * v_h).reshape(bsz, t, RWKV_WIDTH)
    return ((yn + bonus) * g.astype(f32)).astype(zr.dtype)


def _token_mixer(h, w_in, w_out, lru_conv_w, lru_conv_b, lru_w_a, lru_b_a, lru_w_x, lru_b_x, lru_lambda,
                 m_b_i, m_b_f, m_norm, rw_mu, rw_w0, rw_w_up, rw_a0, rw_a_up, rw_g_up, rw_k_k, rw_k_a,
                 rw_r_k, rw_ln_w, rw_ln_b):
    bsz, t, _ = h.shape
    z = h @ w_in
    lru_x, lru_gate, mq, mk, mv, mo, mi, mf, zr = _split_cols(z, MIXER_SPLITS)
    xa = _causal_depthwise_conv(lru_x, lru_conv_w, lru_conv_b)
    ya = _rglru(xa, lru_w_a, lru_b_a, lru_w_x, lru_b_x, lru_lambda) * jax.nn.gelu(lru_gate, approximate=True)
    mh = lambda zz: zz.reshape(bsz, t, MLSTM_HEADS, MLSTM_HEAD_DIM)
    hb = _mlstm_chunkwise(mh(mq), mh(mk), mh(mv), mi + m_b_i, mf + m_b_f)
    hb = _rmsnorm(hb, m_norm.reshape(MLSTM_HEADS, MLSTM_HEAD_DIM)).reshape(bsz, t, MLSTM_WIDTH)
    yb = (hb * jax.nn.sigmoid(mo)).astype(ya.dtype)
    yc = _rwkv7(zr, rw_mu, rw_w0, rw_w_up, rw_a0, rw_a_up, rw_g_up, rw_k_k, rw_k_a, rw_r_k,
                rw_ln_w, rw_ln_b).astype(ya.dtype)
    y = jnp.concatenate([ya, yb, yc], axis=-1)
    return y @ w_out


def setup_inputs(seed: int = 0) -> dict:
    key = jax.random.key(seed)
    ks = iter(jax.random.split(key, 48))
    f32 = jnp.float32
    def nrm(shape, scale):
        return jax.random.normal(next(ks), shape, f32) * scale
    L = DEPTH
    x = nrm((BATCH, SEQ, D_MODEL), 1.0)
    p = nrm((DEPTH, BATCH, SEQ, PLE_DIM), 1.0)
    norm_g = 1.0 + nrm((L, N_NORMS, D_MODEL), 0.05)
    ffn_w_in = nrm((L, 2, D_MODEL, 2 * D_FF), D_MODEL ** -0.5)
    ffn_w_out = nrm((L, 2, D_FF, D_MODEL), D_FF ** -0.5)
    w_in = nrm((L, D_MODEL, N_IN), D_MODEL ** -0.5)
    w_out = nrm((L, D_MIX, D_MODEL), D_MIX ** -0.5)
    lru_conv_w = nrm((L, CONV_WIDTH, LRU_WIDTH), CONV_WIDTH ** -0.5)
    lru_conv_b = nrm((L, LRU_WIDTH), 0.01)
    lru_w_a = nrm((L, LRU_HEADS, LRU_HEAD_DIM, LRU_HEAD_DIM), LRU_HEAD_DIM ** -0.5)
    lru_b_a = nrm((L, LRU_WIDTH), 0.01)
    lru_w_x = nrm((L, LRU_HEADS, LRU_HEAD_DIM, LRU_HEAD_DIM), LRU_HEAD_DIM ** -0.5)
    lru_b_x = nrm((L, LRU_WIDTH), 0.01)
    u = jax.random.uniform(next(ks), (L, LRU_WIDTH), f32, 0.9, 0.999)
    a_base = u ** (1.0 / LRU_C)
    lru_lambda = jnp.log(a_base) - jnp.log1p(-a_base)
    m_b_i = nrm((L, MLSTM_HEADS), 0.5) - 1.0
    m_b_f = jnp.linspace(3.0, 6.0, MLSTM_HEADS, dtype=f32)[None, :] + nrm((L, MLSTM_HEADS), 0.1)
    m_norm = 1.0 + nrm((L, MLSTM_WIDTH), 0.05)
    rw_mu = jax.random.uniform(next(ks), (L, N_RWKV_COLS), f32)
    rw_w0 = jax.random.uniform(next(ks), (L, RWKV_WIDTH), f32, -6.5, -1.5)
    rw_w_up = nrm((L, DECAY_LORA, RWKV_WIDTH), 0.5 * DECAY_LORA ** -0.5)
    rw_a0 = nrm((L, RWKV_WIDTH), 0.1)
    rw_a_up = nrm((L, ICLR_LORA, RWKV_WIDTH), 0.5 * ICLR_LORA ** -0.5)
    rw_g_up = nrm((L, GATE_LORA, RWKV_WIDTH), GATE_LORA ** -0.5)
    rw_k_k = 0.85 + nrm((L, RWKV_WIDTH), 0.05)
    rw_k_a = 1.0 + nrm((L, RWKV_WIDTH), 0.05)
    rw_r_k = nrm((L, RWKV_HEADS, RWKV_HEAD_DIM), 0.1)
    rw_ln_w = 1.0 + nrm((L, RWKV_WIDTH), 0.05)
    rw_ln_b = nrm((L, RWKV_WIDTH), 0.01)
    ple_w_proj = nrm((L, PLE_DIM, D_MODEL), PLE_DIM ** -0.5)
    ple_w_gate = nrm((L, D_MODEL, D_MODEL), D_MODEL ** -0.5)
    return {"x": x, "p": p, "norm_g": norm_g, "ffn_w_in": ffn_w_in, "ffn_w_out": ffn_w_out,
            "w_in": w_in, "w_out": w_out, "lru_conv_w": lru_conv_w, "lru_conv_b": lru_conv_b,
            "lru_w_a": lru_w_a, "lru_b_a": lru_b_a, "lru_w_x": lru_w_x, "lru_b_x": lru_b_x,
            "lru_lambda": lru_lambda, "m_b_i": m_b_i, "m_b_f": m_b_f, "m_norm": m_norm,
            "rw_mu": rw_mu, "rw_w0": rw_w0, "rw_w_up": rw_w_up, "rw_a0": rw_a0, "rw_a_up": rw_a_up,
            "rw_g_up": rw_g_up, "rw_k_k": rw_k_k, "rw_k_a": rw_k_a, "rw_r_k": rw_r_k,
            "rw_ln_w": rw_ln_w, "rw_ln_b": rw_ln_b, "ple_w_proj": ple_w_proj, "ple_w_gate": ple_w_gate}


def reference(x, p, norm_g, ffn_w_in, ffn_w_out, w_in, w_out, lru_conv_w, lru_conv_b, lru_w_a, lru_b_a,
              lru_w_x, lru_b_x, lru_lambda, m_b_i, m_b_f, m_norm, rw_mu, rw_w0, rw_w_up, rw_a0, rw_a_up,
              rw_g_up, rw_k_k, rw_k_a, rw_r_k, rw_ln_w, rw_ln_b, ple_w_proj, ple_w_gate):
    for l in range(DEPTH):
        g = norm_g[l]
        x = x + 0.5 * _rmsnorm(_swiglu(_rmsnorm(x, g[0]), ffn_w_in[l, 0], ffn_w_out[l, 0]), g[1])
        mix = _token_mixer(_rmsnorm(x, g[2]), w_in[l], w_out[l], lru_conv_w[l], lru_conv_b[l], lru_w_a[l],
                           lru_b_a[l], lru_w_x[l], lru_b_x[l], lru_lambda[l], m_b_i[l], m_b_f[l], m_norm[l],
                           rw_mu[l], rw_w0[l], rw_w_up[l], rw_a0[l], rw_a_up[l], rw_g_up[l], rw_k_k[l],
                           rw_k_a[l], rw_r_k[l], rw_ln_w[l], rw_ln_b[l])
        x = x + _rmsnorm(mix, g[3])
        x = x + 0.5 * _rmsnorm(_swiglu(_rmsnorm(x, g[4]), ffn_w_in[l, 1], ffn_w_out[l, 1]), g[5])
        gate = jax.nn.sigmoid(_rmsnorm(x, g[6]) @ ple_w_gate[l])
        x = x + _rmsnorm(gate * (p[l] @ ple_w_proj[l]).astype(gate.dtype), g[7])
    return x
```

```python
import functools

import jax
import jax.numpy as jnp
from jax import lax
from jax.experimental import pallas as pl
from jax.experimental.pallas import tpu as pltpu

F32 = jnp.float32
BF16 = jnp.bfloat16

D_MODEL = 1024
PLE_DIM = 256
D_FF = 2816
RMS_EPS = 1e-6
LRU_WIDTH = 384
LRU_HEADS = 6
CONV_WIDTH = 4
LRU_C = 8.0
MLSTM_HEADS = 4
MLSTM_WIDTH = 256
RWKV_HEADS = 6
RWKV_WIDTH = 384
N_RWKV_COLS = 1408
GN_EPS = 64e-5
HEAD_DIM = 64
CHUNK = 64

ZR_COLS = N_RWKV_COLS
ZL_COLS = 2 * LRU_WIDTH
ZM_COLS = 4 * MLSTM_WIDTH
ZG_COLS = 128
KT_ROWS = MLSTM_WIDTH + 8

VMEM_LIMIT = 56 * 1024 * 1024


def _cparams(sem):
    return pltpu.CompilerParams(dimension_semantics=sem, vmem_limit_bytes=VMEM_LIMIT)


def _rms(x, g):
    return x * lax.rsqrt(jnp.mean(x * x, axis=-1, keepdims=True) + RMS_EPS) * g


def _softplus(x):
    return jnp.maximum(x, 0.0) + jnp.log(1.0 + jnp.exp(-jnp.abs(x)))


def _log_sigmoid(x):
    return -_softplus(-x)


def _dot(a, b):
    return jnp.dot(a.astype(BF16), b.astype(BF16), preferred_element_type=F32)


def _dot_nt(a, b):
    return lax.dot_general(a.astype(BF16), b.astype(BF16), (((1,), (1,)), ((), ())),
                           preferred_element_type=F32)


def _segsum(x, seg):
    hi = x.astype(BF16)
    r1 = x - hi.astype(F32)
    mid = r1.astype(BF16)
    lo = (r1 - mid.astype(F32)).astype(BF16)
    return (jnp.dot(hi, seg, preferred_element_type=F32) + jnp.dot(mid, seg, preferred_element_type=F32)
            + jnp.dot(lo, seg, preferred_element_type=F32))


def _ffn_kernel(x_ref, g_ref, wg_ref, wu_ref, wo_ref, o_ref, xn_ref, acc_ref, *, nf):
    f = pl.program_id(1)

    @pl.when(f == 0)
    def _():
        xn_ref[...] = _rms(x_ref[...], g_ref[0:1, :]).astype(BF16)
        acc_ref[...] = jnp.zeros_like(acc_ref)

    xn = xn_ref[...]
    gate = jnp.dot(xn, wg_ref[...], preferred_element_type=F32)
    up = jnp.dot(xn, wu_ref[...], preferred_element_type=F32)
    h = gate * jax.nn.sigmoid(gate) * up
    acc_ref[...] += jnp.dot(h.astype(BF16), wo_ref[...], preferred_element_type=F32)

    @pl.when(f == nf - 1)
    def _():
        o_ref[...] = x_ref[...] + 0.5 * _rms(acc_ref[...], g_ref[1:2, :])


def _ffn(x, g2, w_in, w_out, *, tm, tf):
    n = x.shape[0]
    nf = D_FF // tf
    return pl.pallas_call(
        functools.partial(_ffn_kernel, nf=nf),
        grid=(n // tm, nf),
        in_specs=[
            pl.BlockSpec((tm, D_MODEL), lambda i, f: (i, 0)),
            pl.BlockSpec((2, D_MODEL), lambda i, f: (0, 0)),
            pl.BlockSpec((D_MODEL, tf), lambda i, f: (0, f)),
            pl.BlockSpec((D_MODEL, tf), lambda i, f: (0, f + nf)),
            pl.BlockSpec((tf, D_MODEL), lambda i, f: (f, 0)),
        ],
        out_specs=pl.BlockSpec((tm, D_MODEL), lambda i, f: (i, 0)),
        out_shape=jax.ShapeDtypeStruct((n, D_MODEL), F32),
        scratch_shapes=[pltpu.VMEM((tm, D_MODEL), BF16), pltpu.VMEM((tm, D_MODEL), F32)],
        compiler_params=_cparams(("parallel", "arbitrary")),
        name="ffn",
    )(x, g2, w_in, w_in, w_out)


def _mixin_kernel(x_ref, g_ref, w_ref, wt_ref, zr_ref, zl_ref, zm_ref, zg_ref, kt_ref):
    xn = _rms(x_ref[...], g_ref[...]).astype(BF16)
    c0, c1, c2 = ZR_COLS, ZR_COLS + ZL_COLS, ZR_COLS + ZL_COLS + ZM_COLS
    zr_ref[...] = jnp.dot(xn, w_ref[:, 0:c0], preferred_element_type=F32)
    zl_ref[...] = jnp.dot(xn, w_ref[:, c0:c1], preferred_element_type=F32)
    zm_ref[...] = jnp.dot(xn, w_ref[:, c1:c2], preferred_element_type=F32)
    zg_ref[...] = jnp.dot(xn, w_ref[:, c2:c2 + ZG_COLS], preferred_element_type=F32)
    kt_ref[...] = lax.dot_general(wt_ref[...], xn, (((1,), (1,)), ((), ())), preferred_element_type=F32)


def _mixer_in(x, g, wcat, wt, *, tm):
    n = x.shape[0]
    ncols = wcat.shape[1]
    row = lambda i: (i, 0)
    return pl.pallas_call(
        _mixin_kernel,
        grid=(n // tm,),
        in_specs=[
            pl.BlockSpec((tm, D_MODEL), row),
            pl.BlockSpec((1, D_MODEL), lambda i: (0, 0)),
            pl.BlockSpec((D_MODEL, ncols), lambda i: (0, 0)),
            pl.BlockSpec((KT_ROWS, D_MODEL), lambda i: (0, 0)),
        ],
        out_specs=[
            pl.BlockSpec((tm, ZR_COLS), row),
            pl.BlockSpec((tm, ZL_COLS), row),
            pl.BlockSpec((tm, ZM_COLS), row),
            pl.BlockSpec((tm, ZG_COLS), row),
            pl.BlockSpec((KT_ROWS, tm), lambda i: (0, i)),
        ],
        out_shape=[
            jax.ShapeDtypeStruct((n, ZR_COLS), F32),
            jax.ShapeDtypeStruct((n, ZL_COLS), F32),
            jax.ShapeDtypeStruct((n, ZM_COLS), F32),
            jax.ShapeDtypeStruct((n, ZG_COLS), F32),
            jax.ShapeDtypeStruct((KT_ROWS, n), F32),
        ],
        compiler_params=_cparams(("parallel",)),
        name="mixer_in",
    )(x, g, wcat, wt)


def _lru_kernel(z_ref, pv_ref, wa_ref, wx_ref, o_ref, xbuf_ref, h_ref, *, tt):
    t = pl.program_id(1)

    @pl.when(t == 0)
    def _():
        xbuf_ref[0:8, :] = jnp.zeros((8, LRU_WIDTH), F32)
        h_ref[...] = jnp.zeros_like(h_ref)

    x = z_ref[:, 0:LRU_WIDTH]
    gate = z_ref[:, LRU_WIDTH:2 * LRU_WIDTH]
    pv = pv_ref[...]
    xbuf_ref[8:8 + tt, :] = x
    xa = pv[4:5, :] + pv[3:4, :] * x
    for j in range(CONV_WIDTH - 1):
        xa = xa + pv[j:j + 1, :] * xbuf_ref[pl.ds(5 + j, tt), :]
    xbuf_ref[0:8, :] = x[tt - 8:tt, :]

    r = jax.nn.sigmoid(_dot(xa, wa_ref[...]) + pv[5:6, :])
    i = jax.nn.sigmoid(_dot(xa, wx_ref[...]) + pv[6:7, :])
    log_a = (-LRU_C) * r * _softplus(-pv[7:8, :])
    a = jnp.exp(log_a)
    u = jnp.sqrt(1.0 - jnp.exp(2.0 * log_a)) * (i * xa)

    row = lax.broadcasted_iota(jnp.int32, (tt, LRU_WIDTH), 0)
    d = 1
    while d < tt:
        keep = row >= d
        a_sh = jnp.where(keep, pltpu.roll(a, d, axis=0), 1.0)
        u_sh = jnp.where(keep, pltpu.roll(u, d, axis=0), 0.0)
        u = a * u_sh + u
        a = a * a_sh
        d *= 2
    h = u + a * h_ref[0:1, :]
    h_ref[...] = jnp.broadcast_to(h[tt - 1:tt, :], h_ref.shape)
    o_ref[...] = h * jax.nn.gelu(gate, approximate=True)


def _lru(zl, pv, wa, wx, *, tt):
    b, t, _ = zl.shape
    full = lambda bb, tt_: (0, 0)
    return pl.pallas_call(
        functools.partial(_lru_kernel, tt=tt),
        grid=(b, t // tt),
        in_specs=[
            pl.BlockSpec((None, tt, ZL_COLS), lambda bb, ti: (bb, ti, 0)),
            pl.BlockSpec((8, LRU_WIDTH), full),
            pl.BlockSpec((LRU_WIDTH, LRU_WIDTH), full),
            pl.BlockSpec((LRU_WIDTH, LRU_WIDTH), full),
        ],
        out_specs=pl.BlockSpec((None, tt, LRU_WIDTH), lambda bb, ti: (bb, ti, 0)),
        out_shape=jax.ShapeDtypeStruct((b, t, LRU_WIDTH), F32),
        scratch_shapes=[pltpu.VMEM((tt + 8, LRU_WIDTH), F32), pltpu.VMEM((8, LRU_WIDTH), F32)],
        compiler_params=_cparams(("parallel", "arbitrary")),
        name="rglru",
    )(zl, pv, wa, wx)


def _mlstm_kernel(q_ref, k_ref, v_ref, og_ref, gc_ref, kt_ref, bcol_ref, brow_ref, nrm_ref, o_ref,
                  cn_ref, m_ref, *, tt):
    t = pl.program_id(1)

    @pl.when(t == 0)
    def _():
        cn_ref[...] = jnp.zeros_like(cn_ref)
        m_ref[...] = jnp.zeros_like(m_ref)

    L = CHUNK
    gr = kt_ref[MLSTM_WIDTH:MLSTM_WIDTH + 8, :] + bcol_ref[:, 0:1]
    br = _log_sigmoid(gr)
    pos = lax.broadcasted_iota(jnp.int32, (8, tt), 1) & (L - 1)
    d = 1
    while d < L:
        br = br + jnp.where(pos >= d, pltpu.roll(br, d, axis=1), 0.0)
        d *= 2
    bc = _log_sigmoid(gc_ref[...] + brow_ref[0:1, :])
    posc = lax.broadcasted_iota(jnp.int32, (tt, ZG_COLS), 0) & (L - 1)
    d = 1
    while d < L:
        bc = bc + jnp.where(posc >= d, pltpu.roll(bc, d, axis=0), 0.0)
        d *= 2

    lane = lax.broadcasted_iota(jnp.int32, (L, 128), 1)
    low = lane < HEAD_DIM
    ones_col = jnp.where(lane == HEAD_DIM, 1.0, 0.0)
    tril = lax.broadcasted_iota(jnp.int32, (L, L), 0) >= lax.broadcasted_iota(jnp.int32, (L, L), 1)

    for c in range(tt // L):
        rs = slice(c * L, (c + 1) * L)
        for pair in range(MLSTM_HEADS // 2):
            ps = slice(128 * pair, 128 * pair + 128)
            v_pair = v_ref[rs, ps]
            halves = []
            for e in range(2):
                h = 2 * pair + e
                hs = slice(HEAD_DIM * h, HEAD_DIM * (h + 1))
                qh = (q_ref[rs, hs] * (HEAD_DIM ** -0.5)).astype(BF16)
                kh = k_ref[rs, hs].astype(BF16)
                vsh = v_pair if e == 0 else pltpu.roll(v_pair, HEAD_DIM, axis=1)
                v_ext = jnp.where(low, vsh, ones_col).astype(BF16)
                b_col = bc[rs, 4 + h:5 + h]
                b_row = br[4 + h:5 + h, rs]
                i_row = gr[h:h + 1, rs]
                m_st = m_ref[h, 0:1, 0:1]
                cn = cn_ref[h]

                dmat = jnp.where(tril, b_col - b_row + i_row, -jnp.inf)
                inter = b_col + m_st
                mj = jnp.maximum(jnp.max(dmat, axis=1, keepdims=True), inter)
                s = jnp.exp(dmat - mj) * _dot_nt(qh, kh)
                sc = jnp.exp(inter - mj)
                numden = _dot(s, v_ext) + sc * _dot(qh, cn)
                den = numden[:, HEAD_DIM:HEAD_DIM + 1]
                hv = numden / jnp.maximum(jnp.abs(den), jnp.exp(-mj))
                ms = jnp.sum(jnp.where(low, hv * hv, 0.0), axis=1, keepdims=True) * (1.0 / HEAD_DIM)
                halves.append(hv * lax.rsqrt(ms + RMS_EPS))

                g = b_row[:, L - 1:L]
                w_log = g - b_row + i_row
                m_new = jnp.maximum(g + m_st, jnp.max(w_log, axis=1, keepdims=True))
                dec = jnp.exp(g + m_st - m_new)
                wk = jnp.exp(w_log - m_new)
                kw = kt_ref[hs, rs] * wk
                cn_ref[h] = dec * cn + _dot(kw, v_ext)
                m_ref[h] = jnp.broadcast_to(m_new, (8, 128))
            hb = jnp.where(low, halves[0], pltpu.roll(halves[1], HEAD_DIM, axis=1))
            o_ref[rs, ps] = hb * nrm_ref[0:1, ps] * jax.nn.sigmoid(og_ref[rs, ps])


def _mlstm(zm, zg, ktg, bcol, brow, nrm, *, tt):
    b, t, _ = zm.shape
    nt = t // tt
    colblk = lambda j: pl.BlockSpec((None, tt, MLSTM_WIDTH), lambda bb, ti, j=j: (bb, ti, j))
    full = lambda bb, ti: (0, 0)
    return pl.pallas_call(
        functools.partial(_mlstm_kernel, tt=tt),
        grid=(b, nt),
        in_specs=[
            colblk(0), colblk(1), colblk(2), colblk(3),
            pl.BlockSpec((None, tt, ZG_COLS), lambda bb, ti: (bb, ti, 0)),
            pl.BlockSpec((KT_ROWS, tt), lambda bb, ti: (0, bb * nt + ti)),
            pl.BlockSpec((8, 128), full),
            pl.BlockSpec((8, 128), full),
            pl.BlockSpec((8, MLSTM_WIDTH), full),
        ],
        out_specs=pl.BlockSpec((None, tt, MLSTM_WIDTH), lambda bb, ti: (bb, ti, 0)),
        out_shape=jax.ShapeDtypeStruct((b, t, MLSTM_WIDTH), F32),
        scratch_shapes=[pltpu.VMEM((MLSTM_HEADS, HEAD_DIM, 128), F32), pltpu.VMEM((MLSTM_HEADS, 8, 128), F32)],
        compiler_params=_cparams(("parallel", "arbitrary")),
        name="mlstm",
    )(zm, zm, zm, zm, zg, ktg, bcol, brow, nrm)


def _rwkv_kernel(z_ref, mu_ref, pv_ref, wup_ref, aup_ref, gup_ref, seg_ref, o_ref,
                 prev_ref, st_ref, y_ref, *, tt):
    t = pl.program_id(1)

    @pl.when(t == 0)
    def _():
        prev_ref[...] = jnp.zeros_like(prev_ref)
        st_ref[...] = jnp.zeros_like(st_ref)

    L = CHUNK
    W = RWKV_WIDTH
    nck = tt // L
    z = z_ref[...]
    row1 = lax.broadcasted_iota(jnp.int32, (tt, 1), 0)
    zprev = jnp.where(row1 == 0, prev_ref[0:1, :], pltpu.roll(z, 1, axis=0))
    prev_ref[...] = jnp.broadcast_to(z[tt - 1:tt, :], prev_ref.shape)
    zs = z + (zprev - z) * mu_ref[...]
    r = zs[:, 0:W]
    k = zs[:, W:2 * W]
    v = zs[:, 2 * W:3 * W]
    wd = zs[:, 3 * W:3 * W + 64]
    ad = zs[:, 3 * W + 64:3 * W + 128]
    gd = zs[:, 3 * W + 128:3 * W + 256]
    pv = pv_ref[...]
    w0, a0, k_k, k_a, r_k, ln_w, ln_b = (pv[j:j + 1, :] for j in range(7))
    seg = seg_ref[...]

    log_w = -_softplus(-(w0 + _dot(jnp.tanh(wd), wup_ref[...]))) - 0.5
    ld = -jnp.exp(log_w)
    iclr = jax.nn.sigmoid(a0 + _dot(ad, aup_ref[...]))
    g = _dot(jax.nn.sigmoid(gd), gup_ref[...])
    kk = k * k_k
    kk = kk / jnp.maximum(jnp.sqrt(_segsum(kk * kk, seg)), 1e-12)
    k_mod = k * (1.0 + (iclr - 1.0) * k_a)
    a_vec = -kk
    b_vec = kk * iclr

    cum = ld
    posr = lax.broadcasted_iota(jnp.int32, (tt, W), 0) & (L - 1)
    d = 1
    while d < L:
        cum = cum + jnp.where(posr >= d, pltpu.roll(cum, d, axis=0), 0.0)
        d *= 2
    cum_last = jnp.concatenate(
        [jnp.broadcast_to(cum[(c + 1) * L - 1:(c + 1) * L, :], (L, W)) for c in range(nck)], axis=0)
    e_neg = jnp.exp(-cum)
    e_tail = jnp.exp(cum_last - cum)
    a_t = jnp.exp(cum - ld) * a_vec
    r_t = jnp.exp(cum) * r
    b_t = e_neg * b_vec
    k_t = e_neg * k_mod
    b_h = e_tail * b_vec
    k_h = e_tail * k_mod

    ri = lax.broadcasted_iota(jnp.int32, (tt, tt), 0)
    ci = lax.broadcasted_iota(jnp.int32, (tt, tt), 1)
    same = (ri // L) == (ci // L)
    strict = same & (ri > ci)
    incl = same & (ri >= ci)
    eye = jnp.where(ri == ci, 1.0, 0.0)

    for pair in range(RWKV_HEADS // 2):
        ps = slice(128 * pair, 128 * pair + 128)
        bh_t = b_h[:, ps].T
        kh_t = k_h[:, ps].T
        cum_t = cum[:, ps].T
        for e in range(2):
            h = 2 * pair + e
            hs = slice(HEAD_DIM * h, HEAD_DIM * (h + 1))
            es = slice(HEAD_DIM * e, HEAD_DIM * (e + 1))
            at_h, rt_h, bt_h, kt_h, v_h = a_t[:, hs], r_t[:, hs], b_t[:, hs], k_t[:, hs], v[:, hs]
            a_ab = jnp.where(strict, _dot_nt(at_h, bt_h), 0.0)
            a_ak = jnp.where(strict, _dot_nt(at_h, kt_h), 0.0)
            a_rb = jnp.where(incl, _dot_nt(rt_h, bt_h), 0.0)
            a_rk = jnp.where(incl, _dot_nt(rt_h, kt_h), 0.0)
            tm = eye + a_ab
            pw = a_ab
            for _ in range(5):
                pw = _dot(pw, pw)
                tm = tm + _dot(pw, tm)
            w_all = _dot(tm, at_h)
            u0_all = _dot(tm, _dot(a_ak, v_h))
            zst = st_ref[h]
            for c in range(nck):
                rs = slice(c * L, (c + 1) * L)
                zb = zst.astype(BF16)
                u = _dot(w_all[rs], zb) + u0_all[rs]
                v_c = v_h[rs]
                y_c = _dot(rt_h[rs], zb) + _dot(a_rb[rs, rs], u) + _dot(a_rk[rs, rs], v_c)
                y_ref[rs, hs] = y_c
                pcol = jnp.exp(cum_t[es, (c + 1) * L - 1:(c + 1) * L])
                zst = pcol * zst + _dot(bh_t[es, rs], u) + _dot(kh_t[es, rs], v_c)
            st_ref[h] = zst

    y = y_ref[...]
    mean = _segsum(y, seg) * (1.0 / HEAD_DIM)
    yc = y - mean
    var = _segsum(yc * yc, seg) * (1.0 / HEAD_DIM)
    yn = yc * lax.rsqrt(var + GN_EPS) * ln_w + ln_b
    bonus = _segsum(r * k_mod * r_k, seg) * v
    o_ref[...] = (yn + bonus) * g


def _rwkv(zr, mu, pv, wup, aup, gup, seg, *, tt):
    b, t, _ = zr.shape
    full = lambda bb, ti: (0, 0)
    return pl.pallas_call(
        functools.partial(_rwkv_kernel, tt=tt),
        grid=(b, t // tt),
        in_specs=[
            pl.BlockSpec((None, tt, N_RWKV_COLS), lambda bb, ti: (bb, ti, 0)),
            pl.BlockSpec((1, N_RWKV_COLS), full),
            pl.BlockSpec((8, RWKV_WIDTH), full),
            pl.BlockSpec((64, RWKV_WIDTH), full),
            pl.BlockSpec((64, RWKV_WIDTH), full),
            pl.BlockSpec((128, RWKV_WIDTH), full),
            pl.BlockSpec((RWKV_WIDTH, RWKV_WIDTH), full),
        ],
        out_specs=pl.BlockSpec((None, tt, RWKV_WIDTH), lambda bb, ti: (bb, ti, 0)),
        out_shape=jax.ShapeDtypeStruct((b, t, RWKV_WIDTH), F32),
        scratch_shapes=[
            pltpu.VMEM((8, N_RWKV_COLS), F32),
            pltpu.VMEM((RWKV_HEADS, HEAD_DIM, HEAD_DIM), F32),
            pltpu.VMEM((tt, RWKV_WIDTH), F32),
        ],
        compiler_params=_cparams(("parallel", "arbitrary")),
        name="rwkv7",
    )(zr, mu, pv, wup, aup, gup, seg)


def _mixout_kernel(x_ref, ya_ref, yb_ref, yc_ref, w_ref, g_ref, o_ref):
    c0, c1 = LRU_WIDTH, LRU_WIDTH + MLSTM_WIDTH
    mix = (jnp.dot(ya_ref[...].astype(BF16), w_ref[0:c0, :], preferred_element_type=F32)
           + jnp.dot(yb_ref[...].astype(BF16), w_ref[c0:c1, :], preferred_element_type=F32)
           + jnp.dot(yc_ref[...].astype(BF16), w_ref[c1:, :], preferred_element_type=F32))
    o_ref[...] = x_ref[...] + _rms(mix, g_ref[...])


def _mixer_out(x, ya, yb, yc, w, g, *, tm):
    n = x.shape[0]
    row = lambda i: (i, 0)
    full = lambda i: (0, 0)
    return pl.pallas_call(
        _mixout_kernel,
        grid=(n // tm,),
        in_specs=[
            pl.BlockSpec((tm, D_MODEL), row),
            pl.BlockSpec((tm, LRU_WIDTH), row),
            pl.BlockSpec((tm, MLSTM_WIDTH), row),
            pl.BlockSpec((tm, RWKV_WIDTH), row),
            pl.BlockSpec((D_MODEL, D_MODEL), full),
            pl.BlockSpec((1, D_MODEL), full),
        ],
        out_specs=pl.BlockSpec((tm, D_MODEL), row),
        out_shape=jax.ShapeDtypeStruct((n, D_MODEL), F32),
        compiler_params=_cparams(("parallel",)),
        name="mixer_out",
    )(x, ya, yb, yc, w, g)


def _ple_kernel(x_ref, p_ref, g_ref, wg_ref, wp_ref, o_ref):
    x = x_ref[...]
    gate = jax.nn.sigmoid(jnp.dot(_rms(x, g_ref[0:1, :]).astype(BF16), wg_ref[...], preferred_element_type=F32))
    pe = jnp.dot(p_ref[...].astype(BF16), wp_ref[...], preferred_element_type=F32)
    o_ref[...] = x + _rms(gate * pe, g_ref[1:2, :])


def _ple(x, p, g2, wg, wp, *, tm):
    n = x.shape[0]
    row = lambda i: (i, 0)
    full = lambda i: (0, 0)
    return pl.pallas_call(
        _ple_kernel,
        grid=(n // tm,),
        in_specs=[
            pl.BlockSpec((tm, D_MODEL), row),
            pl.BlockSpec((tm, PLE_DIM), row),
            pl.BlockSpec((2, D_MODEL), full),
            pl.BlockSpec((D_MODEL, D_MODEL), full),
            pl.BlockSpec((PLE_DIM, D_MODEL), full),
        ],
        out_specs=pl.BlockSpec((tm, D_MODEL), row),
        out_shape=jax.ShapeDtypeStruct((n, D_MODEL), F32),
        compiler_params=_cparams(("parallel",)),
        name="ple",
    )(x, p, g2, wg, wp)


def _block_diag(w):
    nh, dd, _ = w.shape
    eye = jnp.eye(nh, dtype=w.dtype)
    return (eye[:, None, :, None] * w[:, :, None, :]).reshape(nh * dd, nh * dd)


def _tile(n, pref):
    return pref if n % pref == 0 else n


def kernel(x, p, norm_g, ffn_w_in, ffn_w_out, w_in, w_out, lru_conv_w, lru_conv_b, lru_w_a, lru_b_a, lru_w_x, lru_b_x, lru_lambda, m_b_i, m_b_f, m_norm, rw_mu, rw_w0, rw_w_up, rw_a0, rw_a_up, rw_g_up, rw_k_k, rw_k_a, rw_r_k, rw_ln_w, rw_ln_b, ple_w_proj, ple_w_gate):
    bsz, t, _ = x.shape
    n = bsz * t
    depth = norm_g.shape[0]
    tm_ffn = _tile(n, 1024)
    tm = _tile(n, 512)
    tt_lru = _tile(t, 512)
    tt_m = _tile(t, 512)
    tt_r = _tile(t, 256)
    seg = _block_diag(jnp.ones((RWKV_HEADS, HEAD_DIM, HEAD_DIM), BF16))

    xf = x.reshape(n, D_MODEL)
    for l in range(depth):
        g = norm_g[l]
        wl = w_in[l]
        o_zr = 2 * LRU_WIDTH + 4 * MLSTM_WIDTH + 2 * MLSTM_HEADS
        w_gates = wl[:, o_zr - 2 * MLSTM_HEADS:o_zr]
        wcat = jnp.concatenate(
            [wl[:, o_zr:], wl[:, 0:ZL_COLS], wl[:, ZL_COLS:ZL_COLS + ZM_COLS],
             jnp.pad(w_gates, ((0, 0), (0, ZG_COLS - 2 * MLSTM_HEADS)))], axis=1).astype(BF16)
        w_k = wl[:, ZL_COLS + MLSTM_WIDTH:ZL_COLS + 2 * MLSTM_WIDTH]
        wt = jnp.concatenate([w_k, w_gates], axis=1).T.astype(BF16)

        xf = _ffn(xf, g[0:2], ffn_w_in[l, 0].astype(BF16), ffn_w_out[l, 0].astype(BF16), tm=tm_ffn, tf=256)

        zr, zl, zm, zg, ktg = _mixer_in(xf, g[2:3], wcat, wt, tm=tm)

        lru_pv = jnp.concatenate([lru_conv_w[l], lru_conv_b[l][None], lru_b_a[l][None], lru_b_x[l][None],
                                  lru_lambda[l][None]], axis=0)
        ya = _lru(zl.reshape(bsz, t, ZL_COLS), lru_pv, _block_diag(lru_w_a[l]).astype(BF16),
                  _block_diag(lru_w_x[l]).astype(BF16), tt=tt_lru)

        gate_b = jnp.concatenate([m_b_i[l], m_b_f[l]])
        bcol = jnp.broadcast_to(gate_b[:, None], (8, 128))
        brow = jnp.broadcast_to(jnp.pad(gate_b, (0, ZG_COLS - 8))[None, :], (8, ZG_COLS))
        nrm = jnp.broadcast_to(m_norm[l][None, :], (8, MLSTM_WIDTH))
        yb = _mlstm(zm.reshape(bsz, t, ZM_COLS), zg.reshape(bsz, t, ZG_COLS), ktg, bcol, brow, nrm, tt=tt_m)

        rw_pv = jnp.stack([rw_w0[l], rw_a0[l], rw_k_k[l], rw_k_a[l], rw_r_k[l].reshape(-1), rw_ln_w[l],
                           rw_ln_b[l], jnp.zeros((RWKV_WIDTH,), F32)], axis=0)
        yc = _rwkv(zr.reshape(bsz, t, ZR_COLS), rw_mu[l][None, :], rw_pv, rw_w_up[l].astype(BF16),
                   rw_a_up[l].astype(BF16), rw_g_up[l].astype(BF16), seg, tt=tt_r)

        xf = _mixer_out(xf, ya.reshape(n, LRU_WIDTH), yb.reshape(n, MLSTM_WIDTH), yc.reshape(n, RWKV_WIDTH),
                        w_out[l].astype(BF16), g[3:4], tm=tm)
        xf = _ffn(xf, g[4:6], ffn_w_in[l, 1].astype(BF16), ffn_w_out[l, 1].astype(BF16), tm=tm_ffn, tf=256)
        xf = _ple(xf, p[l].reshape(n, PLE_DIM), g[6:8], ple_w_gate[l].astype(BF16), ple_w_proj[l].astype(BF16), tm=tm)
    return xf.reshape(bsz, t, D_MODEL)
```

```python
import functools

import jax
import jax.numpy as jnp
from jax import lax
from jax.experimental import pallas as pl
from jax.experimental.pallas import tpu as pltpu

F32 = jnp.float32
BF16 = jnp.bfloat16

D_MODEL = 1024
PLE_DIM = 256
D_FF = 2816
RMS_EPS = 1e-6
LRU_WIDTH = 384
LRU_HEADS = 6
CONV_WIDTH = 4
LRU_C = 8.0
MLSTM_HEADS = 4
MLSTM_WIDTH = 256
RWKV_HEADS = 6
RWKV_WIDTH = 384
N_RWKV_COLS = 1408
GN_EPS = 64e-5
HEAD_DIM = 64
CHUNK = 64

ZR_COLS = N_RWKV_COLS
ZL_COLS = 2 * LRU_WIDTH
ZM_COLS = 4 * MLSTM_WIDTH
ZG_COLS = 128
KT_ROWS = MLSTM_WIDTH + 8

VMEM_LIMIT = 56 * 1024 * 1024


def _cparams(sem):
    return pltpu.CompilerParams(dimension_semantics=sem, vmem_limit_bytes=VMEM_LIMIT)


def _rms(x, g):
    return x * lax.rsqrt(jnp.mean(x * x, axis=-1, keepdims=True) + RMS_EPS) * g


def _softplus(x):
    return jnp.maximum(x, 0.0) + jnp.log(1.0 + jnp.exp(-jnp.abs(x)))


def _log_sigmoid(x):
    return -_softplus(-x)


def _dot(a, b):
    return jnp.dot(a.astype(BF16), b.astype(BF16), preferred_element_type=F32)


def _dot_nt(a, b):
    return lax.dot_general(a.astype(BF16), b.astype(BF16), (((1,), (1,)), ((), ())),
                           preferred_element_type=F32)


def _segsum(x, seg):
    hi = x.astype(BF16)
    r1 = x - hi.astype(F32)
    mid = r1.astype(BF16)
    lo = (r1 - mid.astype(F32)).astype(BF16)
    return (jnp.dot(hi, seg, preferred_element_type=F32) + jnp.dot(mid, seg, preferred_element_type=F32)
            + jnp.dot(lo, seg, preferred_element_type=F32))


def _ffn_kernel(x_ref, g_ref, wi_ref, wo_ref, o_ref, h_ref, *, tf):
    x = x_ref[...]
    xn = _rms(x, g_ref[0:1, :]).astype(BF16)
    for c in range(D_FF // tf):
        gate = jnp.dot(xn, wi_ref[:, c * tf:(c + 1) * tf], preferred_element_type=F32)
        up = jnp.dot(xn, wi_ref[:, D_FF + c * tf:D_FF + (c + 1) * tf], preferred_element_type=F32)
        h_ref[:, c * tf:(c + 1) * tf] = (gate * jax.nn.sigmoid(gate) * up).astype(BF16)
    y = jnp.dot(h_ref[...], wo_ref[...], preferred_element_type=F32)
    o_ref[...] = x + 0.5 * _rms(y, g_ref[1:2, :])


def _ffn(x, g2, w_in, w_out, *, tm, tf):
    n = x.shape[0]
    const = dict(pipeline_mode=pl.Buffered(1))
    return pl.pallas_call(
        functools.partial(_ffn_kernel, tf=tf),
        grid=(n // tm,),
        in_specs=[
            pl.BlockSpec((tm, D_MODEL), lambda i: (i, 0)),
            pl.BlockSpec((2, D_MODEL), lambda i: (0, 0)),
            pl.BlockSpec((D_MODEL, 2 * D_FF), lambda i: (0, 0), **const),
            pl.BlockSpec((D_FF, D_MODEL), lambda i: (0, 0), **const),
        ],
        out_specs=pl.BlockSpec((tm, D_MODEL), lambda i: (i, 0)),
        out_shape=jax.ShapeDtypeStruct((n, D_MODEL), F32),
        scratch_shapes=[pltpu.VMEM((tm, D_FF), BF16)],
        compiler_params=_cparams(("parallel",)),
        name="ffn",
    )(x, g2, w_in, w_out)


def _mixin_kernel(x_ref, g_ref, w_ref, wt_ref, zr_ref, zl_ref, zm_ref, zg_ref, kt_ref):
    xn = _rms(x_ref[...], g_ref[...]).astype(BF16)
    c0, c1, c2 = ZR_COLS, ZR_COLS + ZL_COLS, ZR_COLS + ZL_COLS + ZM_COLS
    zr_ref[...] = jnp.dot(xn, w_ref[:, 0:c0], preferred_element_type=F32)
    zl_ref[...] = jnp.dot(xn, w_ref[:, c0:c1], preferred_element_type=F32)
    zm_ref[...] = jnp.dot(xn, w_ref[:, c1:c2], preferred_element_type=F32)
    zg_ref[...] = jnp.dot(xn, w_ref[:, c2:c2 + ZG_COLS], preferred_element_type=F32)
    kt_ref[...] = lax.dot_general(wt_ref[...], xn, (((1,), (1,)), ((), ())), preferred_element_type=F32)


def _mixer_in(x, g, wcat, wt, *, tm):
    n = x.shape[0]
    ncols = wcat.shape[1]
    row = lambda i: (i, 0)
    return pl.pallas_call(
        _mixin_kernel,
        grid=(n // tm,),
        in_specs=[
            pl.BlockSpec((tm, D_MODEL), row),
            pl.BlockSpec((1, D_MODEL), lambda i: (0, 0)),
            pl.BlockSpec((D_MODEL, ncols), lambda i: (0, 0)),
            pl.BlockSpec((KT_ROWS, D_MODEL), lambda i: (0, 0)),
        ],
        out_specs=[
            pl.BlockSpec((tm, ZR_COLS), row),
            pl.BlockSpec((tm, ZL_COLS), row),
            pl.BlockSpec((tm, ZM_COLS), row),
            pl.BlockSpec((tm, ZG_COLS), row),
            pl.BlockSpec((KT_ROWS, tm), lambda i: (0, i)),
        ],
        out_shape=[
            jax.ShapeDtypeStruct((n, ZR_COLS), F32),
            jax.ShapeDtypeStruct((n, ZL_COLS), F32),
            jax.ShapeDtypeStruct((n, ZM_COLS), F32),
            jax.ShapeDtypeStruct((n, ZG_COLS), F32),
            jax.ShapeDtypeStruct((KT_ROWS, n), F32),
        ],
        compiler_params=_cparams(("parallel",)),
        name="mixer_in",
    )(x, g, wcat, wt)


def _lru_kernel(z_ref, pv_ref, wa_ref, wx_ref, o_ref, xbuf_ref, h_ref, *, tt):
    t = pl.program_id(1)

    @pl.when(t == 0)
    def _():
        xbuf_ref[0:8, :] = jnp.zeros((8, LRU_WIDTH), F32)
        h_ref[...] = jnp.zeros_like(h_ref)

    x = z_ref[:, 0:LRU_WIDTH]
    gate = z_ref[:, LRU_WIDTH:2 * LRU_WIDTH]
    pv = pv_ref[...]
    xbuf_ref[8:8 + tt, :] = x
    xa = pv[4:5, :] + pv[3:4, :] * x
    for j in range(CONV_WIDTH - 1):
        xa = xa + pv[j:j + 1, :] * xbuf_ref[pl.ds(5 + j, tt), :]
    xbuf_ref[0:8, :] = x[tt - 8:tt, :]

    r = jax.nn.sigmoid(_dot(xa, wa_ref[...]) + pv[5:6, :])
    i = jax.nn.sigmoid(_dot(xa, wx_ref[...]) + pv[6:7, :])
    log_a = (-LRU_C) * r * _softplus(-pv[7:8, :])
    a = jnp.exp(log_a)
    u = jnp.sqrt(1.0 - jnp.exp(2.0 * log_a)) * (i * xa)

    row = lax.broadcasted_iota(jnp.int32, (tt, LRU_WIDTH), 0)
    d = 1
    while d < tt:
        keep = row >= d
        a_sh = jnp.where(keep, pltpu.roll(a, d, axis=0), 1.0)
        u_sh = jnp.where(keep, pltpu.roll(u, d, axis=0), 0.0)
        u = a * u_sh + u
        a = a * a_sh
        d *= 2
    h = u + a * h_ref[0:1, :]
    h_ref[...] = jnp.broadcast_to(h[tt - 1:tt, :], h_ref.shape)
    o_ref[...] = h * jax.nn.gelu(gate, approximate=True)


def _lru(zl, pv, wa, wx, *, tt):
    b, t, _ = zl.shape
    full = lambda bb, tt_: (0, 0)
    return pl.pallas_call(
        functools.partial(_lru_kernel, tt=tt),
        grid=(b, t // tt),
        in_specs=[
            pl.BlockSpec((None, tt, ZL_COLS), lambda bb, ti: (bb, ti, 0)),
            pl.BlockSpec((8, LRU_WIDTH), full),
            pl.BlockSpec((LRU_WIDTH, LRU_WIDTH), full),
            pl.BlockSpec((LRU_WIDTH, LRU_WIDTH), full),
        ],
        out_specs=pl.BlockSpec((None, tt, LRU_WIDTH), lambda bb, ti: (bb, ti, 0)),
        out_shape=jax.ShapeDtypeStruct((b, t, LRU_WIDTH), F32),
        scratch_shapes=[pltpu.VMEM((tt + 8, LRU_WIDTH), F32), pltpu.VMEM((8, LRU_WIDTH), F32)],
        compiler_params=_cparams(("parallel", "arbitrary")),
        name="rglru",
    )(zl, pv, wa, wx)


def _mlstm_kernel(q_ref, k_ref, v_ref, og_ref, gc_ref, kt_ref, bcol_ref, brow_ref, nrm_ref, o_ref,
                  cn_ref, m_ref, *, tt):
    t = pl.program_id(1)

    @pl.when(t == 0)
    def _():
        cn_ref[...] = jnp.zeros_like(cn_ref)
        m_ref[...] = jnp.zeros_like(m_ref)

    L = CHUNK
    gr = kt_ref[MLSTM_WIDTH:MLSTM_WIDTH + 8, :] + bcol_ref[:, 0:1]
    br = _log_sigmoid(gr)
    pos = lax.broadcasted_iota(jnp.int32, (8, tt), 1) & (L - 1)
    d = 1
    while d < L:
        br = br + jnp.where(pos >= d, pltpu.roll(br, d, axis=1), 0.0)
        d *= 2
    bc = _log_sigmoid(gc_ref[...] + brow_ref[0:1, :])
    posc = lax.broadcasted_iota(jnp.int32, (tt, ZG_COLS), 0) & (L - 1)
    d = 1
    while d < L:
        bc = bc + jnp.where(posc >= d, pltpu.roll(bc, d, axis=0), 0.0)
        d *= 2

    nck = tt // L
    heads = range(MLSTM_HEADS)
    units = [(c, h) for c in range(nck) for h in heads]
    rsl = [slice(c * L, (c + 1) * L) for c in range(nck)]
    hsl = [slice(HEAD_DIM * h, HEAD_DIM * (h + 1)) for h in heads]
    psl = [slice(128 * (h // 2), 128 * (h // 2) + 128) for h in heads]
    jsl = [slice(128 * (c // 2), 128 * (c // 2) + 128) for c in range(nck)]

    lane = lax.broadcasted_iota(jnp.int32, (L, 128), 1)
    rowi = lax.broadcasted_iota(jnp.int32, (L, 128), 0)
    half = [lane < HEAD_DIM, lane >= HEAD_DIM]
    den_lane = [HEAD_DIM, HEAD_DIM - 1]
    ones_col = [jnp.where(lane == den_lane[e], 1.0, 0.0) for e in range(2)]
    cmask = [(lane >= L * cc) & (lane < L * (cc + 1)) & (lane - L * cc <= rowi) for cc in range(2)]
    lane1 = lax.broadcasted_iota(jnp.int32, (1, 128), 1)
    rmask = [lane1 < L, lane1 >= L]

    b_row = {(c, h): br[4 + h:5 + h, jsl[c]] for c, h in units}
    i_row = {(c, h): gr[h:h + 1, jsl[c]] for c, h in units}

    g = {(c, h): br[4 + h:5 + h, (c + 1) * L - 1:(c + 1) * L] for c, h in units}
    w_log = {u: jnp.where(rmask[u[0] % 2], g[u] - b_row[u] + i_row[u], -jnp.inf) for u in units}
    w_max = {u: jnp.max(w_log[u], axis=1, keepdims=True) for u in units}
    m_in, m_out = {}, {}
    for h in heads:
        m_st = m_ref[h, 0:1, 0:1]
        for c in range(nck):
            m_in[c, h] = m_st
            m_st = jnp.maximum(g[c, h] + m_st, w_max[c, h])
            m_out[c, h] = m_st
        m_ref[h] = jnp.broadcast_to(m_st, (8, 128))
    dec = {u: jnp.exp(g[u] + m_in[u] - m_out[u]) for u in units}
    wk = {u: jnp.exp(w_log[u] - m_out[u]) for u in units}

    q_m = {(c, h): (jnp.where(half[h % 2], q_ref[rsl[c], psl[h]], 0.0) * (HEAD_DIM ** -0.5)).astype(BF16)
           for c, h in units}
    k2 = {(c, h): k_ref[jsl[c], psl[h]].astype(BF16) for c, h in units}
    vx = {(c, h): jnp.where(jnp.concatenate([half[h % 2]] * 2, axis=0), v_ref[jsl[c], psl[h]],
                            jnp.concatenate([ones_col[h % 2]] * 2, axis=0)).astype(BF16) for c, h in units}
    bcb = {(c, h): jnp.broadcast_to(bc[rsl[c], 4 + h:5 + h], (L, 128)) for c, h in units}
    dmat = {u: jnp.where(cmask[u[0] % 2], bcb[u] - b_row[u] + i_row[u], -jnp.inf) for u in units}
    m_loc = {u: jnp.max(dmat[u], axis=1, keepdims=True) for u in units}
    p_in = {u: jnp.exp(dmat[u] - m_loc[u]) for u in units}
    qk = {u: _dot_nt(q_m[u], k2[u]) for u in units}
    nd = {u: _dot(p_in[u] * qk[u], vx[u]) for u in units}
    kv = {(c, h): _dot(kt_ref[hsl[h], jsl[c]] * wk[c, h], vx[c, h]) for c, h in units}

    cn_in = {}
    for h in heads:
        cn = cn_ref[h]
        for c in range(nck):
            cn_in[c, h] = cn
            cn = dec[c, h] * cn + kv[c, h]
        cn_ref[h] = cn
    cn_pair = {(c, p): jnp.concatenate([cn_in[c, 2 * p], cn_in[c, 2 * p + 1]], axis=0).astype(BF16)
               for c in range(nck) for p in range(MLSTM_HEADS // 2)}

    inter = {u: bcb[u] + m_in[u] for u in units}
    mj = {u: jnp.maximum(m_loc[u], inter[u]) for u in units}
    e_loc = {u: jnp.exp(m_loc[u] - mj[u]) for u in units}
    e_int = {u: jnp.exp(inter[u] - mj[u]) for u in units}
    e_neg = {u: jnp.exp(-mj[u]) for u in units}
    qc = {(c, h): _dot(q_m[c, h], cn_pair[c, h // 2]) for c, h in units}
    numden = {u: e_loc[u] * nd[u] + e_int[u] * qc[u] for u in units}
    den = {(c, h): numden[c, h][:, den_lane[h % 2]:den_lane[h % 2] + 1] for c, h in units}
    hv = {u: numden[u] / jnp.maximum(jnp.abs(den[u]), e_neg[u]) for u in units}
    ms = {(c, h): jnp.sum(jnp.where(half[h % 2], hv[c, h] * hv[c, h], 0.0), axis=1, keepdims=True)
          * (1.0 / HEAD_DIM) for c, h in units}
    hn = {u: hv[u] * lax.rsqrt(ms[u] + RMS_EPS) for u in units}
    for c in range(nck):
        for p in range(MLSTM_HEADS // 2):
            ps = psl[2 * p]
            hb = jnp.where(half[0], hn[c, 2 * p], hn[c, 2 * p + 1])
            o_ref[rsl[c], ps] = hb * nrm_ref[0:1, ps] * jax.nn.sigmoid(og_ref[rsl[c], ps])


def _mlstm(zm, zg, ktg, bcol, brow, nrm, *, tt):
    b, t, _ = zm.shape
    nt = t // tt
    colblk = lambda j: pl.BlockSpec((None, tt, MLSTM_WIDTH), lambda bb, ti, j=j: (bb, ti, j))
    full = lambda bb, ti: (0, 0)
    return pl.pallas_call(
        functools.partial(_mlstm_kernel, tt=tt),
        grid=(b, nt),
        in_specs=[
            colblk(0), colblk(1), colblk(2), colblk(3),
            pl.BlockSpec((None, tt, ZG_COLS), lambda bb, ti: (bb, ti, 0)),
            pl.BlockSpec((KT_ROWS, tt), lambda bb, ti: (0, bb * nt + ti)),
            pl.BlockSpec((8, 128), full),
            pl.BlockSpec((8, 128), full),
            pl.BlockSpec((8, MLSTM_WIDTH), full),
        ],
        out_specs=pl.BlockSpec((None, tt, MLSTM_WIDTH), lambda bb, ti: (bb, ti, 0)),
        out_shape=jax.ShapeDtypeStruct((b, t, MLSTM_WIDTH), F32),
        scratch_shapes=[pltpu.VMEM((MLSTM_HEADS, HEAD_DIM, 128), F32), pltpu.VMEM((MLSTM_HEADS, 8, 128), F32)],
        compiler_params=_cparams(("parallel", "arbitrary")),
        name="mlstm",
    )(zm, zm, zm, zm, zg, ktg, bcol, brow, nrm)


def _rwkv_kernel(z_ref, mu_ref, pv_ref, wup_ref, aup_ref, gup_ref, seg_ref, o_ref,
                 prev_ref, st_ref, y_ref, *, tt):
    t = pl.program_id(1)

    @pl.when(t == 0)
    def _():
        prev_ref[...] = jnp.zeros_like(prev_ref)
        st_ref[...] = jnp.zeros_like(st_ref)

    L = CHUNK
    W = RWKV_WIDTH
    nck = tt // L
    z = z_ref[...]
    row1 = lax.broadcasted_iota(jnp.int32, (tt, 1), 0)
    zprev = jnp.where(row1 == 0, prev_ref[0:1, :], pltpu.roll(z, 1, axis=0))
    prev_ref[...] = jnp.broadcast_to(z[tt - 1:tt, :], prev_ref.shape)
    zs = z + (zprev - z) * mu_ref[...]
    r = zs[:, 0:W]
    k = zs[:, W:2 * W]
    v = zs[:, 2 * W:3 * W]
    wd = zs[:, 3 * W:3 * W + 64]
    ad = zs[:, 3 * W + 64:3 * W + 128]
    gd = zs[:, 3 * W + 128:3 * W + 256]
    pv = pv_ref[...]
    w0, a0, k_k, k_a, r_k, ln_w, ln_b = (pv[j:j + 1, :] for j in range(7))
    seg = seg_ref[...]

    log_w = -_softplus(-(w0 + _dot(jnp.tanh(wd), wup_ref[...]))) - 0.5
    ld = -jnp.exp(log_w)
    iclr = jax.nn.sigmoid(a0 + _dot(ad, aup_ref[...]))
    g = _dot(jax.nn.sigmoid(gd), gup_ref[...])
    kk = k * k_k
    kk = kk / jnp.maximum(jnp.sqrt(_segsum(kk * kk, seg)), 1e-12)
    k_mod = k * (1.0 + (iclr - 1.0) * k_a)
    a_vec = -kk
    b_vec = kk * iclr

    cum = ld
    posr = lax.broadcasted_iota(jnp.int32, (tt, W), 0) & (L - 1)
    d = 1
    while d < L:
        cum = cum + jnp.where(posr >= d, pltpu.roll(cum, d, axis=0), 0.0)
        d *= 2
    cum_last = jnp.concatenate(
        [jnp.broadcast_to(cum[(c + 1) * L - 1:(c + 1) * L, :], (L, W)) for c in range(nck)], axis=0)
    e_neg = jnp.exp(-cum)
    e_tail = jnp.exp(cum_last - cum)
    a_t = jnp.exp(cum - ld) * a_vec
    r_t = jnp.exp(cum) * r
    b_t = e_neg * b_vec
    k_t = e_neg * k_mod
    b_h = e_tail * b_vec
    k_h = e_tail * k_mod

    UR = 2 * L
    ri = lax.broadcasted_iota(jnp.int32, (UR, UR), 0)
    ci = lax.broadcasted_iota(jnp.int32, (UR, UR), 1)
    same = (ri // L) == (ci // L)
    strict = jnp.where(same & (ri > ci), 1.0, 0.0)
    incl = jnp.where(same & (ri >= ci), 1.0, 0.0)
    eye = jnp.where(ri == ci, 1.0, 0.0)

    heads = range(RWKV_HEADS)
    hsl = [slice(HEAD_DIM * h, HEAD_DIM * (h + 1)) for h in heads]
    bh_t, kh_t, cum_t = [], [], []
    for pair in range(RWKV_HEADS // 2):
        ps = slice(128 * pair, 128 * pair + 128)
        bp, kp, cp = b_h[:, ps].T.astype(BF16), k_h[:, ps].T.astype(BF16), cum[:, ps].T
        for e in range(2):
            es = slice(HEAD_DIM * e, HEAD_DIM * (e + 1))
            bh_t.append(bp[es])
            kh_t.append(kp[es])
            cum_t.append(cp[es])

    units = [(h, u) for u in range(tt // UR) for h in heads]
    nu = range(len(units))
    usl = [slice(UR * u, UR * (u + 1)) for _, u in units]
    at = [a_t[usl[i], hsl[h]].astype(BF16) for i, (h, _) in enumerate(units)]
    rt = [r_t[usl[i], hsl[h]] for i, (h, _) in enumerate(units)]
    bt = [b_t[usl[i], hsl[h]].astype(BF16) for i, (h, _) in enumerate(units)]
    kt = [k_t[usl[i], hsl[h]].astype(BF16) for i, (h, _) in enumerate(units)]
    vu = [v[usl[i], hsl[h]].astype(BF16) for i, (h, _) in enumerate(units)]
    a_ab = [strict * _dot_nt(at[i], bt[i]) for i in nu]
    a_ak = [(strict * _dot_nt(at[i], kt[i])).astype(BF16) for i in nu]
    a_rb = [(incl * _dot_nt(rt[i], bt[i])).astype(BF16) for i in nu]
    a_rk = [(incl * _dot_nt(rt[i], kt[i])).astype(BF16) for i in nu]
    tm = [eye + a_ab[i] for i in nu]
    pw = [a_ab[i].astype(BF16) for i in nu]
    for _ in range(5):
        pw = [_dot(pw[i], pw[i]).astype(BF16) for i in nu]
        tm = [tm[i] + _dot(pw[i], tm[i]) for i in nu]
    akv = [_dot(a_ak[i], vu[i]) for i in nu]
    tmb = [tm[i].astype(BF16) for i in nu]
    w_all = [_dot(tmb[i], at[i]).astype(BF16) for i in nu]
    u0_all = [_dot(tmb[i], akv[i]).astype(BF16) for i in nu]
    q_all = [(rt[i] + _dot(a_rb[i], w_all[i])).astype(BF16) for i in nu]
    y0_all = [_dot(a_rb[i], u0_all[i]) + _dot(a_rk[i], vu[i]) for i in nu]
    gm, hm, pcol, qc, y0c = {}, {}, {}, {}, {}
    for i, (h, u) in enumerate(units):
        for cc in range(UR // L):
            c = u * (UR // L) + cc
            ls = slice(cc * L, (cc + 1) * L)
            rs = slice(c * L, (c + 1) * L)
            gm[c, h] = _dot(bh_t[h][:, rs], w_all[i][ls]).astype(BF16)
            hm[c, h] = _dot(bh_t[h][:, rs], u0_all[i][ls]) + _dot(kh_t[h][:, rs], vu[i][ls])
            pcol[c, h] = jnp.exp(cum_t[h][:, (c + 1) * L - 1:(c + 1) * L])
            qc[c, h] = q_all[i][ls]
            y0c[c, h] = y0_all[i][ls]
    zst = [st_ref[h] for h in heads]
    for c in range(nck):
        rs = slice(c * L, (c + 1) * L)
        for h in heads:
            zb = zst[h].astype(BF16)
            y_ref[rs, hsl[h]] = _dot(qc[c, h], zb) + y0c[c, h]
            zst[h] = pcol[c, h] * zst[h] + _dot(gm[c, h], zb) + hm[c, h]
    for h in heads:
        st_ref[h] = zst[h]

    y = y_ref[...]
    mean = _segsum(y, seg) * (1.0 / HEAD_DIM)
    yc = y - mean
    var = _segsum(yc * yc, seg) * (1.0 / HEAD_DIM)
    yn = yc * lax.rsqrt(var + GN_EPS) * ln_w + ln_b
    bonus = _segsum(r * k_mod * r_k, seg) * v
    o_ref[...] = (yn + bonus) * g


def _rwkv(zr, mu, pv, wup, aup, gup, seg, *, tt):
    b, t, _ = zr.shape
    full = lambda bb, ti: (0, 0)
    return pl.pallas_call(
        functools.partial(_rwkv_kernel, tt=tt),
        grid=(b, t // tt),
        in_specs=[
            pl.BlockSpec((None, tt, N_RWKV_COLS), lambda bb, ti: (bb, ti, 0)),
            pl.BlockSpec((1, N_RWKV_COLS), full),
            pl.BlockSpec((8, RWKV_WIDTH), full),
            pl.BlockSpec((64, RWKV_WIDTH), full),
            pl.BlockSpec((64, RWKV_WIDTH), full),
            pl.BlockSpec((128, RWKV_WIDTH), full),
            pl.BlockSpec((RWKV_WIDTH, RWKV_WIDTH), full),
        ],
        out_specs=pl.BlockSpec((None, tt, RWKV_WIDTH), lambda bb, ti: (bb, ti, 0)),
        out_shape=jax.ShapeDtypeStruct((b, t, RWKV_WIDTH), F32),
        scratch_shapes=[
            pltpu.VMEM((8, N_RWKV_COLS), F32),
            pltpu.VMEM((RWKV_HEADS, HEAD_DIM, HEAD_DIM), F32),
            pltpu.VMEM((tt, RWKV_WIDTH), F32),
        ],
        compiler_params=_cparams(("parallel", "arbitrary")),
        name="rwkv7",
    )(zr, mu, pv, wup, aup, gup, seg)


def _mixout_kernel(x_ref, ya_ref, yb_ref, yc_ref, w_ref, g_ref, o_ref):
    c0, c1 = LRU_WIDTH, LRU_WIDTH + MLSTM_WIDTH
    mix = (jnp.dot(ya_ref[...].astype(BF16), w_ref[0:c0, :], preferred_element_type=F32)
           + jnp.dot(yb_ref[...].astype(BF16), w_ref[c0:c1, :], preferred_element_type=F32)
           + jnp.dot(yc_ref[...].astype(BF16), w_ref[c1:, :], preferred_element_type=F32))
    o_ref[...] = x_ref[...] + _rms(mix, g_ref[...])


def _mixer_out(x, ya, yb, yc, w, g, *, tm):
    n = x.shape[0]
    row = lambda i: (i, 0)
    full = lambda i: (0, 0)
    return pl.pallas_call(
        _mixout_kernel,
        grid=(n // tm,),
        in_specs=[
            pl.BlockSpec((tm, D_MODEL), row),
            pl.BlockSpec((tm, LRU_WIDTH), row),
            pl.BlockSpec((tm, MLSTM_WIDTH), row),
            pl.BlockSpec((tm, RWKV_WIDTH), row),
            pl.BlockSpec((D_MODEL, D_MODEL), full),
            pl.BlockSpec((1, D_MODEL), full),
        ],
        out_specs=pl.BlockSpec((tm, D_MODEL), row),
        out_shape=jax.ShapeDtypeStruct((n, D_MODEL), F32),
        compiler_params=_cparams(("parallel",)),
        name="mixer_out",
    )(x, ya, yb, yc, w, g)


def _ple_kernel(x_ref, p_ref, g_ref, wg_ref, wp_ref, o_ref):
    x = x_ref[...]
    gate = jax.nn.sigmoid(jnp.dot(_rms(x, g_ref[0:1, :]).astype(BF16), wg_ref[...], preferred_element_type=F32))
    pe = jnp.dot(p_ref[...].astype(BF16), wp_ref[...], preferred_element_type=F32)
    o_ref[...] = x + _rms(gate * pe, g_ref[1:2, :])


def _ple(x, p, g2, wg, wp, *, tm):
    n = x.shape[0]
    row = lambda i: (i, 0)
    full = lambda i: (0, 0)
    return pl.pallas_call(
        _ple_kernel,
        grid=(n // tm,),
        in_specs=[
            pl.BlockSpec((tm, D_MODEL), row),
            pl.BlockSpec((tm, PLE_DIM), row),
            pl.BlockSpec((2, D_MODEL), full),
            pl.BlockSpec((D_MODEL, D_MODEL), full),
            pl.BlockSpec((PLE_DIM, D_MODEL), full),
        ],
        out_specs=pl.BlockSpec((tm, D_MODEL), row),
        out_shape=jax.ShapeDtypeStruct((n, D_MODEL), F32),
        compiler_params=_cparams(("parallel",)),
        name="ple",
    )(x, p, g2, wg, wp)


def _block_diag(w):
    nh, dd, _ = w.shape
    eye = jnp.eye(nh, dtype=w.dtype)
    return (eye[:, None, :, None] * w[:, :, None, :]).reshape(nh * dd, nh * dd)


def _tile(n, pref):
    return pref if n % pref == 0 else n


def kernel(x, p, norm_g, ffn_w_in, ffn_w_out, w_in, w_out, lru_conv_w, lru_conv_b, lru_w_a, lru_b_a, lru_w_x, lru_b_x, lru_lambda, m_b_i, m_b_f, m_norm, rw_mu, rw_w0, rw_w_up, rw_a0, rw_a_up, rw_g_up, rw_k_k, rw_k_a, rw_r_k, rw_ln_w, rw_ln_b, ple_w_proj, ple_w_gate):
    bsz, t, _ = x.shape
    n = bsz * t
    depth = norm_g.shape[0]
    tm_ffn = _tile(n, 512)
    tm = _tile(n, 512)
    tt_lru = _tile(t, 512)
    tt_m = _tile(t, 512)
    tt_r = _tile(t, 256)
    seg = _block_diag(jnp.ones((RWKV_HEADS, HEAD_DIM, HEAD_DIM), BF16))

    xf = x.reshape(n, D_MODEL)
    for l in range(depth):
        g = norm_g[l]
        wl = w_in[l]
        o_zr = 2 * LRU_WIDTH + 4 * MLSTM_WIDTH + 2 * MLSTM_HEADS
        w_gates = wl[:, o_zr - 2 * MLSTM_HEADS:o_zr]
        wcat = jnp.concatenate(
            [wl[:, o_zr:], wl[:, 0:ZL_COLS], wl[:, ZL_COLS:ZL_COLS + ZM_COLS],
             jnp.pad(w_gates, ((0, 0), (0, ZG_COLS - 2 * MLSTM_HEADS)))], axis=1).astype(BF16)
        w_k = wl[:, ZL_COLS + MLSTM_WIDTH:ZL_COLS + 2 * MLSTM_WIDTH]
        wt = jnp.concatenate([w_k, w_gates], axis=1).T.astype(BF16)

        xf = _ffn(xf, g[0:2], ffn_w_in[l, 0].astype(BF16), ffn_w_out[l, 0].astype(BF16), tm=tm_ffn, tf=256)

        zr, zl, zm, zg, ktg = _mixer_in(xf, g[2:3], wcat, wt, tm=tm)

        lru_pv = jnp.concatenate([lru_conv_w[l], lru_conv_b[l][None], lru_b_a[l][None], lru_b_x[l][None],
                                  lru_lambda[l][None]], axis=0)
        ya = _lru(zl.reshape(bsz, t, ZL_COLS), lru_pv, _block_diag(lru_w_a[l]).astype(BF16),
                  _block_diag(lru_w_x[l]).astype(BF16), tt=tt_lru)

        gate_b = jnp.concatenate([m_b_i[l], m_b_f[l]])
        bcol = jnp.broadcast_to(gate_b[:, None], (8, 128))
        brow = jnp.broadcast_to(jnp.pad(gate_b, (0, ZG_COLS - 8))[None, :], (8, ZG_COLS))
        nrm = jnp.broadcast_to(m_norm[l][None, :], (8, MLSTM_WIDTH))
        yb = _mlstm(zm.reshape(bsz, t, ZM_COLS), zg.reshape(bsz, t, ZG_COLS), ktg, bcol, brow, nrm, tt=tt_m)

        rw_pv = jnp.stack([rw_w0[l], rw_a0[l], rw_k_k[l], rw_k_a[l], rw_r_k[l].reshape(-1), rw_ln_w[l],
                           rw_ln_b[l], jnp.zeros((RWKV_WIDTH,), F32)], axis=0)
        yc = _rwkv(zr.reshape(bsz, t, ZR_COLS), rw_mu[l][None, :], rw_pv, rw_w_up[l].astype(BF16),
                   rw_a_up[l].astype(BF16), rw_g_up[l].astype(BF16), seg, tt=tt_r)

        xf = _mixer_out(xf, ya.reshape(n, LRU_WIDTH), yb.reshape(n, MLSTM_WIDTH), yc.reshape(n, RWKV_WIDTH),
                        w_out[l].astype(BF16), g[3:4], tm=tm)
        xf = _ffn(xf, g[4:6], ffn_w_in[l, 1].astype(BF16), ffn_w_out[l, 1].astype(BF16), tm=tm_ffn, tf=256)
        xf = _ple(xf, p[l].reshape(n, PLE_DIM), g[6:8], ple_w_gate[l].astype(BF16), ple_w_proj[l].astype(BF16), tm=tm)
    return xf.reshape(bsz, t, D_MODEL)
```

```python
import functools

import jax
import jax.numpy as jnp
from jax import lax
from jax.experimental import pallas as pl
from jax.experimental.pallas import tpu as pltpu

F32 = jnp.float32
BF16 = jnp.bfloat16

D_MODEL = 1024
PLE_DIM = 256
D_FF = 2816
RMS_EPS = 1e-6
LRU_WIDTH = 384
LRU_HEADS = 6
CONV_WIDTH = 4
LRU_C = 8.0
MLSTM_HEADS = 4
MLSTM_WIDTH = 256
RWKV_HEADS = 6
RWKV_WIDTH = 384
N_RWKV_COLS = 1408
GN_EPS = 64e-5
HEAD_DIM = 64
CHUNK = 64

ZR_COLS = N_RWKV_COLS
ZL_COLS = 2 * LRU_WIDTH
ZM_COLS = 4 * MLSTM_WIDTH
ZG_COLS = 128
KT_ROWS = MLSTM_WIDTH + 8

VMEM_LIMIT = 56 * 1024 * 1024


def _cparams(sem):
    return pltpu.CompilerParams(dimension_semantics=sem, vmem_limit_bytes=VMEM_LIMIT)


def _rms(x, g):
    return x * lax.rsqrt(jnp.mean(x * x, axis=-1, keepdims=True) + RMS_EPS) * g


def _softplus(x):
    return jnp.maximum(x, 0.0) + jnp.log(1.0 + jnp.exp(-jnp.abs(x)))


def _log_sigmoid(x):
    return -_softplus(-x)


def _dot(a, b):
    return jnp.dot(a.astype(BF16), b.astype(BF16), preferred_element_type=F32)


def _dot_nt(a, b):
    return lax.dot_general(a.astype(BF16), b.astype(BF16), (((1,), (1,)), ((), ())),
                           preferred_element_type=F32)


def _segsum(x, seg):
    hi = x.astype(BF16)
    r1 = x - hi.astype(F32)
    mid = r1.astype(BF16)
    lo = (r1 - mid.astype(F32)).astype(BF16)
    return (jnp.dot(hi, seg, preferred_element_type=F32) + jnp.dot(mid, seg, preferred_element_type=F32)
            + jnp.dot(lo, seg, preferred_element_type=F32))


def _ffn_kernel(x_ref, g_ref, wi_ref, wo_ref, o_ref, h_ref, *, tf):
    x = x_ref[...]
    xn = _rms(x, g_ref[0:1, :]).astype(BF16)
    for c in range(D_FF // tf):
        gate = jnp.dot(xn, wi_ref[:, c * tf:(c + 1) * tf], preferred_element_type=F32)
        up = jnp.dot(xn, wi_ref[:, D_FF + c * tf:D_FF + (c + 1) * tf], preferred_element_type=F32)
        h_ref[:, c * tf:(c + 1) * tf] = (gate * jax.nn.sigmoid(gate) * up).astype(BF16)
    y = jnp.dot(h_ref[...], wo_ref[...], preferred_element_type=F32)
    o_ref[...] = x + 0.5 * _rms(y, g_ref[1:2, :])


def _ffn(x, g2, w_in, w_out, *, tm, tf):
    n = x.shape[0]
    const = dict(pipeline_mode=pl.Buffered(1))
    return pl.pallas_call(
        functools.partial(_ffn_kernel, tf=tf),
        grid=(n // tm,),
        in_specs=[
            pl.BlockSpec((tm, D_MODEL), lambda i: (i, 0)),
            pl.BlockSpec((2, D_MODEL), lambda i: (0, 0)),
            pl.BlockSpec((D_MODEL, 2 * D_FF), lambda i: (0, 0), **const),
            pl.BlockSpec((D_FF, D_MODEL), lambda i: (0, 0), **const),
        ],
        out_specs=pl.BlockSpec((tm, D_MODEL), lambda i: (i, 0)),
        out_shape=jax.ShapeDtypeStruct((n, D_MODEL), F32),
        scratch_shapes=[pltpu.VMEM((tm, D_FF), BF16)],
        compiler_params=_cparams(("parallel",)),
        name="ffn",
    )(x, g2, w_in, w_out)


def _mixin_kernel(x_ref, g_ref, w_ref, wt_ref, zr_ref, zl_ref, zm_ref, zg_ref, kt_ref):
    xn = _rms(x_ref[...], g_ref[...]).astype(BF16)
    c0, c1, c2 = ZR_COLS, ZR_COLS + ZL_COLS, ZR_COLS + ZL_COLS + ZM_COLS
    zr_ref[...] = jnp.dot(xn, w_ref[:, 0:c0], preferred_element_type=F32)
    zl_ref[...] = jnp.dot(xn, w_ref[:, c0:c1], preferred_element_type=F32)
    zm_ref[...] = jnp.dot(xn, w_ref[:, c1:c2], preferred_element_type=F32)
    zg_ref[...] = jnp.dot(xn, w_ref[:, c2:c2 + ZG_COLS], preferred_element_type=F32)
    kt_ref[...] = lax.dot_general(wt_ref[...], xn, (((1,), (1,)), ((), ())), preferred_element_type=F32)


def _mixer_in(x, g, wcat, wt, *, tm):
    n = x.shape[0]
    ncols = wcat.shape[1]
    row = lambda i: (i, 0)
    return pl.pallas_call(
        _mixin_kernel,
        grid=(n // tm,),
        in_specs=[
            pl.BlockSpec((tm, D_MODEL), row),
            pl.BlockSpec((1, D_MODEL), lambda i: (0, 0)),
            pl.BlockSpec((D_MODEL, ncols), lambda i: (0, 0)),
            pl.BlockSpec((KT_ROWS, D_MODEL), lambda i: (0, 0)),
        ],
        out_specs=[
            pl.BlockSpec((tm, ZR_COLS), row),
            pl.BlockSpec((tm, ZL_COLS), row),
            pl.BlockSpec((tm, ZM_COLS), row),
            pl.BlockSpec((tm, ZG_COLS), row),
            pl.BlockSpec((KT_ROWS, tm), lambda i: (0, i)),
        ],
        out_shape=[
            jax.ShapeDtypeStruct((n, ZR_COLS), F32),
            jax.ShapeDtypeStruct((n, ZL_COLS), F32),
            jax.ShapeDtypeStruct((n, ZM_COLS), F32),
            jax.ShapeDtypeStruct((n, ZG_COLS), F32),
            jax.ShapeDtypeStruct((KT_ROWS, n), F32),
        ],
        compiler_params=_cparams(("parallel",)),
        name="mixer_in",
    )(x, g, wcat, wt)


def _lru_kernel(z_ref, pv_ref, wa_ref, wx_ref, o_ref, xbuf_ref, h_ref, al_ref, ul_ref, c_ref, *, tt):
    t = pl.program_id(1)

    @pl.when(t == 0)
    def _():
        xbuf_ref[0:8, :] = jnp.zeros((8, LRU_WIDTH), F32)
        h_ref[...] = jnp.zeros_like(h_ref)

    x = z_ref[:, 0:LRU_WIDTH]
    gate = z_ref[:, LRU_WIDTH:2 * LRU_WIDTH]
    pv = pv_ref[...]
    xbuf_ref[8:8 + tt, :] = x
    xa = pv[4:5, :] + pv[3:4, :] * x
    for j in range(CONV_WIDTH - 1):
        xa = xa + pv[j:j + 1, :] * xbuf_ref[pl.ds(5 + j, tt), :]
    xbuf_ref[0:8, :] = x[tt - 8:tt, :]

    r = jax.nn.sigmoid(_dot(xa, wa_ref[...]) + pv[5:6, :])
    i = jax.nn.sigmoid(_dot(xa, wx_ref[...]) + pv[6:7, :])
    log_a = (-LRU_C) * r * _softplus(-pv[7:8, :])
    a = jnp.exp(log_a)
    u = jnp.sqrt(1.0 - jnp.exp(2.0 * log_a)) * (i * xa)

    ng = tt // 8
    a3 = a.reshape(ng, 8, LRU_WIDTH)
    u3 = u.reshape(ng, 8, LRU_WIDTH)
    sub = lax.broadcasted_iota(jnp.int32, (ng, 8, LRU_WIDTH), 1)
    for d in (1, 2, 4):
        keep = sub >= d
        a_sh = jnp.where(keep, pltpu.roll(a3, d, axis=1), 1.0)
        u_sh = jnp.where(keep, pltpu.roll(u3, d, axis=1), 0.0)
        u3 = a3 * u_sh + u3
        a3 = a3 * a_sh
    a_l = a3.reshape(tt, LRU_WIDTH)
    u_l = u3.reshape(tt, LRU_WIDTH)
    nslab = LRU_WIDTH // 128
    for j in range(nslab):
        al_ref[j] = a_l[:, 128 * j:128 * (j + 1)]
        ul_ref[j] = u_l[:, 128 * j:128 * (j + 1)]
    ag = jnp.concatenate([al_ref[j, pl.ds(7, ng, stride=8), :] for j in range(nslab)], axis=1)
    ug = jnp.concatenate([ul_ref[j, pl.ds(7, ng, stride=8), :] for j in range(nslab)], axis=1)
    rowg = lax.broadcasted_iota(jnp.int32, (ng, LRU_WIDTH), 0)
    d = 1
    while d < ng:
        keep = rowg >= d
        a_sh = jnp.where(keep, pltpu.roll(ag, d, axis=0), 1.0)
        u_sh = jnp.where(keep, pltpu.roll(ug, d, axis=0), 0.0)
        ug = ag * u_sh + ug
        ag = ag * a_sh
        d *= 2
    h0 = h_ref[0:1, :]
    h_end = ug + ag * h0
    h_ref[...] = jnp.broadcast_to(h_end[ng - 1:ng, :], h_ref.shape)
    c_ref[...] = jnp.where(rowg == 0, h0, pltpu.roll(h_end, 1, axis=0))
    gg = jax.nn.gelu(gate, approximate=True)
    for gi in range(ng):
        rs = slice(8 * gi, 8 * gi + 8)
        o_ref[rs, :] = (u_l[rs, :] + a_l[rs, :] * c_ref[gi:gi + 1, :]) * gg[rs, :]


def _lru(zl, pv, wa, wx, *, tt):
    b, t, _ = zl.shape
    full = lambda bb, tt_: (0, 0)
    return pl.pallas_call(
        functools.partial(_lru_kernel, tt=tt),
        grid=(b, t // tt),
        in_specs=[
            pl.BlockSpec((None, tt, ZL_COLS), lambda bb, ti: (bb, ti, 0)),
            pl.BlockSpec((8, LRU_WIDTH), full),
            pl.BlockSpec((LRU_WIDTH, LRU_WIDTH), full),
            pl.BlockSpec((LRU_WIDTH, LRU_WIDTH), full),
        ],
        out_specs=pl.BlockSpec((None, tt, LRU_WIDTH), lambda bb, ti: (bb, ti, 0)),
        out_shape=jax.ShapeDtypeStruct((b, t, LRU_WIDTH), F32),
        scratch_shapes=[pltpu.VMEM((tt + 8, LRU_WIDTH), F32), pltpu.VMEM((8, LRU_WIDTH), F32),
                        pltpu.VMEM((LRU_WIDTH // 128, tt, 128), F32), pltpu.VMEM((LRU_WIDTH // 128, tt, 128), F32),
                        pltpu.VMEM((tt // 8, LRU_WIDTH), F32)],
        compiler_params=_cparams(("parallel", "arbitrary")),
        name="rglru",
    )(zl, pv, wa, wx)


def _mlstm_kernel(q_ref, k_ref, v_ref, og_ref, gc_ref, kt_ref, bcol_ref, brow_ref, nrm_ref, o_ref,
                  cn_ref, m_ref, *, tt):
    t = pl.program_id(1)

    @pl.when(t == 0)
    def _():
        cn_ref[...] = jnp.zeros_like(cn_ref)
        m_ref[...] = jnp.zeros_like(m_ref)

    L = CHUNK
    gr = kt_ref[MLSTM_WIDTH:MLSTM_WIDTH + 8, :] + bcol_ref[:, 0:1]
    br = _log_sigmoid(gr)
    pos = lax.broadcasted_iota(jnp.int32, (8, tt), 1) & (L - 1)
    d = 1
    while d < L:
        br = br + jnp.where(pos >= d, pltpu.roll(br, d, axis=1), 0.0)
        d *= 2
    bc = _log_sigmoid(gc_ref[...] + brow_ref[0:1, :])
    posc = lax.broadcasted_iota(jnp.int32, (tt, ZG_COLS), 0) & (L - 1)
    d = 1
    while d < L:
        bc = bc + jnp.where(posc >= d, pltpu.roll(bc, d, axis=0), 0.0)
        d *= 2

    nck = tt // L
    heads = range(MLSTM_HEADS)
    units = [(c, h) for c in range(nck) for h in heads]
    rsl = [slice(c * L, (c + 1) * L) for c in range(nck)]
    hsl = [slice(HEAD_DIM * h, HEAD_DIM * (h + 1)) for h in heads]
    psl = [slice(128 * (h // 2), 128 * (h // 2) + 128) for h in heads]
    jsl = [slice(128 * (c // 2), 128 * (c // 2) + 128) for c in range(nck)]

    lane = lax.broadcasted_iota(jnp.int32, (L, 128), 1)
    rowi = lax.broadcasted_iota(jnp.int32, (L, 128), 0)
    half = [lane < HEAD_DIM, lane >= HEAD_DIM]
    den_lane = [HEAD_DIM, HEAD_DIM - 1]
    ones_col = [jnp.where(lane == den_lane[e], 1.0, 0.0) for e in range(2)]
    cmask = [(lane >= L * cc) & (lane < L * (cc + 1)) & (lane - L * cc <= rowi) for cc in range(2)]
    lane1 = lax.broadcasted_iota(jnp.int32, (1, 128), 1)
    rmask = [lane1 < L, lane1 >= L]

    b_row = {(c, h): br[4 + h:5 + h, jsl[c]] for c, h in units}
    i_row = {(c, h): gr[h:h + 1, jsl[c]] for c, h in units}

    g = {(c, h): br[4 + h:5 + h, (c + 1) * L - 1:(c + 1) * L] for c, h in units}
    w_log = {u: jnp.where(rmask[u[0] % 2], g[u] - b_row[u] + i_row[u], -jnp.inf) for u in units}
    w_max = {u: jnp.max(w_log[u], axis=1, keepdims=True) for u in units}
    m_in, m_out = {}, {}
    for h in heads:
        m_st = m_ref[h, 0:1, 0:1]
        for c in range(nck):
            m_in[c, h] = m_st
            m_st = jnp.maximum(g[c, h] + m_st, w_max[c, h])
            m_out[c, h] = m_st
        m_ref[h] = jnp.broadcast_to(m_st, (8, 128))
    dec = {u: jnp.exp(g[u] + m_in[u] - m_out[u]) for u in units}
    wk = {u: jnp.exp(w_log[u] - m_out[u]) for u in units}

    q_m = {(c, h): (jnp.where(half[h % 2], q_ref[rsl[c], psl[h]], 0.0) * (HEAD_DIM ** -0.5)).astype(BF16)
           for c, h in units}
    k2 = {(c, h): k_ref[jsl[c], psl[h]].astype(BF16) for c, h in units}
    vx = {(c, h): jnp.where(jnp.concatenate([half[h % 2]] * 2, axis=0), v_ref[jsl[c], psl[h]],
                            jnp.concatenate([ones_col[h % 2]] * 2, axis=0)).astype(BF16) for c, h in units}
    bcb = {(c, h): jnp.broadcast_to(bc[rsl[c], 4 + h:5 + h], (L, 128)) for c, h in units}
    dmat = {u: jnp.where(cmask[u[0] % 2], bcb[u] - b_row[u] + i_row[u], -jnp.inf) for u in units}
    m_loc = {u: jnp.max(dmat[u], axis=1, keepdims=True) for u in units}
    p_in = {u: jnp.exp(dmat[u] - m_loc[u]) for u in units}
    qk = {u: _dot_nt(q_m[u], k2[u]) for u in units}
    nd = {u: _dot(p_in[u] * qk[u], vx[u]) for u in units}
    kv = {(c, h): _dot(kt_ref[hsl[h], jsl[c]] * wk[c, h], vx[c, h]) for c, h in units}

    cn_in = {}
    for h in heads:
        cn = cn_ref[h]
        for c in range(nck):
            cn_in[c, h] = cn
            cn = dec[c, h] * cn + kv[c, h]
        cn_ref[h] = cn
    cn_pair = {(c, p): jnp.concatenate([cn_in[c, 2 * p], cn_in[c, 2 * p + 1]], axis=0).astype(BF16)
               for c in range(nck) for p in range(MLSTM_HEADS // 2)}

    inter = {u: bcb[u] + m_in[u] for u in units}
    mj = {u: jnp.maximum(m_loc[u], inter[u]) for u in units}
    e_loc = {u: jnp.exp(m_loc[u] - mj[u]) for u in units}
    e_int = {u: jnp.exp(inter[u] - mj[u]) for u in units}
    e_neg = {u: jnp.exp(-mj[u]) for u in units}
    qc = {(c, h): _dot(q_m[c, h], cn_pair[c, h // 2]) for c, h in units}
    numden = {u: e_loc[u] * nd[u] + e_int[u] * qc[u] for u in units}
    den = {(c, h): numden[c, h][:, den_lane[h % 2]:den_lane[h % 2] + 1] for c, h in units}
    hv = {u: numden[u] / jnp.maximum(jnp.abs(den[u]), e_neg[u]) for u in units}
    ms = {(c, h): jnp.sum(jnp.where(half[h % 2], hv[c, h] * hv[c, h], 0.0), axis=1, keepdims=True)
          * (1.0 / HEAD_DIM) for c, h in units}
    hn = {u: hv[u] * lax.rsqrt(ms[u] + RMS_EPS) for u in units}
    for c in range(nck):
        for p in range(MLSTM_HEADS // 2):
            ps = psl[2 * p]
            hb = jnp.where(half[0], hn[c, 2 * p], hn[c, 2 * p + 1])
            o_ref[rsl[c], ps] = hb * nrm_ref[0:1, ps] * jax.nn.sigmoid(og_ref[rsl[c], ps])


def _mlstm(zm, zg, ktg, bcol, brow, nrm, *, tt):
    b, t, _ = zm.shape
    nt = t // tt
    colblk = lambda j: pl.BlockSpec((None, tt, MLSTM_WIDTH), lambda bb, ti, j=j: (bb, ti, j))
    full = lambda bb, ti: (0, 0)
    return pl.pallas_call(
        functools.partial(_mlstm_kernel, tt=tt),
        grid=(b, nt),
        in_specs=[
            colblk(0), colblk(1), colblk(2), colblk(3),
            pl.BlockSpec((None, tt, ZG_COLS), lambda bb, ti: (bb, ti, 0)),
            pl.BlockSpec((KT_ROWS, tt), lambda bb, ti: (0, bb * nt + ti)),
            pl.BlockSpec((8, 128), full),
            pl.BlockSpec((8, 128), full),
            pl.BlockSpec((8, MLSTM_WIDTH), full),
        ],
        out_specs=pl.BlockSpec((None, tt, MLSTM_WIDTH), lambda bb, ti: (bb, ti, 0)),
        out_shape=jax.ShapeDtypeStruct((b, t, MLSTM_WIDTH), F32),
        scratch_shapes=[pltpu.VMEM((MLSTM_HEADS, HEAD_DIM, 128), F32), pltpu.VMEM((MLSTM_HEADS, 8, 128), F32)],
        compiler_params=_cparams(("parallel", "arbitrary")),
        name="mlstm",
    )(zm, zm, zm, zm, zg, ktg, bcol, brow, nrm)


def _rwkv_kernel(z_ref, mu_ref, pv_ref, wup_ref, aup_ref, gup_ref, seg_ref, o_ref,
                 prev_ref, st_ref, y_ref, *, tt):
    t = pl.program_id(1)

    @pl.when(t == 0)
    def _():
        prev_ref[...] = jnp.zeros_like(prev_ref)
        st_ref[...] = jnp.zeros_like(st_ref)

    L = CHUNK
    W = RWKV_WIDTH
    nck = tt // L
    z = z_ref[...]
    row1 = lax.broadcasted_iota(jnp.int32, (tt, 1), 0)
    zprev = jnp.where(row1 == 0, prev_ref[0:1, :], pltpu.roll(z, 1, axis=0))
    prev_ref[...] = jnp.broadcast_to(z[tt - 1:tt, :], prev_ref.shape)
    zs = z + (zprev - z) * mu_ref[...]
    r = zs[:, 0:W]
    k = zs[:, W:2 * W]
    v = zs[:, 2 * W:3 * W]
    wd = zs[:, 3 * W:3 * W + 64]
    ad = zs[:, 3 * W + 64:3 * W + 128]
    gd = zs[:, 3 * W + 128:3 * W + 256]
    pv = pv_ref[...]
    w0, a0, k_k, k_a, r_k, ln_w, ln_b = (pv[j:j + 1, :] for j in range(7))
    seg = seg_ref[...]

    log_w = -_softplus(-(w0 + _dot(jnp.tanh(wd), wup_ref[...]))) - 0.5
    ld = -jnp.exp(log_w)
    iclr = jax.nn.sigmoid(a0 + _dot(ad, aup_ref[...]))
    g = _dot(jax.nn.sigmoid(gd), gup_ref[...])
    kk = k * k_k
    kk = kk / jnp.maximum(jnp.sqrt(_segsum(kk * kk, seg)), 1e-12)
    k_mod = k * (1.0 + (iclr - 1.0) * k_a)
    a_vec = -kk
    b_vec = kk * iclr

    cum = ld
    posr = lax.broadcasted_iota(jnp.int32, (tt, W), 0) & (L - 1)
    d = 1
    while d < L:
        cum = cum + jnp.where(posr >= d, pltpu.roll(cum, d, axis=0), 0.0)
        d *= 2
    cum_last = jnp.concatenate(
        [jnp.broadcast_to(cum[(c + 1) * L - 1:(c + 1) * L, :], (L, W)) for c in range(nck)], axis=0)
    e_neg = jnp.exp(-cum)
    e_tail = jnp.exp(cum_last - cum)
    a_t = jnp.exp(cum - ld) * a_vec
    r_t = jnp.exp(cum) * r
    b_t = e_neg * b_vec
    k_t = e_neg * k_mod
    b_h = e_tail * b_vec
    k_h = e_tail * k_mod

    UR = 2 * L
    ri = lax.broadcasted_iota(jnp.int32, (UR, UR), 0)
    ci = lax.broadcasted_iota(jnp.int32, (UR, UR), 1)
    same = (ri // L) == (ci // L)
    strict = jnp.where(same & (ri > ci), 1.0, 0.0)
    incl = jnp.where(same & (ri >= ci), 1.0, 0.0)
    eye = jnp.where(ri == ci, 1.0, 0.0)

    heads = range(RWKV_HEADS)
    hsl = [slice(HEAD_DIM * h, HEAD_DIM * (h + 1)) for h in heads]
    bh_t, kh_t, cum_t = [], [], []
    for pair in range(RWKV_HEADS // 2):
        ps = slice(128 * pair, 128 * pair + 128)
        bp, kp, cp = b_h[:, ps].T.astype(BF16), k_h[:, ps].T.astype(BF16), cum[:, ps].T
        for e in range(2):
            es = slice(HEAD_DIM * e, HEAD_DIM * (e + 1))
            bh_t.append(bp[es])
            kh_t.append(kp[es])
            cum_t.append(cp[es])

    units = [(h, u) for u in range(tt // UR) for h in heads]
    nu = range(len(units))
    usl = [slice(UR * u, UR * (u + 1)) for _, u in units]
    at = [a_t[usl[i], hsl[h]].astype(BF16) for i, (h, _) in enumerate(units)]
    rt = [r_t[usl[i], hsl[h]] for i, (h, _) in enumerate(units)]
    bt = [b_t[usl[i], hsl[h]].astype(BF16) for i, (h, _) in enumerate(units)]
    kt = [k_t[usl[i], hsl[h]].astype(BF16) for i, (h, _) in enumerate(units)]
    vu = [v[usl[i], hsl[h]].astype(BF16) for i, (h, _) in enumerate(units)]
    a_ab = [strict * _dot_nt(at[i], bt[i]) for i in nu]
    a_ak = [(strict * _dot_nt(at[i], kt[i])).astype(BF16) for i in nu]
    a_rb = [(incl * _dot_nt(rt[i], bt[i])).astype(BF16) for i in nu]
    a_rk = [(incl * _dot_nt(rt[i], kt[i])).astype(BF16) for i in nu]
    tm = [eye + a_ab[i] for i in nu]
    pw = [a_ab[i].astype(BF16) for i in nu]
    for _ in range(5):
        pw = [_dot(pw[i], pw[i]).astype(BF16) for i in nu]
        tm = [tm[i] + _dot(pw[i], tm[i]) for i in nu]
    akv = [_dot(a_ak[i], vu[i]) for i in nu]
    tmb = [tm[i].astype(BF16) for i in nu]
    w_all = [_dot(tmb[i], at[i]).astype(BF16) for i in nu]
    u0_all = [_dot(tmb[i], akv[i]).astype(BF16) for i in nu]
    q_all = [(rt[i] + _dot(a_rb[i], w_all[i])).astype(BF16) for i in nu]
    y0_all = [_dot(a_rb[i], u0_all[i]) + _dot(a_rk[i], vu[i]) for i in nu]
    gm, hm, pcol, qc, y0c = {}, {}, {}, {}, {}
    for i, (h, u) in enumerate(units):
        for cc in range(UR // L):
            c = u * (UR // L) + cc
            ls = slice(cc * L, (cc + 1) * L)
            rs = slice(c * L, (c + 1) * L)
            gm[c, h] = _dot(bh_t[h][:, rs], w_all[i][ls]).astype(BF16)
            hm[c, h] = _dot(bh_t[h][:, rs], u0_all[i][ls]) + _dot(kh_t[h][:, rs], vu[i][ls])
            pcol[c, h] = jnp.exp(cum_t[h][:, (c + 1) * L - 1:(c + 1) * L])
            qc[c, h] = q_all[i][ls]
            y0c[c, h] = y0_all[i][ls]
    zst = [st_ref[h] for h in heads]
    for c in range(nck):
        rs = slice(c * L, (c + 1) * L)
        for h in heads:
            zb = zst[h].astype(BF16)
            y_ref[rs, hsl[h]] = _dot(qc[c, h], zb) + y0c[c, h]
            zst[h] = pcol[c, h] * zst[h] + _dot(gm[c, h], zb) + hm[c, h]
    for h in heads:
        st_ref[h] = zst[h]

    y = y_ref[...]
    mean = _segsum(y, seg) * (1.0 / HEAD_DIM)
    yc = y - mean
    var = _segsum(yc * yc, seg) * (1.0 / HEAD_DIM)
    yn = yc * lax.rsqrt(var + GN_EPS) * ln_w + ln_b
    bonus = _segsum(r * k_mod * r_k, seg) * v
    o_ref[...] = (yn + bonus) * g


def _rwkv(zr, mu, pv, wup, aup, gup, seg, *, tt):
    b, t, _ = zr.shape
    full = lambda bb, ti: (0, 0)
    return pl.pallas_call(
        functools.partial(_rwkv_kernel, tt=tt),
        grid=(b, t // tt),
        in_specs=[
            pl.BlockSpec((None, tt, N_RWKV_COLS), lambda bb, ti: (bb, ti, 0)),
            pl.BlockSpec((1, N_RWKV_COLS), full),
            pl.BlockSpec((8, RWKV_WIDTH), full),
            pl.BlockSpec((64, RWKV_WIDTH), full),
            pl.BlockSpec((64, RWKV_WIDTH), full),
            pl.BlockSpec((128, RWKV_WIDTH), full),
            pl.BlockSpec((RWKV_WIDTH, RWKV_WIDTH), full),
        ],
        out_specs=pl.BlockSpec((None, tt, RWKV_WIDTH), lambda bb, ti: (bb, ti, 0)),
        out_shape=jax.ShapeDtypeStruct((b, t, RWKV_WIDTH), F32),
        scratch_shapes=[
            pltpu.VMEM((8, N_RWKV_COLS), F32),
            pltpu.VMEM((RWKV_HEADS, HEAD_DIM, HEAD_DIM), F32),
            pltpu.VMEM((tt, RWKV_WIDTH), F32),
        ],
        compiler_params=_cparams(("parallel", "arbitrary")),
        name="rwkv7",
    )(zr, mu, pv, wup, aup, gup, seg)


def _tail_kernel(x_ref, ya_ref, yb_ref, yc_ref, p_ref, g_ref, wm_ref, wi_ref, wo_ref, wg_ref, wp_ref, o_ref,
                 h_ref, *, tf):
    c0, c1 = LRU_WIDTH, LRU_WIDTH + MLSTM_WIDTH
    mix = (jnp.dot(ya_ref[...].astype(BF16), wm_ref[0:c0, :], preferred_element_type=F32)
           + jnp.dot(yb_ref[...].astype(BF16), wm_ref[c0:c1, :], preferred_element_type=F32)
           + jnp.dot(yc_ref[...].astype(BF16), wm_ref[c1:, :], preferred_element_type=F32))
    x = x_ref[...] + _rms(mix, g_ref[3:4, :])
    xn = _rms(x, g_ref[4:5, :]).astype(BF16)
    for c in range(D_FF // tf):
        gate = jnp.dot(xn, wi_ref[:, c * tf:(c + 1) * tf], preferred_element_type=F32)
        up = jnp.dot(xn, wi_ref[:, D_FF + c * tf:D_FF + (c + 1) * tf], preferred_element_type=F32)
        h_ref[:, c * tf:(c + 1) * tf] = (gate * jax.nn.sigmoid(gate) * up).astype(BF16)
    y = jnp.dot(h_ref[...], wo_ref[...], preferred_element_type=F32)
    x = x + 0.5 * _rms(y, g_ref[5:6, :])
    pgate = jax.nn.sigmoid(jnp.dot(_rms(x, g_ref[6:7, :]).astype(BF16), wg_ref[...], preferred_element_type=F32))
    pe = jnp.dot(p_ref[...].astype(BF16), wp_ref[...], preferred_element_type=F32)
    o_ref[...] = x + _rms(pgate * pe, g_ref[7:8, :])


def _tail(x, ya, yb, yc, p, g8, w_mix, w_in, w_out, w_gate, w_proj, *, tm, tf):
    n = x.shape[0]
    row = lambda i: (i, 0)
    const = lambda shape: pl.BlockSpec(shape, lambda i: (0, 0), pipeline_mode=pl.Buffered(1))
    return pl.pallas_call(
        functools.partial(_tail_kernel, tf=tf),
        grid=(n // tm,),
        in_specs=[
            pl.BlockSpec((tm, D_MODEL), row),
            pl.BlockSpec((tm, LRU_WIDTH), row),
            pl.BlockSpec((tm, MLSTM_WIDTH), row),
            pl.BlockSpec((tm, RWKV_WIDTH), row),
            pl.BlockSpec((tm, PLE_DIM), row),
            const((8, D_MODEL)),
            const((D_MODEL, D_MODEL)),
            const((D_MODEL, 2 * D_FF)),
            const((D_FF, D_MODEL)),
            const((D_MODEL, D_MODEL)),
            const((PLE_DIM, D_MODEL)),
        ],
        out_specs=pl.BlockSpec((tm, D_MODEL), row),
        out_shape=jax.ShapeDtypeStruct((n, D_MODEL), F32),
        scratch_shapes=[pltpu.VMEM((tm, D_FF), BF16)],
        compiler_params=_cparams(("parallel",)),
        name="tail",
    )(x, ya, yb, yc, p, g8, w_mix, w_in, w_out, w_gate, w_proj)


def _block_diag(w):
    nh, dd, _ = w.shape
    eye = jnp.eye(nh, dtype=w.dtype)
    return (eye[:, None, :, None] * w[:, :, None, :]).reshape(nh * dd, nh * dd)


def _tile(n, pref):
    return pref if n % pref == 0 else n


def kernel(x, p, norm_g, ffn_w_in, ffn_w_out, w_in, w_out, lru_conv_w, lru_conv_b, lru_w_a, lru_b_a, lru_w_x, lru_b_x, lru_lambda, m_b_i, m_b_f, m_norm, rw_mu, rw_w0, rw_w_up, rw_a0, rw_a_up, rw_g_up, rw_k_k, rw_k_a, rw_r_k, rw_ln_w, rw_ln_b, ple_w_proj, ple_w_gate):
    bsz, t, _ = x.shape
    n = bsz * t
    depth = norm_g.shape[0]
    tm_ffn = _tile(n, 512)
    tm = _tile(n, 512)
    tt_lru = _tile(t, 512)
    tt_m = _tile(t, 512)
    tt_r = _tile(t, 256)
    seg = _block_diag(jnp.ones((RWKV_HEADS, HEAD_DIM, HEAD_DIM), BF16))

    xf = x.reshape(n, D_MODEL)
    for l in range(depth):
        g = norm_g[l]
        wl = w_in[l]
        o_zr = 2 * LRU_WIDTH + 4 * MLSTM_WIDTH + 2 * MLSTM_HEADS
        w_gates = wl[:, o_zr - 2 * MLSTM_HEADS:o_zr]
        wcat = jnp.concatenate(
            [wl[:, o_zr:], wl[:, 0:ZL_COLS], wl[:, ZL_COLS:ZL_COLS + ZM_COLS],
             jnp.pad(w_gates, ((0, 0), (0, ZG_COLS - 2 * MLSTM_HEADS)))], axis=1).astype(BF16)
        w_k = wl[:, ZL_COLS + MLSTM_WIDTH:ZL_COLS + 2 * MLSTM_WIDTH]
        wt = jnp.concatenate([w_k, w_gates], axis=1).T.astype(BF16)

        xf = _ffn(xf, g[0:2], ffn_w_in[l, 0].astype(BF16), ffn_w_out[l, 0].astype(BF16), tm=tm_ffn, tf=256)

        zr, zl, zm, zg, ktg = _mixer_in(xf, g[2:3], wcat, wt, tm=tm)

        lru_pv = jnp.concatenate([lru_conv_w[l], lru_conv_b[l][None], lru_b_a[l][None], lru_b_x[l][None],
                                  lru_lambda[l][None]], axis=0)
        ya = _lru(zl.reshape(bsz, t, ZL_COLS), lru_pv, _block_diag(lru_w_a[l]).astype(BF16),
                  _block_diag(lru_w_x[l]).astype(BF16), tt=tt_lru)

        gate_b = jnp.concatenate([m_b_i[l], m_b_f[l]])
        bcol = jnp.broadcast_to(gate_b[:, None], (8, 128))
        brow = jnp.broadcast_to(jnp.pad(gate_b, (0, ZG_COLS - 8))[None, :], (8, ZG_COLS))
        nrm = jnp.broadcast_to(m_norm[l][None, :], (8, MLSTM_WIDTH))
        yb = _mlstm(zm.reshape(bsz, t, ZM_COLS), zg.reshape(bsz, t, ZG_COLS), ktg, bcol, brow, nrm, tt=tt_m)

        rw_pv = jnp.stack([rw_w0[l], rw_a0[l], rw_k_k[l], rw_k_a[l], rw_r_k[l].reshape(-1), rw_ln_w[l],
                           rw_ln_b[l], jnp.zeros((RWKV_WIDTH,), F32)], axis=0)
        yc = _rwkv(zr.reshape(bsz, t, ZR_COLS), rw_mu[l][None, :], rw_pv, rw_w_up[l].astype(BF16),
                   rw_a_up[l].astype(BF16), rw_g_up[l].astype(BF16), seg, tt=tt_r)

        xf = _tail(xf, ya.reshape(n, LRU_WIDTH), yb.reshape(n, MLSTM_WIDTH), yc.reshape(n, RWKV_WIDTH),
                   p[l].reshape(n, PLE_DIM), g, w_out[l].astype(BF16), ffn_w_in[l, 1].astype(BF16),
                   ffn_w_out[l, 1].astype(BF16), ple_w_gate[l].astype(BF16), ple_w_proj[l].astype(BF16),
                   tm=tm_ffn, tf=256)
    return xf.reshape(bsz, t, D_MODEL)
```

```python
import functools

import jax
import jax.numpy as jnp
from jax import lax
from jax.experimental import pallas as pl
from jax.experimental.pallas import tpu as pltpu

F32 = jnp.float32
BF16 = jnp.bfloat16

D_MODEL = 1024
PLE_DIM = 256
D_FF = 2816
RMS_EPS = 1e-6
LRU_WIDTH = 384
LRU_HEADS = 6
CONV_WIDTH = 4
LRU_C = 8.0
MLSTM_HEADS = 4
MLSTM_WIDTH = 256
RWKV_HEADS = 6
RWKV_WIDTH = 384
N_RWKV_COLS = 1408
GN_EPS = 64e-5
HEAD_DIM = 64
CHUNK = 64

ZR_COLS = N_RWKV_COLS
ZL_COLS = 2 * LRU_WIDTH
ZM_COLS = 4 * MLSTM_WIDTH
ZG_COLS = 128
KT_ROWS = MLSTM_WIDTH + 8

VMEM_LIMIT = 56 * 1024 * 1024


def _cparams(sem):
    return pltpu.CompilerParams(dimension_semantics=sem, vmem_limit_bytes=VMEM_LIMIT)


def _rms(x, g):
    return x * lax.rsqrt(jnp.mean(x * x, axis=-1, keepdims=True) + RMS_EPS) * g


def _softplus(x):
    return jnp.maximum(x, 0.0) + jnp.log(1.0 + jnp.exp(-jnp.abs(x)))


def _log_sigmoid(x):
    return -_softplus(-x)


def _dot(a, b):
    return jnp.dot(a.astype(BF16), b.astype(BF16), preferred_element_type=F32)


def _dot_nt(a, b):
    return lax.dot_general(a.astype(BF16), b.astype(BF16), (((1,), (1,)), ((), ())),
                           preferred_element_type=F32)


def _segsum(x, seg):
    hi = x.astype(BF16)
    lo = (x - hi.astype(F32)).astype(BF16)
    w = seg.shape[0]
    head = (jnp.dot(hi[:, 0:w], seg, preferred_element_type=F32)
            + jnp.dot(lo[:, 0:w], seg, preferred_element_type=F32))
    tail = jnp.dot(jnp.concatenate([hi[:, w:], lo[:, w:]], axis=1), seg, preferred_element_type=F32)
    return jnp.concatenate([head, tail[:, 0:w // 2] + tail[:, w // 2:]], axis=1)


def _ffn_kernel(x_ref, g_ref, wi_ref, wo_ref, o_ref, h_ref, *, tf):
    x = x_ref[...]
    xn = _rms(x, g_ref[0:1, :]).astype(BF16)
    for c in range(D_FF // tf):
        gate = jnp.dot(xn, wi_ref[:, c * tf:(c + 1) * tf], preferred_element_type=F32)
        up = jnp.dot(xn, wi_ref[:, D_FF + c * tf:D_FF + (c + 1) * tf], preferred_element_type=F32)
        h_ref[:, c * tf:(c + 1) * tf] = (gate * jax.nn.sigmoid(gate) * up).astype(BF16)
    y = jnp.dot(h_ref[...], wo_ref[...], preferred_element_type=F32)
    o_ref[...] = x + 0.5 * _rms(y, g_ref[1:2, :])


def _ffn(x, g2, w_in, w_out, *, tm, tf):
    n = x.shape[0]
    const = dict(pipeline_mode=pl.Buffered(1))
    return pl.pallas_call(
        functools.partial(_ffn_kernel, tf=tf),
        grid=(n // tm,),
        in_specs=[
            pl.BlockSpec((tm, D_MODEL), lambda i: (i, 0)),
            pl.BlockSpec((2, D_MODEL), lambda i: (0, 0)),
            pl.BlockSpec((D_MODEL, 2 * D_FF), lambda i: (0, 0), **const),
            pl.BlockSpec((D_FF, D_MODEL), lambda i: (0, 0), **const),
        ],
        out_specs=pl.BlockSpec((tm, D_MODEL), lambda i: (i, 0)),
        out_shape=jax.ShapeDtypeStruct((n, D_MODEL), F32),
        scratch_shapes=[pltpu.VMEM((tm, D_FF), BF16)],
        compiler_params=_cparams(("parallel",)),
        name="ffn",
    )(x, g2, w_in, w_out)


def _mixin_kernel(x_ref, g_ref, w_ref, wt_ref, zr_ref, zl_ref, zm_ref, zg_ref, kt_ref):
    xn = _rms(x_ref[...], g_ref[...]).astype(BF16)
    c0, c1, c2 = ZR_COLS + ZG_COLS, ZR_COLS + ZG_COLS + ZL_COLS, ZR_COLS + ZG_COLS + ZL_COLS + ZM_COLS
    zrg = jnp.dot(xn, w_ref[:, 0:c0], preferred_element_type=F32)
    zg = zrg[:, ZR_COLS:c0]
    zr_ref[...] = zrg[:, 0:ZR_COLS]
    zg_ref[...] = zg
    zl_ref[...] = jnp.dot(xn, w_ref[:, c0:c1], preferred_element_type=F32)
    zm_ref[...] = jnp.dot(xn, w_ref[:, c1:c2], preferred_element_type=F32)
    kt_ref[0:MLSTM_WIDTH, :] = lax.dot_general(wt_ref[...], xn, (((1,), (1,)), ((), ())),
                                               preferred_element_type=F32)
    kt_ref[MLSTM_WIDTH:KT_ROWS, :] = zg.T[0:KT_ROWS - MLSTM_WIDTH, :]


def _mixer_in(x, g, wcat, wt, *, tm):
    n = x.shape[0]
    ncols = wcat.shape[1]
    row = lambda i: (i, 0)
    return pl.pallas_call(
        _mixin_kernel,
        grid=(n // tm,),
        in_specs=[
            pl.BlockSpec((tm, D_MODEL), row),
            pl.BlockSpec((1, D_MODEL), lambda i: (0, 0)),
            pl.BlockSpec((D_MODEL, ncols), lambda i: (0, 0)),
            pl.BlockSpec((MLSTM_WIDTH, D_MODEL), lambda i: (0, 0)),
        ],
        out_specs=[
            pl.BlockSpec((tm, ZR_COLS), row),
            pl.BlockSpec((tm, ZL_COLS), row),
            pl.BlockSpec((tm, ZM_COLS), row),
            pl.BlockSpec((tm, ZG_COLS), row),
            pl.BlockSpec((KT_ROWS, tm), lambda i: (0, i)),
        ],
        out_shape=[
            jax.ShapeDtypeStruct((n, ZR_COLS), F32),
            jax.ShapeDtypeStruct((n, ZL_COLS), F32),
            jax.ShapeDtypeStruct((n, ZM_COLS), F32),
            jax.ShapeDtypeStruct((n, ZG_COLS), F32),
            jax.ShapeDtypeStruct((KT_ROWS, n), F32),
        ],
        compiler_params=_cparams(("parallel",)),
        name="mixer_in",
    )(x, g, wcat, wt)


def _lru_kernel(z_ref, pv_ref, wa_ref, wx_ref, o_ref, xbuf_ref, h_ref, al_ref, ul_ref, c_ref, *, tt):
    t = pl.program_id(1)

    @pl.when(t == 0)
    def _():
        xbuf_ref[0:8, :] = jnp.zeros((8, LRU_WIDTH), F32)
        h_ref[...] = jnp.zeros_like(h_ref)

    x = z_ref[:, 0:LRU_WIDTH]
    gate = z_ref[:, LRU_WIDTH:2 * LRU_WIDTH]
    pv = pv_ref[...]
    xbuf_ref[8:8 + tt, :] = x
    xa = pv[4:5, :] + pv[3:4, :] * x
    for j in range(CONV_WIDTH - 1):
        xa = xa + pv[j:j + 1, :] * xbuf_ref[pl.ds(5 + j, tt), :]
    xbuf_ref[0:8, :] = x[tt - 8:tt, :]

    r = jax.nn.sigmoid(_dot(xa, wa_ref[...]) + pv[5:6, :])
    i = jax.nn.sigmoid(_dot(xa, wx_ref[...]) + pv[6:7, :])
    log_a = (-LRU_C) * r * _softplus(-pv[7:8, :])
    a = jnp.exp(log_a)
    u = jnp.sqrt(1.0 - jnp.exp(2.0 * log_a)) * (i * xa)

    ng = tt // 8
    a3 = a.reshape(ng, 8, LRU_WIDTH)
    u3 = u.reshape(ng, 8, LRU_WIDTH)
    sub = lax.broadcasted_iota(jnp.int32, (ng, 8, LRU_WIDTH), 1)
    for d in (1, 2, 4):
        keep = sub >= d
        a_sh = jnp.where(keep, pltpu.roll(a3, d, axis=1), 1.0)
        u_sh = jnp.where(keep, pltpu.roll(u3, d, axis=1), 0.0)
        u3 = a3 * u_sh + u3
        a3 = a3 * a_sh
    a_l = a3.reshape(tt, LRU_WIDTH)
    u_l = u3.reshape(tt, LRU_WIDTH)
    nslab = LRU_WIDTH // 128
    for j in range(nslab):
        al_ref[j] = a_l[:, 128 * j:128 * (j + 1)]
        ul_ref[j] = u_l[:, 128 * j:128 * (j + 1)]
    ag = jnp.concatenate([al_ref[j, pl.ds(7, ng, stride=8), :] for j in range(nslab)], axis=1)
    ug = jnp.concatenate([ul_ref[j, pl.ds(7, ng, stride=8), :] for j in range(nslab)], axis=1)
    rowg = lax.broadcasted_iota(jnp.int32, (ng, LRU_WIDTH), 0)
    d = 1
    while d < ng:
        keep = rowg >= d
        a_sh = jnp.where(keep, pltpu.roll(ag, d, axis=0), 1.0)
        u_sh = jnp.where(keep, pltpu.roll(ug, d, axis=0), 0.0)
        ug = ag * u_sh + ug
        ag = ag * a_sh
        d *= 2
    h0 = h_ref[0:1, :]
    h_end = ug + ag * h0
    h_ref[...] = jnp.broadcast_to(h_end[ng - 1:ng, :], h_ref.shape)
    c_ref[...] = jnp.where(rowg == 0, h0, pltpu.roll(h_end, 1, axis=0))
    gg = jax.nn.gelu(gate, approximate=True)
    for gi in range(ng):
        rs = slice(8 * gi, 8 * gi + 8)
        o_ref[rs, :] = (u_l[rs, :] + a_l[rs, :] * c_ref[gi:gi + 1, :]) * gg[rs, :]


def _lru(zl, pv, wa, wx, *, tt):
    b, t, _ = zl.shape
    full = lambda bb, tt_: (0, 0)
    return pl.pallas_call(
        functools.partial(_lru_kernel, tt=tt),
        grid=(b, t // tt),
        in_specs=[
            pl.BlockSpec((None, tt, ZL_COLS), lambda bb, ti: (bb, ti, 0)),
            pl.BlockSpec((8, LRU_WIDTH), full),
            pl.BlockSpec((LRU_WIDTH, LRU_WIDTH), full),
            pl.BlockSpec((LRU_WIDTH, LRU_WIDTH), full),
        ],
        out_specs=pl.BlockSpec((None, tt, LRU_WIDTH), lambda bb, ti: (bb, ti, 0)),
        out_shape=jax.ShapeDtypeStruct((b, t, LRU_WIDTH), F32),
        scratch_shapes=[pltpu.VMEM((tt + 8, LRU_WIDTH), F32), pltpu.VMEM((8, LRU_WIDTH), F32),
                        pltpu.VMEM((LRU_WIDTH // 128, tt, 128), F32), pltpu.VMEM((LRU_WIDTH // 128, tt, 128), F32),
                        pltpu.VMEM((tt // 8, LRU_WIDTH), F32)],
        compiler_params=_cparams(("parallel", "arbitrary")),
        name="rglru",
    )(zl, pv, wa, wx)


def _mlstm_kernel(q_ref, k_ref, v_ref, og_ref, gc_ref, kt_ref, bcol_ref, brow_ref, nrm_ref, o_ref,
                  cn_ref, m_ref, *, tt):
    t = pl.program_id(1)

    @pl.when(t == 0)
    def _():
        cn_ref[...] = jnp.zeros_like(cn_ref)
        m_ref[...] = jnp.zeros_like(m_ref)

    L = CHUNK
    gr = kt_ref[MLSTM_WIDTH:MLSTM_WIDTH + 8, :] + bcol_ref[:, 0:1]
    br = _log_sigmoid(gr)
    pos = lax.broadcasted_iota(jnp.int32, (8, tt), 1) & (L - 1)
    d = 1
    while d < L:
        br = br + jnp.where(pos >= d, pltpu.roll(br, d, axis=1), 0.0)
        d *= 2
    bc = _log_sigmoid(gc_ref[...] + brow_ref[0:1, :])
    posc = lax.broadcasted_iota(jnp.int32, (tt, ZG_COLS), 0) & (L - 1)
    d = 1
    while d < L:
        bc = bc + jnp.where(posc >= d, pltpu.roll(bc, d, axis=0), 0.0)
        d *= 2

    nck = tt // L
    heads = range(MLSTM_HEADS)
    units = [(c, h) for c in range(nck) for h in heads]
    rsl = [slice(c * L, (c + 1) * L) for c in range(nck)]
    hsl = [slice(HEAD_DIM * h, HEAD_DIM * (h + 1)) for h in heads]
    psl = [slice(128 * (h // 2), 128 * (h // 2) + 128) for h in heads]
    jsl = [slice(128 * (c // 2), 128 * (c // 2) + 128) for c in range(nck)]

    lane = lax.broadcasted_iota(jnp.int32, (L, 128), 1)
    rowi = lax.broadcasted_iota(jnp.int32, (L, 128), 0)
    half = [lane < HEAD_DIM, lane >= HEAD_DIM]
    den_lane = [HEAD_DIM, HEAD_DIM - 1]
    ones_col = [jnp.where(lane == den_lane[e], 1.0, 0.0) for e in range(2)]
    cmask = [(lane >= L * cc) & (lane < L * (cc + 1)) & (lane - L * cc <= rowi) for cc in range(2)]
    lane1 = lax.broadcasted_iota(jnp.int32, (1, 128), 1)
    rmask = [lane1 < L, lane1 >= L]

    b_row = {(c, h): br[4 + h:5 + h, jsl[c]] for c, h in units}
    i_row = {(c, h): gr[h:h + 1, jsl[c]] for c, h in units}

    g = {(c, h): br[4 + h:5 + h, (c + 1) * L - 1:(c + 1) * L] for c, h in units}
    w_log = {u: jnp.where(rmask[u[0] % 2], g[u] - b_row[u] + i_row[u], -jnp.inf) for u in units}
    w_max = {u: jnp.max(w_log[u], axis=1, keepdims=True) for u in units}
    m_in, m_out = {}, {}
    for h in heads:
        m_st = m_ref[h, 0:1, 0:1]
        for c in range(nck):
            m_in[c, h] = m_st
            m_st = jnp.maximum(g[c, h] + m_st, w_max[c, h])
            m_out[c, h] = m_st
        m_ref[h] = jnp.broadcast_to(m_st, (8, 128))
    dec = {u: jnp.exp(g[u] + m_in[u] - m_out[u]) for u in units}
    wk = {u: jnp.exp(w_log[u] - m_out[u]) for u in units}

    q_m = {(c, h): (jnp.where(half[h % 2], q_ref[rsl[c], psl[h]], 0.0) * (HEAD_DIM ** -0.5)).astype(BF16)
           for c, h in units}
    k2 = {(c, h): k_ref[jsl[c], psl[h]].astype(BF16) for c, h in units}
    vx = {(c, h): jnp.where(jnp.concatenate([half[h % 2]] * 2, axis=0), v_ref[jsl[c], psl[h]],
                            jnp.concatenate([ones_col[h % 2]] * 2, axis=0)).astype(BF16) for c, h in units}
    bcb = {(c, h): jnp.broadcast_to(bc[rsl[c], 4 + h:5 + h], (L, 128)) for c, h in units}
    dmat = {u: jnp.where(cmask[u[0] % 2], bcb[u] - b_row[u] + i_row[u], -jnp.inf) for u in units}
    m_loc = {u: jnp.max(dmat[u], axis=1, keepdims=True) for u in units}
    p_in = {u: jnp.exp(dmat[u] - m_loc[u]) for u in units}
    qk = {u: _dot_nt(q_m[u], k2[u]) for u in units}
    nd = {u: _dot(p_in[u] * qk[u], vx[u]) for u in units}
    kv = {(c, h): _dot(kt_ref[hsl[h], jsl[c]] * wk[c, h], vx[c, h]) for c, h in units}

    cn_in = {}
    for h in heads:
        cn = cn_ref[h]
        for c in range(nck):
            cn_in[c, h] = cn
            cn = dec[c, h] * cn + kv[c, h]
        cn_ref[h] = cn
    cn_pair = {(c, p): jnp.concatenate([cn_in[c, 2 * p], cn_in[c, 2 * p + 1]], axis=0).astype(BF16)
               for c in range(nck) for p in range(MLSTM_HEADS // 2)}

    inter = {u: bcb[u] + m_in[u] for u in units}
    mj = {u: jnp.maximum(m_loc[u], inter[u]) for u in units}
    e_loc = {u: jnp.exp(m_loc[u] - mj[u]) for u in units}
    e_int = {u: jnp.exp(inter[u] - mj[u]) for u in units}
    e_neg = {u: jnp.exp(-mj[u]) for u in units}
    qc = {(c, h): _dot(q_m[c, h], cn_pair[c, h // 2]) for c, h in units}
    numden = {u: e_loc[u] * nd[u] + e_int[u] * qc[u] for u in units}
    den = {(c, h): numden[c, h][:, den_lane[h % 2]:den_lane[h % 2] + 1] for c, h in units}
    hv = {u: numden[u] / jnp.maximum(jnp.abs(den[u]), e_neg[u]) for u in units}
    ms = {(c, h): jnp.sum(jnp.where(half[h % 2], hv[c, h] * hv[c, h], 0.0), axis=1, keepdims=True)
          * (1.0 / HEAD_DIM) for c, h in units}
    hn = {u: hv[u] * lax.rsqrt(ms[u] + RMS_EPS) for u in units}
    for c in range(nck):
        for p in range(MLSTM_HEADS // 2):
            ps = psl[2 * p]
            hb = jnp.where(half[0], hn[c, 2 * p], hn[c, 2 * p + 1])
            o_ref[rsl[c], ps] = hb * nrm_ref[0:1, ps] * jax.nn.sigmoid(og_ref[rsl[c], ps])


def _mlstm(zm, zg, ktg, bcol, brow, nrm, *, tt):
    b, t, _ = zm.shape
    nt = t // tt
    colblk = lambda j: pl.BlockSpec((None, tt, MLSTM_WIDTH), lambda bb, ti, j=j: (bb, ti, j))
    full = lambda bb, ti: (0, 0)
    return pl.pallas_call(
        functools.partial(_mlstm_kernel, tt=tt),
        grid=(b, nt),
        in_specs=[
            colblk(0), colblk(1), colblk(2), colblk(3),
            pl.BlockSpec((None, tt, ZG_COLS), lambda bb, ti: (bb, ti, 0)),
            pl.BlockSpec((KT_ROWS, tt), lambda bb, ti: (0, bb * nt + ti)),
            pl.BlockSpec((8, 128), full),
            pl.BlockSpec((8, 128), full),
            pl.BlockSpec((8, MLSTM_WIDTH), full),
        ],
        out_specs=pl.BlockSpec((None, tt, MLSTM_WIDTH), lambda bb, ti: (bb, ti, 0)),
        out_shape=jax.ShapeDtypeStruct((b, t, MLSTM_WIDTH), F32),
        scratch_shapes=[pltpu.VMEM((MLSTM_HEADS, HEAD_DIM, 128), F32), pltpu.VMEM((MLSTM_HEADS, 8, 128), F32)],
        compiler_params=_cparams(("parallel", "arbitrary")),
        name="mlstm",
    )(zm, zm, zm, zm, zg, ktg, bcol, brow, nrm)


def _rwkv_kernel(z_ref, mu_ref, pv_ref, wup_ref, aup_ref, gup_ref, seg_ref, o_ref,
                 prev_ref, st_ref, y_ref, *, tt):
    t = pl.program_id(1)

    @pl.when(t == 0)
    def _():
        prev_ref[...] = jnp.zeros_like(prev_ref)
        st_ref[...] = jnp.zeros_like(st_ref)

    L = CHUNK
    W = RWKV_WIDTH
    nck = tt // L
    z = z_ref[...]
    row1 = lax.broadcasted_iota(jnp.int32, (tt, 1), 0)
    zprev = jnp.where(row1 == 0, prev_ref[0:1, :], pltpu.roll(z, 1, axis=0))
    prev_ref[...] = jnp.broadcast_to(z[tt - 1:tt, :], prev_ref.shape)
    zs = z + (zprev - z) * mu_ref[...]
    r = zs[:, 0:W]
    k = zs[:, W:2 * W]
    v = zs[:, 2 * W:3 * W]
    wd = zs[:, 3 * W:3 * W + 64]
    ad = zs[:, 3 * W + 64:3 * W + 128]
    gd = zs[:, 3 * W + 128:3 * W + 256]
    pv = pv_ref[...]
    w0, a0, k_k, k_a, r_k, ln_w, ln_b = (pv[j:j + 1, :] for j in range(7))
    seg = seg_ref[...]

    log_w = -_softplus(-(w0 + _dot(jnp.tanh(wd), wup_ref[...]))) - 0.5
    ld = -jnp.exp(log_w)
    iclr = jax.nn.sigmoid(a0 + _dot(ad, aup_ref[...]))
    g = _dot(jax.nn.sigmoid(gd), gup_ref[...])
    kk = k * k_k
    kk = kk / jnp.maximum(jnp.sqrt(_segsum(kk * kk, seg)), 1e-12)
    k_mod = k * (1.0 + (iclr - 1.0) * k_a)
    a_vec = -kk
    b_vec = kk * iclr

    cum = ld
    posr = lax.broadcasted_iota(jnp.int32, (tt, W), 0) & (L - 1)
    d = 1
    while d < L:
        cum = cum + jnp.where(posr >= d, pltpu.roll(cum, d, axis=0), 0.0)
        d *= 2
    cum_last = jnp.concatenate(
        [jnp.broadcast_to(cum[(c + 1) * L - 1:(c + 1) * L, :], (L, W)) for c in range(nck)], axis=0)
    e_neg = jnp.exp(-cum)
    e_tail = jnp.exp(cum_last - cum)
    a_t = jnp.exp(cum - ld) * a_vec
    r_t = jnp.exp(cum) * r
    b_t = e_neg * b_vec
    k_t = e_neg * k_mod
    b_h = e_tail * b_vec
    k_h = e_tail * k_mod

    UR = 2 * L
    ri = lax.broadcasted_iota(jnp.int32, (UR, UR), 0)
    ci = lax.broadcasted_iota(jnp.int32, (UR, UR), 1)
    same = (ri // L) == (ci // L)
    strict = jnp.where(same & (ri > ci), 1.0, 0.0)
    incl = jnp.where(same & (ri >= ci), 1.0, 0.0)
    eye = jnp.where(ri == ci, 1.0, 0.0)

    heads = range(RWKV_HEADS)
    hsl = [slice(HEAD_DIM * h, HEAD_DIM * (h + 1)) for h in heads]
    bh_t, kh_t, cum_t = [], [], []
    for pair in range(RWKV_HEADS // 2):
        ps = slice(128 * pair, 128 * pair + 128)
        bp, kp, cp = b_h[:, ps].T.astype(BF16), k_h[:, ps].T.astype(BF16), cum[:, ps].T
        for e in range(2):
            es = slice(HEAD_DIM * e, HEAD_DIM * (e + 1))
            bh_t.append(bp[es])
            kh_t.append(kp[es])
            cum_t.append(cp[es])

    units = [(h, u) for u in range(tt // UR) for h in heads]
    nu = range(len(units))
    usl = [slice(UR * u, UR * (u + 1)) for _, u in units]
    at = [a_t[usl[i], hsl[h]].astype(BF16) for i, (h, _) in enumerate(units)]
    rt = [r_t[usl[i], hsl[h]] for i, (h, _) in enumerate(units)]
    bt = [b_t[usl[i], hsl[h]].astype(BF16) for i, (h, _) in enumerate(units)]
    kt = [k_t[usl[i], hsl[h]].astype(BF16) for i, (h, _) in enumerate(units)]
    vu = [v[usl[i], hsl[h]].astype(BF16) for i, (h, _) in enumerate(units)]
    mask4 = jnp.concatenate([jnp.concatenate([strict, strict], axis=1),
                             jnp.concatenate([incl, incl], axis=1)], axis=0)
    prod = [mask4 * _dot_nt(jnp.concatenate([at[i], rt[i].astype(BF16)], axis=0),
                            jnp.concatenate([bt[i], kt[i]], axis=0)) for i in nu]
    a_ab = [prod[i][0:UR, 0:UR] for i in nu]
    a_ak = [prod[i][0:UR, UR:2 * UR].astype(BF16) for i in nu]
    a_rbk = [prod[i][UR:2 * UR, :].astype(BF16) for i in nu]
    tm = [eye + a_ab[i] for i in nu]
    pw = [a_ab[i].astype(BF16) for i in nu]
    pw = [_dot(pw[i], pw[i]).astype(BF16) for i in nu]
    for _ in range(4):
        res = [_dot(pw[i], jnp.concatenate([pw[i], tm[i].astype(BF16)], axis=1)) for i in nu]
        pw = [res[i][:, 0:UR].astype(BF16) for i in nu]
        tm = [tm[i] + res[i][:, UR:2 * UR] for i in nu]
    tm = [tm[i] + _dot(pw[i], tm[i]) for i in nu]
    akv = [_dot(a_ak[i], vu[i]) for i in nu]
    tmb = [tm[i].astype(BF16) for i in nu]
    w_all = [_dot(tmb[i], at[i]).astype(BF16) for i in nu]
    u0_all = [_dot(tmb[i], akv[i]).astype(BF16) for i in nu]
    q_all = [(rt[i] + _dot(a_rbk[i][:, 0:UR], w_all[i])).astype(BF16) for i in nu]
    y0_all = [_dot(a_rbk[i], jnp.concatenate([u0_all[i], vu[i]], axis=0)) for i in nu]
    gm, hm, pcol, qc, y0c = {}, {}, {}, {}, {}
    for i, (h, u) in enumerate(units):
        for cc in range(UR // L):
            c = u * (UR // L) + cc
            ls = slice(cc * L, (cc + 1) * L)
            rs = slice(c * L, (c + 1) * L)
            gm[c, h] = _dot(bh_t[h][:, rs], w_all[i][ls]).astype(BF16)
            hm[c, h] = _dot(bh_t[h][:, rs], u0_all[i][ls]) + _dot(kh_t[h][:, rs], vu[i][ls])
            pcol[c, h] = jnp.exp(cum_t[h][:, (c + 1) * L - 1:(c + 1) * L])
            qc[c, h] = jnp.concatenate([q_all[i][ls], gm[c, h]], axis=0)
            y0c[c, h] = y0_all[i][ls]
    zst = [st_ref[h] for h in heads]
    for c in range(nck):
        rs = slice(c * L, (c + 1) * L)
        for h in heads:
            qz = _dot(qc[c, h], zst[h])
            y_ref[rs, hsl[h]] = qz[0:L] + y0c[c, h]
            zst[h] = pcol[c, h] * zst[h] + qz[L:2 * L] + hm[c, h]
    for h in heads:
        st_ref[h] = zst[h]

    y = y_ref[...]
    mean = _segsum(y, seg) * (1.0 / HEAD_DIM)
    yc = y - mean
    var = _segsum(yc * yc, seg) * (1.0 / HEAD_DIM)
    yn = yc * lax.rsqrt(var + GN_EPS) * ln_w + ln_b
    bonus = _segsum(r * k_mod * r_k, seg) * v
    o_ref[...] = (yn + bonus) * g


def _rwkv(zr, mu, pv, wup, aup, gup, seg, *, tt):
    b, t, _ = zr.shape
    full = lambda bb, ti: (0, 0)
    return pl.pallas_call(
        functools.partial(_rwkv_kernel, tt=tt),
        grid=(b, t // tt),
        in_specs=[
            pl.BlockSpec((None, tt, N_RWKV_COLS), lambda bb, ti: (bb, ti, 0)),
            pl.BlockSpec((1, N_RWKV_COLS), full),
            pl.BlockSpec((8, RWKV_WIDTH), full),
            pl.BlockSpec((64, RWKV_WIDTH), full),
            pl.BlockSpec((64, RWKV_WIDTH), full),
            pl.BlockSpec((128, RWKV_WIDTH), full),
            pl.BlockSpec((4 * HEAD_DIM, 4 * HEAD_DIM), full),
        ],
        out_specs=pl.BlockSpec((None, tt, RWKV_WIDTH), lambda bb, ti: (bb, ti, 0)),
        out_shape=jax.ShapeDtypeStruct((b, t, RWKV_WIDTH), F32),
        scratch_shapes=[
            pltpu.VMEM((8, N_RWKV_COLS), F32),
            pltpu.VMEM((RWKV_HEADS, HEAD_DIM, HEAD_DIM), F32),
            pltpu.VMEM((tt, RWKV_WIDTH), F32),
        ],
        compiler_params=_cparams(("parallel", "arbitrary")),
        name="rwkv7",
    )(zr, mu, pv, wup, aup, gup, seg)


def _tail_kernel(x_ref, ya_ref, yb_ref, yc_ref, p_ref, g_ref, wm_ref, wi_ref, wo_ref, wg_ref, wp_ref, o_ref,
                 h_ref, *, tf):
    c0, c1 = LRU_WIDTH, LRU_WIDTH + MLSTM_WIDTH
    mix = (jnp.dot(ya_ref[...].astype(BF16), wm_ref[0:c0, :], preferred_element_type=F32)
           + jnp.dot(yb_ref[...].astype(BF16), wm_ref[c0:c1, :], preferred_element_type=F32)
           + jnp.dot(yc_ref[...].astype(BF16), wm_ref[c1:, :], preferred_element_type=F32))
    x = x_ref[...] + _rms(mix, g_ref[3:4, :])
    xn = _rms(x, g_ref[4:5, :]).astype(BF16)
    for c in range(D_FF // tf):
        gate = jnp.dot(xn, wi_ref[:, c * tf:(c + 1) * tf], preferred_element_type=F32)
        up = jnp.dot(xn, wi_ref[:, D_FF + c * tf:D_FF + (c + 1) * tf], preferred_element_type=F32)
        h_ref[:, c * tf:(c + 1) * tf] = (gate * jax.nn.sigmoid(gate) * up).astype(BF16)
    y = jnp.dot(h_ref[...], wo_ref[...], preferred_element_type=F32)
    x = x + 0.5 * _rms(y, g_ref[5:6, :])
    pgate = jax.nn.sigmoid(jnp.dot(_rms(x, g_ref[6:7, :]).astype(BF16), wg_ref[...], preferred_element_type=F32))
    pe = jnp.dot(p_ref[...].astype(BF16), wp_ref[...], preferred_element_type=F32)
    o_ref[...] = x + _rms(pgate * pe, g_ref[7:8, :])


def _tail(x, ya, yb, yc, p, g8, w_mix, w_in, w_out, w_gate, w_proj, *, tm, tf):
    n = x.shape[0]
    row = lambda i: (i, 0)
    const = lambda shape: pl.BlockSpec(shape, lambda i: (0, 0), pipeline_mode=pl.Buffered(1))
    return pl.pallas_call(
        functools.partial(_tail_kernel, tf=tf),
        grid=(n // tm,),
        in_specs=[
            pl.BlockSpec((tm, D_MODEL), row),
            pl.BlockSpec((tm, LRU_WIDTH), row),
            pl.BlockSpec((tm, MLSTM_WIDTH), row),
            pl.BlockSpec((tm, RWKV_WIDTH), row),
            pl.BlockSpec((tm, PLE_DIM), row),
            const((8, D_MODEL)),
            const((D_MODEL, D_MODEL)),
            const((D_MODEL, 2 * D_FF)),
            const((D_FF, D_MODEL)),
            const((D_MODEL, D_MODEL)),
            const((PLE_DIM, D_MODEL)),
        ],
        out_specs=pl.BlockSpec((tm, D_MODEL), row),
        out_shape=jax.ShapeDtypeStruct((n, D_MODEL), F32),
        scratch_shapes=[pltpu.VMEM((tm, D_FF), BF16)],
        compiler_params=_cparams(("parallel",)),
        name="tail",
    )(x, ya, yb, yc, p, g8, w_mix, w_in, w_out, w_gate, w_proj)


def _block_diag(w):
    nh, dd, _ = w.shape
    eye = jnp.eye(nh, dtype=w.dtype)
    return (eye[:, None, :, None] * w[:, :, None, :]).reshape(nh * dd, nh * dd)


def _tile(n, pref):
    return pref if n % pref == 0 else n


def kernel(x, p, norm_g, ffn_w_in, ffn_w_out, w_in, w_out, lru_conv_w, lru_conv_b, lru_w_a, lru_b_a, lru_w_x, lru_b_x, lru_lambda, m_b_i, m_b_f, m_norm, rw_mu, rw_w0, rw_w_up, rw_a0, rw_a_up, rw_g_up, rw_k_k, rw_k_a, rw_r_k, rw_ln_w, rw_ln_b, ple_w_proj, ple_w_gate):
    bsz, t, _ = x.shape
    n = bsz * t
    depth = norm_g.shape[0]
    tm_ffn = _tile(n, 512)
    tm = _tile(n, 512)
    tt_lru = _tile(t, 512)
    tt_m = _tile(t, 512)
    tt_r = _tile(t, 256)
    seg = _block_diag(jnp.ones((4, HEAD_DIM, HEAD_DIM), BF16))

    xf = x.reshape(n, D_MODEL)
    for l in range(depth):
        g = norm_g[l]
        wl = w_in[l]
        o_zr = 2 * LRU_WIDTH + 4 * MLSTM_WIDTH + 2 * MLSTM_HEADS
        w_gates = wl[:, o_zr - 2 * MLSTM_HEADS:o_zr]
        wcat = jnp.concatenate(
            [wl[:, o_zr:], jnp.pad(w_gates, ((0, 0), (0, ZG_COLS - 2 * MLSTM_HEADS))),
             wl[:, 0:ZL_COLS], wl[:, ZL_COLS:ZL_COLS + ZM_COLS]], axis=1).astype(BF16)
        wt = wl[:, ZL_COLS + MLSTM_WIDTH:ZL_COLS + 2 * MLSTM_WIDTH].T.astype(BF16)

        xf = _ffn(xf, g[0:2], ffn_w_in[l, 0].astype(BF16), ffn_w_out[l, 0].astype(BF16), tm=tm_ffn, tf=256)

        zr, zl, zm, zg, ktg = _mixer_in(xf, g[2:3], wcat, wt, tm=tm)

        lru_pv = jnp.concatenate([lru_conv_w[l], lru_conv_b[l][None], lru_b_a[l][None], lru_b_x[l][None],
                                  lru_lambda[l][None]], axis=0)
        ya = _lru(zl.reshape(bsz, t, ZL_COLS), lru_pv, _block_diag(lru_w_a[l]).astype(BF16),
                  _block_diag(lru_w_x[l]).astype(BF16), tt=tt_lru)

        gate_b = jnp.concatenate([m_b_i[l], m_b_f[l]])
        bcol = jnp.broadcast_to(gate_b[:, None], (8, 128))
        brow = jnp.broadcast_to(jnp.pad(gate_b, (0, ZG_COLS - 8))[None, :], (8, ZG_COLS))
        nrm = jnp.broadcast_to(m_norm[l][None, :], (8, MLSTM_WIDTH))
        yb = _mlstm(zm.reshape(bsz, t, ZM_COLS), zg.reshape(bsz, t, ZG_COLS), ktg, bcol, brow, nrm, tt=tt_m)

        rw_pv = jnp.stack([rw_w0[l], rw_a0[l], rw_k_k[l], rw_k_a[l], rw_r_k[l].reshape(-1), rw_ln_w[l],
                           rw_ln_b[l], jnp.zeros((RWKV_WIDTH,), F32)], axis=0)
        yc = _rwkv(zr.reshape(bsz, t, ZR_COLS), rw_mu[l][None, :], rw_pv, rw_w_up[l].astype(BF16),
                   rw_a_up[l].astype(BF16), rw_g_up[l].astype(BF16), seg, tt=tt_r)

        xf = _tail(xf, ya.reshape(n, LRU_WIDTH), yb.reshape(n, MLSTM_WIDTH), yc.reshape(n, RWKV_WIDTH),
                   p[l].reshape(n, PLE_DIM), g, w_out[l].astype(BF16), ffn_w_in[l, 1].astype(BF16),
                   ffn_w_out[l, 1].astype(BF16), ple_w_gate[l].astype(BF16), ple_w_proj[l].astype(BF16),
                   tm=tm_ffn, tf=256)
    return xf.reshape(bsz, t, D_MODEL)
```

```python
import functools

import jax
import jax.numpy as jnp
from jax import lax
from jax.experimental import pallas as pl
from jax.experimental.pallas import tpu as pltpu

F32 = jnp.float32
BF16 = jnp.bfloat16

D_MODEL = 1024
PLE_DIM = 256
D_FF = 2816
RMS_EPS = 1e-6
LRU_WIDTH = 384
LRU_HEADS = 6
CONV_WIDTH = 4
LRU_C = 8.0
MLSTM_HEADS = 4
MLSTM_WIDTH = 256
RWKV_HEADS = 6
RWKV_WIDTH = 384
N_RWKV_COLS = 1408
GN_EPS = 64e-5
HEAD_DIM = 64
CHUNK = 64

ZR_COLS = N_RWKV_COLS
ZL_COLS = 2 * LRU_WIDTH
ZM_COLS = 4 * MLSTM_WIDTH
ZG_COLS = 128
KT_ROWS = MLSTM_WIDTH + 8

VMEM_LIMIT = 56 * 1024 * 1024


def _cparams(sem):
    return pltpu.CompilerParams(dimension_semantics=sem, vmem_limit_bytes=VMEM_LIMIT)


def _rms(x, g):
    return x * lax.rsqrt(jnp.mean(x * x, axis=-1, keepdims=True) + RMS_EPS) * g


def _softplus(x):
    return jnp.maximum(x, 0.0) + jnp.log(1.0 + jnp.exp(-jnp.abs(x)))


def _log_sigmoid(x):
    return -_softplus(-x)


def _dot(a, b):
    return jnp.dot(a.astype(BF16), b.astype(BF16), preferred_element_type=F32)


def _dot_nt(a, b):
    return lax.dot_general(a.astype(BF16), b.astype(BF16), (((1,), (1,)), ((), ())),
                           preferred_element_type=F32)


def _segsum(x, seg):
    hi = x.astype(BF16)
    lo = (x - hi.astype(F32)).astype(BF16)
    w = seg.shape[0]
    head = (jnp.dot(hi[:, 0:w], seg, preferred_element_type=F32)
            + jnp.dot(lo[:, 0:w], seg, preferred_element_type=F32))
    tail = jnp.dot(jnp.concatenate([hi[:, w:], lo[:, w:]], axis=1), seg, preferred_element_type=F32)
    return jnp.concatenate([head, tail[:, 0:w // 2] + tail[:, w // 2:]], axis=1)


def _ffn_kernel(x_ref, g_ref, wi_ref, wo_ref, o_ref, h_ref, *, tf):
    x = x_ref[...]
    xn = _rms(x, g_ref[0:1, :]).astype(BF16)
    for c in range(D_FF // tf):
        gate = jnp.dot(xn, wi_ref[:, c * tf:(c + 1) * tf], preferred_element_type=F32)
        up = jnp.dot(xn, wi_ref[:, D_FF + c * tf:D_FF + (c + 1) * tf], preferred_element_type=F32)
        h_ref[:, c * tf:(c + 1) * tf] = (gate * jax.nn.sigmoid(gate) * up).astype(BF16)
    y = jnp.dot(h_ref[...], wo_ref[...], preferred_element_type=F32)
    o_ref[...] = x + 0.5 * _rms(y, g_ref[1:2, :])


def _ffn(x, g2, w_in, w_out, *, tm, tf):
    n = x.shape[0]
    const = dict(pipeline_mode=pl.Buffered(1))
    return pl.pallas_call(
        functools.partial(_ffn_kernel, tf=tf),
        grid=(n // tm,),
        in_specs=[
            pl.BlockSpec((tm, D_MODEL), lambda i: (i, 0)),
            pl.BlockSpec((2, D_MODEL), lambda i: (0, 0)),
            pl.BlockSpec((D_MODEL, 2 * D_FF), lambda i: (0, 0), **const),
            pl.BlockSpec((D_FF, D_MODEL), lambda i: (0, 0), **const),
        ],
        out_specs=pl.BlockSpec((tm, D_MODEL), lambda i: (i, 0)),
        out_shape=jax.ShapeDtypeStruct((n, D_MODEL), F32),
        scratch_shapes=[pltpu.VMEM((tm, D_FF), BF16)],
        compiler_params=_cparams(("parallel",)),
        name="ffn",
    )(x, g2, w_in, w_out)


def _lru_body(z_ref, pv_ref, wa_ref, wx_ref, o_ref, xbuf_ref, h_ref, al_ref, ul_ref, c_ref, *, tt, emit):
    t = pl.program_id(1)

    @pl.when(t == 0)
    def _():
        xbuf_ref[0:8, :] = jnp.zeros((8, LRU_WIDTH), F32)
        h_ref[...] = jnp.zeros_like(h_ref)

    x = z_ref[:, 0:LRU_WIDTH]
    gate = z_ref[:, LRU_WIDTH:2 * LRU_WIDTH]
    pv = pv_ref[...]
    xbuf_ref[8:8 + tt, :] = x
    xa = pv[4:5, :] + pv[3:4, :] * x
    for j in range(CONV_WIDTH - 1):
        xa = xa + pv[j:j + 1, :] * xbuf_ref[pl.ds(5 + j, tt), :]
    xbuf_ref[0:8, :] = x[tt - 8:tt, :]
    emit()

    r = jax.nn.sigmoid(_dot(xa, wa_ref[...]) + pv[5:6, :])
    i = jax.nn.sigmoid(_dot(xa, wx_ref[...]) + pv[6:7, :])
    log_a = (-LRU_C) * r * _softplus(-pv[7:8, :])
    a = jnp.exp(log_a)
    u = jnp.sqrt(1.0 - jnp.exp(2.0 * log_a)) * (i * xa)
    emit()

    ng = tt // 8
    a3 = a.reshape(ng, 8, LRU_WIDTH)
    u3 = u.reshape(ng, 8, LRU_WIDTH)
    sub = lax.broadcasted_iota(jnp.int32, (ng, 8, LRU_WIDTH), 1)
    for d in (1, 2, 4):
        keep = sub >= d
        a_sh = jnp.where(keep, pltpu.roll(a3, d, axis=1), 1.0)
        u_sh = jnp.where(keep, pltpu.roll(u3, d, axis=1), 0.0)
        u3 = a3 * u_sh + u3
        a3 = a3 * a_sh
        emit()
    a_l = a3.reshape(tt, LRU_WIDTH)
    u_l = u3.reshape(tt, LRU_WIDTH)
    nslab = LRU_WIDTH // 128
    for j in range(nslab):
        al_ref[j] = a_l[:, 128 * j:128 * (j + 1)]
        ul_ref[j] = u_l[:, 128 * j:128 * (j + 1)]
    ag = jnp.concatenate([al_ref[j, pl.ds(7, ng, stride=8), :] for j in range(nslab)], axis=1)
    ug = jnp.concatenate([ul_ref[j, pl.ds(7, ng, stride=8), :] for j in range(nslab)], axis=1)
    rowg = lax.broadcasted_iota(jnp.int32, (ng, LRU_WIDTH), 0)
    d = 1
    while d < ng:
        keep = rowg >= d
        a_sh = jnp.where(keep, pltpu.roll(ag, d, axis=0), 1.0)
        u_sh = jnp.where(keep, pltpu.roll(ug, d, axis=0), 0.0)
        ug = ag * u_sh + ug
        ag = ag * a_sh
        d *= 2
    h0 = h_ref[0:1, :]
    h_end = ug + ag * h0
    h_ref[...] = jnp.broadcast_to(h_end[ng - 1:ng, :], h_ref.shape)
    c_ref[...] = jnp.where(rowg == 0, h0, pltpu.roll(h_end, 1, axis=0))
    gg = jax.nn.gelu(gate, approximate=True)
    for gi in range(ng):
        rs = slice(8 * gi, 8 * gi + 8)
        o_ref[rs, :] = (u_l[rs, :] + a_l[rs, :] * c_ref[gi:gi + 1, :]) * gg[rs, :]
        if gi % 16 == 15:
            emit()


def _mlstm_body(q_ref, k_ref, v_ref, og_ref, gc_ref, kt_ref, bcol_ref, brow_ref, nrm_ref, o_ref,
                cn_ref, m_ref, *, tt, emit):
    t = pl.program_id(1)

    @pl.when(t == 0)
    def _():
        cn_ref[...] = jnp.zeros_like(cn_ref)
        m_ref[...] = jnp.zeros_like(m_ref)

    L = CHUNK
    gr = kt_ref[MLSTM_WIDTH:MLSTM_WIDTH + 8, :] + bcol_ref[:, 0:1]
    br = _log_sigmoid(gr)
    pos = lax.broadcasted_iota(jnp.int32, (8, tt), 1) & (L - 1)
    d = 1
    while d < L:
        br = br + jnp.where(pos >= d, pltpu.roll(br, d, axis=1), 0.0)
        d *= 2
    bc = _log_sigmoid(gc_ref[...] + brow_ref[0:1, :])
    posc = lax.broadcasted_iota(jnp.int32, (tt, ZG_COLS), 0) & (L - 1)
    d = 1
    while d < L:
        bc = bc + jnp.where(posc >= d, pltpu.roll(bc, d, axis=0), 0.0)
        d *= 2

    nck = tt // L
    heads = range(MLSTM_HEADS)
    units = [(c, h) for c in range(nck) for h in heads]
    rsl = [slice(c * L, (c + 1) * L) for c in range(nck)]
    hsl = [slice(HEAD_DIM * h, HEAD_DIM * (h + 1)) for h in heads]
    psl = [slice(128 * (h // 2), 128 * (h // 2) + 128) for h in heads]
    jsl = [slice(128 * (c // 2), 128 * (c // 2) + 128) for c in range(nck)]

    lane = lax.broadcasted_iota(jnp.int32, (L, 128), 1)
    rowi = lax.broadcasted_iota(jnp.int32, (L, 128), 0)
    half = [lane < HEAD_DIM, lane >= HEAD_DIM]
    den_lane = [HEAD_DIM, HEAD_DIM - 1]
    ones_col = [jnp.where(lane == den_lane[e], 1.0, 0.0) for e in range(2)]
    cmask = [(lane >= L * cc) & (lane < L * (cc + 1)) & (lane - L * cc <= rowi) for cc in range(2)]
    lane1 = lax.broadcasted_iota(jnp.int32, (1, 128), 1)
    rmask = [lane1 < L, lane1 >= L]

    b_row = {(c, h): br[4 + h:5 + h, jsl[c]] for c, h in units}
    i_row = {(c, h): gr[h:h + 1, jsl[c]] for c, h in units}

    g = {(c, h): br[4 + h:5 + h, (c + 1) * L - 1:(c + 1) * L] for c, h in units}
    w_log = {u: jnp.where(rmask[u[0] % 2], g[u] - b_row[u] + i_row[u], -jnp.inf) for u in units}
    w_max = {u: jnp.max(w_log[u], axis=1, keepdims=True) for u in units}
    m_in, m_out = {}, {}
    for h in heads:
        m_st = m_ref[h, 0:1, 0:1]
        for c in range(nck):
            m_in[c, h] = m_st
            m_st = jnp.maximum(g[c, h] + m_st, w_max[c, h])
            m_out[c, h] = m_st
        m_ref[h] = jnp.broadcast_to(m_st, (8, 128))
    dec = {u: jnp.exp(g[u] + m_in[u] - m_out[u]) for u in units}
    wk = {u: jnp.exp(w_log[u] - m_out[u]) for u in units}
    emit()

    q_m = {(c, h): (jnp.where(half[h % 2], q_ref[rsl[c], psl[h]], 0.0) * (HEAD_DIM ** -0.5)).astype(BF16)
           for c, h in units}
    k2 = {(c, h): k_ref[jsl[c], psl[h]].astype(BF16) for c, h in units}
    vx = {(c, h): jnp.where(jnp.concatenate([half[h % 2]] * 2, axis=0), v_ref[jsl[c], psl[h]],
                            jnp.concatenate([ones_col[h % 2]] * 2, axis=0)).astype(BF16) for c, h in units}
    bcb = {(c, h): jnp.broadcast_to(bc[rsl[c], 4 + h:5 + h], (L, 128)) for c, h in units}
    emit()
    dmat = {u: jnp.where(cmask[u[0] % 2], bcb[u] - b_row[u] + i_row[u], -jnp.inf) for u in units}
    m_loc = {u: jnp.max(dmat[u], axis=1, keepdims=True) for u in units}
    p_in = {u: jnp.exp(dmat[u] - m_loc[u]) for u in units}
    emit()
    qk = {u: _dot_nt(q_m[u], k2[u]) for u in units}
    nd = {u: _dot(p_in[u] * qk[u], vx[u]) for u in units}
    kv = {(c, h): _dot(kt_ref[hsl[h], jsl[c]] * wk[c, h], vx[c, h]) for c, h in units}
    emit()

    cn_in = {}
    for h in heads:
        cn = cn_ref[h]
        for c in range(nck):
            cn_in[c, h] = cn
            cn = dec[c, h] * cn + kv[c, h]
        cn_ref[h] = cn
    cn_pair = {(c, p): jnp.concatenate([cn_in[c, 2 * p], cn_in[c, 2 * p + 1]], axis=0).astype(BF16)
               for c in range(nck) for p in range(MLSTM_HEADS // 2)}

    inter = {u: bcb[u] + m_in[u] for u in units}
    mj = {u: jnp.maximum(m_loc[u], inter[u]) for u in units}
    e_loc = {u: jnp.exp(m_loc[u] - mj[u]) for u in units}
    e_int = {u: jnp.exp(inter[u] - mj[u]) for u in units}
    e_neg = {u: jnp.exp(-mj[u]) for u in units}
    emit()
    qc = {(c, h): _dot(q_m[c, h], cn_pair[c, h // 2]) for c, h in units}
    numden = {u: e_loc[u] * nd[u] + e_int[u] * qc[u] for u in units}
    den = {(c, h): numden[c, h][:, den_lane[h % 2]:den_lane[h % 2] + 1] for c, h in units}
    hv = {u: numden[u] / jnp.maximum(jnp.abs(den[u]), e_neg[u]) for u in units}
    emit()
    ms = {(c, h): jnp.sum(jnp.where(half[h % 2], hv[c, h] * hv[c, h], 0.0), axis=1, keepdims=True)
          * (1.0 / HEAD_DIM) for c, h in units}
    hn = {u: hv[u] * lax.rsqrt(ms[u] + RMS_EPS) for u in units}
    for c in range(nck):
        for p in range(MLSTM_HEADS // 2):
            ps = psl[2 * p]
            hb = jnp.where(half[0], hn[c, 2 * p], hn[c, 2 * p + 1])
            o_ref[rsl[c], ps] = hb * nrm_ref[0:1, ps] * jax.nn.sigmoid(og_ref[rsl[c], ps])


def _mixer_kernel(x_ref, g_ref, w_ref, wt_ref, lpv_ref, wa_ref, wx_ref, bcol_ref, brow_ref, nrm_ref,
                  zr_ref, ya_ref, yb_ref,
                  zl_s, zm_s, zg_s, kt_s, xbuf_s, h_s, al_s, ul_s, c_s, cn_s, m_s, *, tt):
    xn = _rms(x_ref[...], g_ref[...]).astype(BF16)
    c_zl = ZR_COLS + ZG_COLS
    c_zm = c_zl + ZL_COLS
    zl_s[...] = jnp.dot(xn, w_ref[:, c_zl:c_zm], preferred_element_type=F32)

    def zm_chunk(j):
        zm_s[:, j:j + 256] = jnp.dot(xn, w_ref[:, c_zm + j:c_zm + j + 256], preferred_element_type=F32)

    def kt_chunk():
        kt_s[0:MLSTM_WIDTH, :] = lax.dot_general(wt_ref[...], xn, (((1,), (1,)), ((), ())),
                                                 preferred_element_type=F32)

    def zg_chunk():
        lo = ZR_COLS - 128
        res = jnp.dot(xn, w_ref[:, lo:c_zl], preferred_element_type=F32)
        zr_ref[:, lo:ZR_COLS] = res[:, 0:128]
        zg = res[:, 128:256]
        zg_s[...] = zg
        kt_s[MLSTM_WIDTH:KT_ROWS, :] = zg.T[0:KT_ROWS - MLSTM_WIDTH, :]

    def zr_chunk(j):
        zr_ref[:, j:j + 256] = jnp.dot(xn, w_ref[:, j:j + 256], preferred_element_type=F32)

    def emitter(thunks):
        pending = list(thunks)

        def emit(flush=False):
            while pending:
                pending.pop(0)()
                if not flush:
                    break
        return emit

    emit = emitter([functools.partial(zm_chunk, j) for j in range(0, ZM_COLS, 256)] + [kt_chunk, zg_chunk])
    _lru_body(zl_s, lpv_ref, wa_ref, wx_ref, ya_ref, xbuf_s, h_s, al_s, ul_s, c_s, tt=tt, emit=emit)
    emit(flush=True)
    emit = emitter([functools.partial(zr_chunk, j) for j in range(0, ZR_COLS - 128, 256)])
    wm = MLSTM_WIDTH
    _mlstm_body(zm_s.at[:, 0:wm], zm_s.at[:, wm:2 * wm], zm_s.at[:, 2 * wm:3 * wm], zm_s.at[:, 3 * wm:4 * wm],
                zg_s, kt_s, bcol_ref, brow_ref, nrm_ref, yb_ref, cn_s, m_s, tt=tt, emit=emit)
    emit(flush=True)


def _mixer(x, g, wcat, wt, lru_pv, wa, wx, bcol, brow, nrm, *, tt):
    b, t, _ = x.shape
    tile = lambda w: pl.BlockSpec((None, tt, w), lambda bb, ti: (bb, ti, 0))
    const = lambda shape: pl.BlockSpec(shape, lambda bb, ti: (0, 0), pipeline_mode=pl.Buffered(1))
    return pl.pallas_call(
        functools.partial(_mixer_kernel, tt=tt),
        grid=(b, t // tt),
        in_specs=[
            tile(D_MODEL),
            const((1, D_MODEL)),
            const(wcat.shape),
            const((MLSTM_WIDTH, D_MODEL)),
            const((8, LRU_WIDTH)),
            const((LRU_WIDTH, LRU_WIDTH)),
            const((LRU_WIDTH, LRU_WIDTH)),
            const((8, 128)),
            const((8, ZG_COLS)),
            const((8, MLSTM_WIDTH)),
        ],
        out_specs=[tile(ZR_COLS), tile(LRU_WIDTH), tile(MLSTM_WIDTH)],
        out_shape=[
            jax.ShapeDtypeStruct((b, t, ZR_COLS), F32),
            jax.ShapeDtypeStruct((b, t, LRU_WIDTH), F32),
            jax.ShapeDtypeStruct((b, t, MLSTM_WIDTH), F32),
        ],
        scratch_shapes=[
            pltpu.VMEM((tt, ZL_COLS), F32), pltpu.VMEM((tt, ZM_COLS), F32), pltpu.VMEM((tt, ZG_COLS), F32),
            pltpu.VMEM((KT_ROWS, tt), F32),
            pltpu.VMEM((tt + 8, LRU_WIDTH), F32), pltpu.VMEM((8, LRU_WIDTH), F32),
            pltpu.VMEM((LRU_WIDTH // 128, tt, 128), F32), pltpu.VMEM((LRU_WIDTH // 128, tt, 128), F32),
            pltpu.VMEM((tt // 8, LRU_WIDTH), F32),
            pltpu.VMEM((MLSTM_HEADS, HEAD_DIM, 128), F32), pltpu.VMEM((MLSTM_HEADS, 8, 128), F32),
        ],
        compiler_params=_cparams(("parallel", "arbitrary")),
        name="mixer",
    )(x, g, wcat, wt, lru_pv, wa, wx, bcol, brow, nrm)


def _rwkv_kernel(z_ref, mu_ref, pv_ref, wup_ref, aup_ref, gup_ref, seg_ref, o_ref,
                 prev_ref, st_ref, y_ref, *, tt):
    t = pl.program_id(1)

    @pl.when(t == 0)
    def _():
        prev_ref[...] = jnp.zeros_like(prev_ref)
        st_ref[...] = jnp.zeros_like(st_ref)

    L = CHUNK
    W = RWKV_WIDTH
    nck = tt // L
    z = z_ref[...]
    row1 = lax.broadcasted_iota(jnp.int32, (tt, 1), 0)
    zprev = jnp.where(row1 == 0, prev_ref[0:1, :], pltpu.roll(z, 1, axis=0))
    prev_ref[...] = jnp.broadcast_to(z[tt - 1:tt, :], prev_ref.shape)
    zs = z + (zprev - z) * mu_ref[...]
    r = zs[:, 0:W]
    k = zs[:, W:2 * W]
    v = zs[:, 2 * W:3 * W]
    wd = zs[:, 3 * W:3 * W + 64]
    ad = zs[:, 3 * W + 64:3 * W + 128]
    gd = zs[:, 3 * W + 128:3 * W + 256]
    pv = pv_ref[...]
    w0, a0, k_k, k_a, r_k, ln_w, ln_b = (pv[j:j + 1, :] for j in range(7))
    seg = seg_ref[...]

    log_w = -_softplus(-(w0 + _dot(jnp.tanh(wd), wup_ref[...]))) - 0.5
    ld = -jnp.exp(log_w)
    iclr = jax.nn.sigmoid(a0 + _dot(ad, aup_ref[...]))
    g = _dot(jax.nn.sigmoid(gd), gup_ref[...])
    kk = k * k_k
    kk = kk / jnp.maximum(jnp.sqrt(_segsum(kk * kk, seg)), 1e-12)
    k_mod = k * (1.0 + (iclr - 1.0) * k_a)
    a_vec = -kk
    b_vec = kk * iclr

    cum = ld
    posr = lax.broadcasted_iota(jnp.int32, (tt, W), 0) & (L - 1)
    d = 1
    while d < L:
        cum = cum + jnp.where(posr >= d, pltpu.roll(cum, d, axis=0), 0.0)
        d *= 2
    cum_last = jnp.concatenate(
        [jnp.broadcast_to(cum[(c + 1) * L - 1:(c + 1) * L, :], (L, W)) for c in range(nck)], axis=0)
    e_neg = jnp.exp(-cum)
    e_tail = jnp.exp(cum_last - cum)
    a_t = jnp.exp(cum - ld) * a_vec
    r_t = jnp.exp(cum) * r
    b_t = e_neg * b_vec
    k_t = e_neg * k_mod
    b_h = e_tail * b_vec
    k_h = e_tail * k_mod

    UR = 2 * L
    ri = lax.broadcasted_iota(jnp.int32, (UR, UR), 0)
    ci = lax.broadcasted_iota(jnp.int32, (UR, UR), 1)
    same = (ri // L) == (ci // L)
    strict = jnp.where(same & (ri > ci), 1.0, 0.0)
    incl = jnp.where(same & (ri >= ci), 1.0, 0.0)
    eye = jnp.where(ri == ci, 1.0, 0.0)

    heads = range(RWKV_HEADS)
    hsl = [slice(HEAD_DIM * h, HEAD_DIM * (h + 1)) for h in heads]
    bh_t, kh_t, cum_t = [], [], []
    for pair in range(RWKV_HEADS // 2):
        ps = slice(128 * pair, 128 * pair + 128)
        bp, kp, cp = b_h[:, ps].T.astype(BF16), k_h[:, ps].T.astype(BF16), cum[:, ps].T
        for e in range(2):
            es = slice(HEAD_DIM * e, HEAD_DIM * (e + 1))
            bh_t.append(bp[es])
            kh_t.append(kp[es])
            cum_t.append(cp[es])

    units = [(h, u) for u in range(tt // UR) for h in heads]
    nu = range(len(units))
    usl = [slice(UR * u, UR * (u + 1)) for _, u in units]
    at = [a_t[usl[i], hsl[h]].astype(BF16) for i, (h, _) in enumerate(units)]
    rt = [r_t[usl[i], hsl[h]] for i, (h, _) in enumerate(units)]
    bt = [b_t[usl[i], hsl[h]].astype(BF16) for i, (h, _) in enumerate(units)]
    kt = [k_t[usl[i], hsl[h]].astype(BF16) for i, (h, _) in enumerate(units)]
    vu = [v[usl[i], hsl[h]].astype(BF16) for i, (h, _) in enumerate(units)]
    mask4 = jnp.concatenate([jnp.concatenate([strict, strict], axis=1),
                             jnp.concatenate([incl, incl], axis=1)], axis=0)
    prod = [mask4 * _dot_nt(jnp.concatenate([at[i], rt[i].astype(BF16)], axis=0),
                            jnp.concatenate([bt[i], kt[i]], axis=0)) for i in nu]
    a_ab = [prod[i][0:UR, 0:UR] for i in nu]
    a_ak = [prod[i][0:UR, UR:2 * UR].astype(BF16) for i in nu]
    a_rbk = [prod[i][UR:2 * UR, :].astype(BF16) for i in nu]
    tm = [eye + a_ab[i] for i in nu]
    pw = [a_ab[i].astype(BF16) for i in nu]
    pw = [_dot(pw[i], pw[i]).astype(BF16) for i in nu]
    for _ in range(4):
        res = [_dot(pw[i], jnp.concatenate([pw[i], tm[i].astype(BF16)], axis=1)) for i in nu]
        pw = [res[i][:, 0:UR].astype(BF16) for i in nu]
        tm = [tm[i] + res[i][:, UR:2 * UR] for i in nu]
    tm = [tm[i] + _dot(pw[i], tm[i]) for i in nu]
    akv = [_dot(a_ak[i], vu[i]) for i in nu]
    tmb = [tm[i].astype(BF16) for i in nu]
    w_all = [_dot(tmb[i], at[i]).astype(BF16) for i in nu]
    u0_all = [_dot(tmb[i], akv[i]).astype(BF16) for i in nu]
    q_all = [(rt[i] + _dot(a_rbk[i][:, 0:UR], w_all[i])).astype(BF16) for i in nu]
    y0_all = [_dot(a_rbk[i], jnp.concatenate([u0_all[i], vu[i]], axis=0)) for i in nu]
    gm, hm, pcol, qc, y0c = {}, {}, {}, {}, {}
    for i, (h, u) in enumerate(units):
        for cc in range(UR // L):
            c = u * (UR // L) + cc
            ls = slice(cc * L, (cc + 1) * L)
            rs = slice(c * L, (c + 1) * L)
            gm[c, h] = _dot(bh_t[h][:, rs], w_all[i][ls]).astype(BF16)
            hm[c, h] = _dot(bh_t[h][:, rs], u0_all[i][ls]) + _dot(kh_t[h][:, rs], vu[i][ls])
            pcol[c, h] = jnp.exp(cum_t[h][:, (c + 1) * L - 1:(c + 1) * L])
            qc[c, h] = jnp.concatenate([q_all[i][ls], gm[c, h]], axis=0)
            y0c[c, h] = y0_all[i][ls]
    zst = [st_ref[h] for h in heads]
    for c in range(nck):
        rs = slice(c * L, (c + 1) * L)
        for h in heads:
            qz = _dot(qc[c, h], zst[h])
            y_ref[rs, hsl[h]] = qz[0:L] + y0c[c, h]
            zst[h] = pcol[c, h] * zst[h] + qz[L:2 * L] + hm[c, h]
    for h in heads:
        st_ref[h] = zst[h]

    y = y_ref[...]
    mean = _segsum(y, seg) * (1.0 / HEAD_DIM)
    yc = y - mean
    var = _segsum(yc * yc, seg) * (1.0 / HEAD_DIM)
    yn = yc * lax.rsqrt(var + GN_EPS) * ln_w + ln_b
    bonus = _segsum(r * k_mod * r_k, seg) * v
    o_ref[...] = (yn + bonus) * g


def _rwkv(zr, mu, pv, wup, aup, gup, seg, *, tt):
    b, t, _ = zr.shape
    full = lambda bb, ti: (0, 0)
    return pl.pallas_call(
        functools.partial(_rwkv_kernel, tt=tt),
        grid=(b, t // tt),
        in_specs=[
            pl.BlockSpec((None, tt, N_RWKV_COLS), lambda bb, ti: (bb, ti, 0)),
            pl.BlockSpec((1, N_RWKV_COLS), full),
            pl.BlockSpec((8, RWKV_WIDTH), full),
            pl.BlockSpec((64, RWKV_WIDTH), full),
            pl.BlockSpec((64, RWKV_WIDTH), full),
            pl.BlockSpec((128, RWKV_WIDTH), full),
            pl.BlockSpec((4 * HEAD_DIM, 4 * HEAD_DIM), full),
        ],
        out_specs=pl.BlockSpec((None, tt, RWKV_WIDTH), lambda bb, ti: (bb, ti, 0)),
        out_shape=jax.ShapeDtypeStruct((b, t, RWKV_WIDTH), F32),
        scratch_shapes=[
            pltpu.VMEM((8, N_RWKV_COLS), F32),
            pltpu.VMEM((RWKV_HEADS, HEAD_DIM, HEAD_DIM), F32),
            pltpu.VMEM((tt, RWKV_WIDTH), F32),
        ],
        compiler_params=_cparams(("parallel", "arbitrary")),
        name="rwkv7",
    )(zr, mu, pv, wup, aup, gup, seg)


def _tail_kernel(x_ref, ya_ref, yb_ref, yc_ref, p_ref, g_ref, wm_ref, wi_ref, wo_ref, wg_ref, wp_ref, o_ref,
                 h_ref, *, tf):
    c0, c1 = LRU_WIDTH, LRU_WIDTH + MLSTM_WIDTH
    mix = (jnp.dot(ya_ref[...].astype(BF16), wm_ref[0:c0, :], preferred_element_type=F32)
           + jnp.dot(yb_ref[...].astype(BF16), wm_ref[c0:c1, :], preferred_element_type=F32)
           + jnp.dot(yc_ref[...].astype(BF16), wm_ref[c1:, :], preferred_element_type=F32))
    x = x_ref[...] + _rms(mix, g_ref[3:4, :])
    xn = _rms(x, g_ref[4:5, :]).astype(BF16)
    for c in range(D_FF // tf):
        gate = jnp.dot(xn, wi_ref[:, c * tf:(c + 1) * tf], preferred_element_type=F32)
        up = jnp.dot(xn, wi_ref[:, D_FF + c * tf:D_FF + (c + 1) * tf], preferred_element_type=F32)
        h_ref[:, c * tf:(c + 1) * tf] = (gate * jax.nn.sigmoid(gate) * up).astype(BF16)
    y = jnp.dot(h_ref[...], wo_ref[...], preferred_element_type=F32)
    x = x + 0.5 * _rms(y, g_ref[5:6, :])
    pgate = jax.nn.sigmoid(jnp.dot(_rms(x, g_ref[6:7, :]).astype(BF16), wg_ref[...], preferred_element_type=F32))
    pe = jnp.dot(p_ref[...].astype(BF16), wp_ref[...], preferred_element_type=F32)
    o_ref[...] = x + _rms(pgate * pe, g_ref[7:8, :])


def _tail(x, ya, yb, yc, p, g8, w_mix, w_in, w_out, w_gate, w_proj, *, tm, tf):
    n = x.shape[0]
    row = lambda i: (i, 0)
    const = lambda shape: pl.BlockSpec(shape, lambda i: (0, 0), pipeline_mode=pl.Buffered(1))
    return pl.pallas_call(
        functools.partial(_tail_kernel, tf=tf),
        grid=(n // tm,),
        in_specs=[
            pl.BlockSpec((tm, D_MODEL), row),
            pl.BlockSpec((tm, LRU_WIDTH), row),
            pl.BlockSpec((tm, MLSTM_WIDTH), row),
            pl.BlockSpec((tm, RWKV_WIDTH), row),
            pl.BlockSpec((tm, PLE_DIM), row),
            const((8, D_MODEL)),
            const((D_MODEL, D_MODEL)),
            const((D_MODEL, 2 * D_FF)),
            const((D_FF, D_MODEL)),
            const((D_MODEL, D_MODEL)),
            const((PLE_DIM, D_MODEL)),
        ],
        out_specs=pl.BlockSpec((tm, D_MODEL), row),
        out_shape=jax.ShapeDtypeStruct((n, D_MODEL), F32),
        scratch_shapes=[pltpu.VMEM((tm, D_FF), BF16)],
        compiler_params=_cparams(("parallel",)),
        name="tail",
    )(x, ya, yb, yc, p, g8, w_mix, w_in, w_out, w_gate, w_proj)


def _block_diag(w):
    nh, dd, _ = w.shape
    eye = jnp.eye(nh, dtype=w.dtype)
    return (eye[:, None, :, None] * w[:, :, None, :]).reshape(nh * dd, nh * dd)


def _tile(n, pref):
    return pref if n % pref == 0 else n


def kernel(x, p, norm_g, ffn_w_in, ffn_w_out, w_in, w_out, lru_conv_w, lru_conv_b, lru_w_a, lru_b_a, lru_w_x, lru_b_x, lru_lambda, m_b_i, m_b_f, m_norm, rw_mu, rw_w0, rw_w_up, rw_a0, rw_a_up, rw_g_up, rw_k_k, rw_k_a, rw_r_k, rw_ln_w, rw_ln_b, ple_w_proj, ple_w_gate):
    bsz, t, _ = x.shape
    n = bsz * t
    depth = norm_g.shape[0]
    tm_ffn = _tile(n, 512)
    tm = _tile(n, 512)
    tt_lru = _tile(t, 512)
    tt_m = _tile(t, 512)
    tt_r = _tile(t, 256)
    seg = _block_diag(jnp.ones((4, HEAD_DIM, HEAD_DIM), BF16))

    xf = x.reshape(n, D_MODEL)
    for l in range(depth):
        g = norm_g[l]
        wl = w_in[l]
        o_zr = 2 * LRU_WIDTH + 4 * MLSTM_WIDTH + 2 * MLSTM_HEADS
        w_gates = wl[:, o_zr - 2 * MLSTM_HEADS:o_zr]
        wcat = jnp.concatenate(
            [wl[:, o_zr:], jnp.pad(w_gates, ((0, 0), (0, ZG_COLS - 2 * MLSTM_HEADS))),
             wl[:, 0:ZL_COLS], wl[:, ZL_COLS:ZL_COLS + ZM_COLS]], axis=1).astype(BF16)
        wt = wl[:, ZL_COLS + MLSTM_WIDTH:ZL_COLS + 2 * MLSTM_WIDTH].T.astype(BF16)

        xf = _ffn(xf, g[0:2], ffn_w_in[l, 0].astype(BF16), ffn_w_out[l, 0].astype(BF16), tm=tm_ffn, tf=256)

        lru_pv = jnp.concatenate([lru_conv_w[l], lru_conv_b[l][None], lru_b_a[l][None], lru_b_x[l][None],
                                  lru_lambda[l][None]], axis=0)
        gate_b = jnp.concatenate([m_b_i[l], m_b_f[l]])
        bcol = jnp.broadcast_to(gate_b[:, None], (8, 128))
        brow = jnp.broadcast_to(jnp.pad(gate_b, (0, ZG_COLS - 8))[None, :], (8, ZG_COLS))
        nrm = jnp.broadcast_to(m_norm[l][None, :], (8, MLSTM_WIDTH))
        zr, ya, yb = _mixer(xf.reshape(bsz, t, D_MODEL), g[2:3], wcat, wt, lru_pv,
                            _block_diag(lru_w_a[l]).astype(BF16), _block_diag(lru_w_x[l]).astype(BF16),
                            bcol, brow, nrm, tt=tt_m)

        rw_pv = jnp.stack([rw_w0[l], rw_a0[l], rw_k_k[l], rw_k_a[l], rw_r_k[l].reshape(-1), rw_ln_w[l],
                           rw_ln_b[l], jnp.zeros((RWKV_WIDTH,), F32)], axis=0)
        yc = _rwkv(zr, rw_mu[l][None, :], rw_pv, rw_w_up[l].astype(BF16),
                   rw_a_up[l].astype(BF16), rw_g_up[l].astype(BF16), seg, tt=tt_r)

        xf = _tail(xf, ya.reshape(n, LRU_WIDTH), yb.reshape(n, MLSTM_WIDTH), yc.reshape(n, RWKV_WIDTH),
                   p[l].reshape(n, PLE_DIM), g, w_out[l].astype(BF16), ffn_w_in[l, 1].astype(BF16),
                   ffn_w_out[l, 1].astype(BF16), ple_w_gate[l].astype(BF16), ple_w_proj[l].astype(BF16),
                   tm=tm_ffn, tf=256)
    return xf.reshape(bsz, t, D_MODEL)
```

```python
import functools

import jax
import jax.numpy as jnp
from jax import lax
from jax.experimental import pallas as pl
from jax.experimental.pallas import tpu as pltpu

F32 = jnp.float32
BF16 = jnp.bfloat16

D_MODEL = 1024
PLE_DIM = 256
D_FF = 2816
RMS_EPS = 1e-6
LRU_WIDTH = 384
LRU_HEADS = 6
CONV_WIDTH = 4
LRU_C = 8.0
MLSTM_HEADS = 4
MLSTM_WIDTH = 256
RWKV_HEADS = 6
RWKV_WIDTH = 384
N_RWKV_COLS = 1408
GN_EPS = 64e-5
HEAD_DIM = 64
CHUNK = 64

ZR_COLS = N_RWKV_COLS
ZL_COLS = 2 * LRU_WIDTH
ZM_COLS = 4 * MLSTM_WIDTH
ZG_COLS = 128
KT_ROWS = MLSTM_WIDTH + 8

VMEM_LIMIT = 56 * 1024 * 1024


def _cparams(sem):
    return pltpu.CompilerParams(dimension_semantics=sem, vmem_limit_bytes=VMEM_LIMIT)


def _rms(x, g):
    return x * lax.rsqrt(jnp.mean(x * x, axis=-1, keepdims=True) + RMS_EPS) * g


def _softplus(x):
    return jnp.maximum(x, 0.0) + jnp.log(1.0 + jnp.exp(-jnp.abs(x)))


def _log_sigmoid(x):
    return -_softplus(-x)


def _dot(a, b):
    return jnp.dot(a.astype(BF16), b.astype(BF16), preferred_element_type=F32)


def _dot_nt(a, b):
    return lax.dot_general(a.astype(BF16), b.astype(BF16), (((1,), (1,)), ((), ())),
                           preferred_element_type=F32)


def _segsum(x, seg):
    hi = x.astype(BF16)
    lo = (x - hi.astype(F32)).astype(BF16)
    w = seg.shape[0]
    head = (jnp.dot(hi[:, 0:w], seg, preferred_element_type=F32)
            + jnp.dot(lo[:, 0:w], seg, preferred_element_type=F32))
    tail = jnp.dot(jnp.concatenate([hi[:, w:], lo[:, w:]], axis=1), seg, preferred_element_type=F32)
    return jnp.concatenate([head, tail[:, 0:w // 2] + tail[:, w // 2:]], axis=1)


def _ffn_kernel(x_ref, g_ref, wi_ref, wo_ref, o_ref, h_ref, *, tf):
    x = x_ref[...]
    xn = _rms(x, g_ref[0:1, :]).astype(BF16)
    for c in range(D_FF // tf):
        gate = jnp.dot(xn, wi_ref[:, c * tf:(c + 1) * tf], preferred_element_type=F32)
        up = jnp.dot(xn, wi_ref[:, D_FF + c * tf:D_FF + (c + 1) * tf], preferred_element_type=F32)
        h_ref[:, c * tf:(c + 1) * tf] = (gate * jax.nn.sigmoid(gate) * up).astype(BF16)
    y = jnp.dot(h_ref[...], wo_ref[...], preferred_element_type=F32)
    o_ref[...] = x + 0.5 * _rms(y, g_ref[1:2, :])


def _ffn(x, g2, w_in, w_out, *, tm, tf):
    n = x.shape[0]
    const = dict(pipeline_mode=pl.Buffered(1))
    return pl.pallas_call(
        functools.partial(_ffn_kernel, tf=tf),
        grid=(n // tm,),
        in_specs=[
            pl.BlockSpec((tm, D_MODEL), lambda i: (i, 0)),
            pl.BlockSpec((2, D_MODEL), lambda i: (0, 0)),
            pl.BlockSpec((D_MODEL, 2 * D_FF), lambda i: (0, 0), **const),
            pl.BlockSpec((D_FF, D_MODEL), lambda i: (0, 0), **const),
        ],
        out_specs=pl.BlockSpec((tm, D_MODEL), lambda i: (i, 0)),
        out_shape=jax.ShapeDtypeStruct((n, D_MODEL), F32),
        scratch_shapes=[pltpu.VMEM((tm, D_FF), BF16)],
        compiler_params=_cparams(("parallel",)),
        name="ffn",
    )(x, g2, w_in, w_out)


def _mixin_kernel(x_ref, g_ref, w_ref, zr_ref, zl_ref, zm_ref, zg_ref, kt_ref):
    xn = _rms(x_ref[...], g_ref[...]).astype(BF16)
    c0, c1, c2 = ZR_COLS + ZG_COLS, ZR_COLS + ZG_COLS + ZL_COLS, ZR_COLS + ZG_COLS + ZL_COLS + ZM_COLS
    zrg = jnp.dot(xn, w_ref[:, 0:c0], preferred_element_type=F32)
    zg = zrg[:, ZR_COLS:c0]
    zr_ref[...] = zrg[:, 0:ZR_COLS]
    zg_ref[...] = zg
    zl_ref[...] = jnp.dot(xn, w_ref[:, c0:c1], preferred_element_type=F32)
    zm = jnp.dot(xn, w_ref[:, c1:c2], preferred_element_type=F32)
    zm_ref[...] = zm
    kt_ref[0:MLSTM_WIDTH, :] = zm[:, MLSTM_WIDTH:2 * MLSTM_WIDTH].T
    kt_ref[MLSTM_WIDTH:KT_ROWS, :] = zg.T[0:KT_ROWS - MLSTM_WIDTH, :]


def _mixer_in(x, g, wcat, *, tm):
    n = x.shape[0]
    ncols = wcat.shape[1]
    row = lambda i: (i, 0)
    return pl.pallas_call(
        _mixin_kernel,
        grid=(n // tm,),
        in_specs=[
            pl.BlockSpec((tm, D_MODEL), row),
            pl.BlockSpec((1, D_MODEL), lambda i: (0, 0)),
            pl.BlockSpec((D_MODEL, ncols), lambda i: (0, 0)),
        ],
        out_specs=[
            pl.BlockSpec((tm, ZR_COLS), row),
            pl.BlockSpec((tm, ZL_COLS), row),
            pl.BlockSpec((tm, ZM_COLS), row),
            pl.BlockSpec((tm, ZG_COLS), row),
            pl.BlockSpec((KT_ROWS, tm), lambda i: (0, i)),
        ],
        out_shape=[
            jax.ShapeDtypeStruct((n, ZR_COLS), F32),
            jax.ShapeDtypeStruct((n, ZL_COLS), F32),
            jax.ShapeDtypeStruct((n, ZM_COLS), F32),
            jax.ShapeDtypeStruct((n, ZG_COLS), F32),
            jax.ShapeDtypeStruct((KT_ROWS, n), F32),
        ],
        compiler_params=_cparams(("parallel",)),
        name="mixer_in",
    )(x, g, wcat)


def _lru_kernel(z_ref, pv_ref, wa_ref, wx_ref, o_ref, xbuf_ref, h_ref, al_ref, ul_ref, c_ref, *, tt):
    t = pl.program_id(1)

    @pl.when(t == 0)
    def _():
        xbuf_ref[0:8, :] = jnp.zeros((8, LRU_WIDTH), F32)
        h_ref[...] = jnp.zeros_like(h_ref)

    x = z_ref[:, 0:LRU_WIDTH]
    gate = z_ref[:, LRU_WIDTH:2 * LRU_WIDTH]
    pv = pv_ref[...]
    xbuf_ref[8:8 + tt, :] = x
    xa = pv[4:5, :] + pv[3:4, :] * x
    for j in range(CONV_WIDTH - 1):
        xa = xa + pv[j:j + 1, :] * xbuf_ref[pl.ds(5 + j, tt), :]
    xbuf_ref[0:8, :] = x[tt - 8:tt, :]

    r = jax.nn.sigmoid(_dot(xa, wa_ref[...]) + pv[5:6, :])
    i = jax.nn.sigmoid(_dot(xa, wx_ref[...]) + pv[6:7, :])
    log_a = (-LRU_C) * r * _softplus(-pv[7:8, :])
    a = jnp.exp(log_a)
    u = jnp.sqrt(1.0 - jnp.exp(2.0 * log_a)) * (i * xa)

    ng = tt // 8
    a3 = a.reshape(ng, 8, LRU_WIDTH)
    u3 = u.reshape(ng, 8, LRU_WIDTH)
    sub = lax.broadcasted_iota(jnp.int32, (ng, 8, LRU_WIDTH), 1)
    for d in (1, 2, 4):
        keep = sub >= d
        a_sh = jnp.where(keep, pltpu.roll(a3, d, axis=1), 1.0)
        u_sh = jnp.where(keep, pltpu.roll(u3, d, axis=1), 0.0)
        u3 = a3 * u_sh + u3
        a3 = a3 * a_sh
    a_l = a3.reshape(tt, LRU_WIDTH)
    u_l = u3.reshape(tt, LRU_WIDTH)
    nslab = LRU_WIDTH // 128
    for j in range(nslab):
        al_ref[j] = a_l[:, 128 * j:128 * (j + 1)]
        ul_ref[j] = u_l[:, 128 * j:128 * (j + 1)]
    ag = jnp.concatenate([al_ref[j, pl.ds(7, ng, stride=8), :] for j in range(nslab)], axis=1)
    ug = jnp.concatenate([ul_ref[j, pl.ds(7, ng, stride=8), :] for j in range(nslab)], axis=1)
    rowg = lax.broadcasted_iota(jnp.int32, (ng, LRU_WIDTH), 0)
    d = 1
    while d < ng:
        keep = rowg >= d
        a_sh = jnp.where(keep, pltpu.roll(ag, d, axis=0), 1.0)
        u_sh = jnp.where(keep, pltpu.roll(ug, d, axis=0), 0.0)
        ug = ag * u_sh + ug
        ag = ag * a_sh
        d *= 2
    h0 = h_ref[0:1, :]
    h_end = ug + ag * h0
    h_ref[...] = jnp.broadcast_to(h_end[ng - 1:ng, :], h_ref.shape)
    c_ref[...] = jnp.where(rowg == 0, h0, pltpu.roll(h_end, 1, axis=0))
    gg = jax.nn.gelu(gate, approximate=True)
    for gi in range(ng):
        rs = slice(8 * gi, 8 * gi + 8)
        o_ref[rs, :] = (u_l[rs, :] + a_l[rs, :] * c_ref[gi:gi + 1, :]) * gg[rs, :]


def _lru(zl, pv, wa, wx, *, tt):
    b, t, _ = zl.shape
    full = lambda bb, tt_: (0, 0)
    return pl.pallas_call(
        functools.partial(_lru_kernel, tt=tt),
        grid=(b, t // tt),
        in_specs=[
            pl.BlockSpec((None, tt, ZL_COLS), lambda bb, ti: (bb, ti, 0)),
            pl.BlockSpec((8, LRU_WIDTH), full),
            pl.BlockSpec((LRU_WIDTH, LRU_WIDTH), full),
            pl.BlockSpec((LRU_WIDTH, LRU_WIDTH), full),
        ],
        out_specs=pl.BlockSpec((None, tt, LRU_WIDTH), lambda bb, ti: (bb, ti, 0)),
        out_shape=jax.ShapeDtypeStruct((b, t, LRU_WIDTH), F32),
        scratch_shapes=[pltpu.VMEM((tt + 8, LRU_WIDTH), F32), pltpu.VMEM((8, LRU_WIDTH), F32),
                        pltpu.VMEM((LRU_WIDTH // 128, tt, 128), F32), pltpu.VMEM((LRU_WIDTH // 128, tt, 128), F32),
                        pltpu.VMEM((tt // 8, LRU_WIDTH), F32)],
        compiler_params=_cparams(("parallel", "arbitrary")),
        name="rglru",
    )(zl, pv, wa, wx)


def _mlstm_kernel(q_ref, k_ref, v_ref, og_ref, gc_ref, kt_ref, bcol_ref, brow_ref, nrm_ref, o_ref,
                  cn_ref, m_ref, *, tt):
    t = pl.program_id(1)

    @pl.when(t == 0)
    def _():
        cn_ref[...] = jnp.zeros_like(cn_ref)
        m_ref[...] = jnp.zeros_like(m_ref)

    L = CHUNK
    gr = kt_ref[MLSTM_WIDTH:MLSTM_WIDTH + 8, :] + bcol_ref[:, 0:1]
    br = _log_sigmoid(gr)
    pos = lax.broadcasted_iota(jnp.int32, (8, tt), 1) & (L - 1)
    d = 1
    while d < L:
        br = br + jnp.where(pos >= d, pltpu.roll(br, d, axis=1), 0.0)
        d *= 2
    bc = _log_sigmoid(gc_ref[...] + brow_ref[0:1, :])
    posc = lax.broadcasted_iota(jnp.int32, (tt, ZG_COLS), 0) & (L - 1)
    d = 1
    while d < L:
        bc = bc + jnp.where(posc >= d, pltpu.roll(bc, d, axis=0), 0.0)
        d *= 2

    nck = tt // L
    heads = range(MLSTM_HEADS)
    units = [(c, h) for c in range(nck) for h in heads]
    rsl = [slice(c * L, (c + 1) * L) for c in range(nck)]
    hsl = [slice(HEAD_DIM * h, HEAD_DIM * (h + 1)) for h in heads]
    psl = [slice(128 * (h // 2), 128 * (h // 2) + 128) for h in heads]
    jsl = [slice(128 * (c // 2), 128 * (c // 2) + 128) for c in range(nck)]

    lane = lax.broadcasted_iota(jnp.int32, (L, 128), 1)
    rowi = lax.broadcasted_iota(jnp.int32, (L, 128), 0)
    half = [lane < HEAD_DIM, lane >= HEAD_DIM]
    den_lane = [HEAD_DIM, HEAD_DIM - 1]
    ones_col = [jnp.where(lane == den_lane[e], 1.0, 0.0) for e in range(2)]
    cmask = [(lane >= L * cc) & (lane < L * (cc + 1)) & (lane - L * cc <= rowi) for cc in range(2)]
    lane1 = lax.broadcasted_iota(jnp.int32, (1, 128), 1)
    rmask = [lane1 < L, lane1 >= L]

    b_row = {(c, h): br[4 + h:5 + h, jsl[c]] for c, h in units}
    i_row = {(c, h): gr[h:h + 1, jsl[c]] for c, h in units}

    g = {(c, h): br[4 + h:5 + h, (c + 1) * L - 1:(c + 1) * L] for c, h in units}
    w_log = {u: jnp.where(rmask[u[0] % 2], g[u] - b_row[u] + i_row[u], -jnp.inf) for u in units}
    w_max = {u: jnp.max(w_log[u], axis=1, keepdims=True) for u in units}
    m_in, m_out = {}, {}
    for h in heads:
        m_st = m_ref[h, 0:1, 0:1]
        for c in range(nck):
            m_in[c, h] = m_st
            m_st = jnp.maximum(g[c, h] + m_st, w_max[c, h])
            m_out[c, h] = m_st
        m_ref[h] = jnp.broadcast_to(m_st, (8, 128))
    dec = {u: jnp.exp(g[u] + m_in[u] - m_out[u]) for u in units}
    wk = {u: jnp.exp(w_log[u] - m_out[u]) for u in units}

    q_m = {(c, h): (jnp.where(half[h % 2], q_ref[rsl[c], psl[h]], 0.0) * (HEAD_DIM ** -0.5)).astype(BF16)
           for c, h in units}
    k2 = {(c, h): k_ref[jsl[c], psl[h]].astype(BF16) for c, h in units}
    vx = {(c, h): jnp.where(jnp.concatenate([half[h % 2]] * 2, axis=0), v_ref[jsl[c], psl[h]],
                            jnp.concatenate([ones_col[h % 2]] * 2, axis=0)).astype(BF16) for c, h in units}
    bcb = {(c, h): jnp.broadcast_to(bc[rsl[c], 4 + h:5 + h], (L, 128)) for c, h in units}
    dmat = {u: jnp.where(cmask[u[0] % 2], bcb[u] - b_row[u] + i_row[u], -jnp.inf) for u in units}
    m_loc = {u: jnp.max(dmat[u], axis=1, keepdims=True) for u in units}
    p_in = {u: jnp.exp(dmat[u] - m_loc[u]) for u in units}
    qk = {u: _dot_nt(q_m[u], k2[u]) for u in units}
    nd = {u: _dot(p_in[u] * qk[u], vx[u]) for u in units}
    kv = {(c, h): _dot(kt_ref[hsl[h], jsl[c]] * wk[c, h], vx[c, h]) for c, h in units}

    cn_in = {}
    for h in heads:
        cn = cn_ref[h]
        for c in range(nck):
            cn_in[c, h] = cn
            cn = dec[c, h] * cn + kv[c, h]
        cn_ref[h] = cn
    cn_pair = {(c, p): jnp.concatenate([cn_in[c, 2 * p], cn_in[c, 2 * p + 1]], axis=0).astype(BF16)
               for c in range(nck) for p in range(MLSTM_HEADS // 2)}

    inter = {u: bcb[u] + m_in[u] for u in units}
    mj = {u: jnp.maximum(m_loc[u], inter[u]) for u in units}
    e_loc = {u: jnp.exp(m_loc[u] - mj[u]) for u in units}
    e_int = {u: jnp.exp(inter[u] - mj[u]) for u in units}
    e_neg = {u: jnp.exp(-mj[u]) for u in units}
    qc = {(c, h): _dot(q_m[c, h], cn_pair[c, h // 2]) for c, h in units}
    numden = {u: e_loc[u] * nd[u] + e_int[u] * qc[u] for u in units}
    den = {(c, h): numden[c, h][:, den_lane[h % 2]:den_lane[h % 2] + 1] for c, h in units}
    hv = {u: numden[u] / jnp.maximum(jnp.abs(den[u]), e_neg[u]) for u in units}
    ms = {(c, h): jnp.sum(jnp.where(half[h % 2], hv[c, h] * hv[c, h], 0.0), axis=1, keepdims=True)
          * (1.0 / HEAD_DIM) for c, h in units}
    hn = {u: hv[u] * lax.rsqrt(ms[u] + RMS_EPS) for u in units}
    for c in range(nck):
        for p in range(MLSTM_HEADS // 2):
            ps = psl[2 * p]
            hb = jnp.where(half[0], hn[c, 2 * p], hn[c, 2 * p + 1])
            o_ref[rsl[c], ps] = hb * nrm_ref[0:1, ps] * jax.nn.sigmoid(og_ref[rsl[c], ps])


def _mlstm(zm, zg, ktg, bcol, brow, nrm, *, tt):
    b, t, _ = zm.shape
    nt = t // tt
    colblk = lambda j: pl.BlockSpec((None, tt, MLSTM_WIDTH), lambda bb, ti, j=j: (bb, ti, j))
    full = lambda bb, ti: (0, 0)
    return pl.pallas_call(
        functools.partial(_mlstm_kernel, tt=tt),
        grid=(b, nt),
        in_specs=[
            colblk(0), colblk(1), colblk(2), colblk(3),
            pl.BlockSpec((None, tt, ZG_COLS), lambda bb, ti: (bb, ti, 0)),
            pl.BlockSpec((KT_ROWS, tt), lambda bb, ti: (0, bb * nt + ti)),
            pl.BlockSpec((8, 128), full),
            pl.BlockSpec((8, 128), full),
            pl.BlockSpec((8, MLSTM_WIDTH), full),
        ],
        out_specs=pl.BlockSpec((None, tt, MLSTM_WIDTH), lambda bb, ti: (bb, ti, 0)),
        out_shape=jax.ShapeDtypeStruct((b, t, MLSTM_WIDTH), F32),
        scratch_shapes=[pltpu.VMEM((MLSTM_HEADS, HEAD_DIM, 128), F32), pltpu.VMEM((MLSTM_HEADS, 8, 128), F32)],
        compiler_params=_cparams(("parallel", "arbitrary")),
        name="mlstm",
    )(zm, zm, zm, zm, zg, ktg, bcol, brow, nrm)


def _rwkv_kernel(z_ref, mu_ref, pv_ref, wup_ref, aup_ref, gup_ref, seg_ref, o_ref,
                 prev_ref, st_ref, y_ref, *, tt):
    t = pl.program_id(1)

    @pl.when(t == 0)
    def _():
        prev_ref[...] = jnp.zeros_like(prev_ref)
        st_ref[...] = jnp.zeros_like(st_ref)

    L = CHUNK
    W = RWKV_WIDTH
    nck = tt // L
    z = z_ref[...]
    row1 = lax.broadcasted_iota(jnp.int32, (tt, 1), 0)
    zprev = jnp.where(row1 == 0, prev_ref[0:1, :], pltpu.roll(z, 1, axis=0))
    prev_ref[...] = jnp.broadcast_to(z[tt - 1:tt, :], prev_ref.shape)
    zs = z + (zprev - z) * mu_ref[...]
    r = zs[:, 0:W]
    k = zs[:, W:2 * W]
    v = zs[:, 2 * W:3 * W]
    wd = zs[:, 3 * W:3 * W + 64]
    ad = zs[:, 3 * W + 64:3 * W + 128]
    gd = zs[:, 3 * W + 128:3 * W + 256]
    pv = pv_ref[...]
    w0, a0, k_k, k_a, r_k, ln_w, ln_b = (pv[j:j + 1, :] for j in range(7))
    seg = seg_ref[...]

    log_w = -_softplus(-(w0 + _dot(jnp.tanh(wd), wup_ref[...]))) - 0.5
    ld = -jnp.exp(log_w)
    iclr = jax.nn.sigmoid(a0 + _dot(ad, aup_ref[...]))
    g = _dot(jax.nn.sigmoid(gd), gup_ref[...])
    kk = k * k_k
    kk = kk / jnp.maximum(jnp.sqrt(_segsum(kk * kk, seg)), 1e-12)
    k_mod = k * (1.0 + (iclr - 1.0) * k_a)
    a_vec = -kk
    b_vec = kk * iclr

    cum = ld
    posr = lax.broadcasted_iota(jnp.int32, (tt, W), 0) & (L - 1)
    d = 1
    while d < L:
        cum = cum + jnp.where(posr >= d, pltpu.roll(cum, d, axis=0), 0.0)
        d *= 2
    cum_last = jnp.concatenate(
        [jnp.broadcast_to(cum[(c + 1) * L - 1:(c + 1) * L, :], (L, W)) for c in range(nck)], axis=0)
    e_neg = jnp.exp(-cum)
    e_tail = jnp.exp(cum_last - cum)
    a_t = jnp.exp(cum - ld) * a_vec
    r_t = jnp.exp(cum) * r
    b_t = e_neg * b_vec
    k_t = e_neg * k_mod
    b_h = e_tail * b_vec
    k_h = e_tail * k_mod

    UR = 2 * L
    ri = lax.broadcasted_iota(jnp.int32, (UR, UR), 0)
    ci = lax.broadcasted_iota(jnp.int32, (UR, UR), 1)
    same = (ri // L) == (ci // L)
    strict = jnp.where(same & (ri > ci), 1.0, 0.0)
    incl = jnp.where(same & (ri >= ci), 1.0, 0.0)
    eye = jnp.where(ri == ci, 1.0, 0.0)

    heads = range(RWKV_HEADS)
    hsl = [slice(HEAD_DIM * h, HEAD_DIM * (h + 1)) for h in heads]
    bh_t, kh_t, cum_t = [], [], []
    for pair in range(RWKV_HEADS // 2):
        ps = slice(128 * pair, 128 * pair + 128)
        bp, kp, cp = b_h[:, ps].T.astype(BF16), k_h[:, ps].T.astype(BF16), cum[:, ps].T
        for e in range(2):
            es = slice(HEAD_DIM * e, HEAD_DIM * (e + 1))
            bh_t.append(bp[es])
            kh_t.append(kp[es])
            cum_t.append(cp[es])

    units = [(h, u) for u in range(tt // UR) for h in heads]
    nu = range(len(units))
    usl = [slice(UR * u, UR * (u + 1)) for _, u in units]
    at = [a_t[usl[i], hsl[h]].astype(BF16) for i, (h, _) in enumerate(units)]
    rt = [r_t[usl[i], hsl[h]] for i, (h, _) in enumerate(units)]
    bt = [b_t[usl[i], hsl[h]].astype(BF16) for i, (h, _) in enumerate(units)]
    kt = [k_t[usl[i], hsl[h]].astype(BF16) for i, (h, _) in enumerate(units)]
    vu = [v[usl[i], hsl[h]].astype(BF16) for i, (h, _) in enumerate(units)]
    mask4 = jnp.concatenate([jnp.concatenate([strict, strict], axis=1),
                             jnp.concatenate([incl, incl], axis=1)], axis=0)
    prod = [mask4 * _dot_nt(jnp.concatenate([at[i], rt[i].astype(BF16)], axis=0),
                            jnp.concatenate([bt[i], kt[i]], axis=0)) for i in nu]
    a_ab = [prod[i][0:UR, 0:UR] for i in nu]
    a_ak = [prod[i][0:UR, UR:2 * UR].astype(BF16) for i in nu]
    a_rbk = [prod[i][UR:2 * UR, :].astype(BF16) for i in nu]
    tm = [eye + a_ab[i] for i in nu]
    pw = [a_ab[i].astype(BF16) for i in nu]
    pw = [_dot(pw[i], pw[i]).astype(BF16) for i in nu]
    for _ in range(4):
        res = [_dot(pw[i], jnp.concatenate([pw[i], tm[i].astype(BF16)], axis=1)) for i in nu]
        pw = [res[i][:, 0:UR].astype(BF16) for i in nu]
        tm = [tm[i] + res[i][:, UR:2 * UR] for i in nu]
    tm = [tm[i] + _dot(pw[i], tm[i]) for i in nu]
    akv = [_dot(a_ak[i], vu[i]) for i in nu]
    tmb = [tm[i].astype(BF16) for i in nu]
    w_all = [_dot(tmb[i], at[i]).astype(BF16) for i in nu]
    u0_all = [_dot(tmb[i], akv[i]).astype(BF16) for i in nu]
    q_all = [(rt[i] + _dot(a_rbk[i][:, 0:UR], w_all[i])).astype(BF16) for i in nu]
    y0_all = [_dot(a_rbk[i], jnp.concatenate([u0_all[i], vu[i]], axis=0)) for i in nu]
    gm, hm, pcol, qc, y0c = {}, {}, {}, {}, {}
    for i, (h, u) in enumerate(units):
        for cc in range(UR // L):
            c = u * (UR // L) + cc
            ls = slice(cc * L, (cc + 1) * L)
            rs = slice(c * L, (c + 1) * L)
            gm[c, h] = _dot(bh_t[h][:, rs], w_all[i][ls]).astype(BF16)
            hm[c, h] = _dot(bh_t[h][:, rs], u0_all[i][ls]) + _dot(kh_t[h][:, rs], vu[i][ls])
            pcol[c, h] = jnp.exp(cum_t[h][:, (c + 1) * L - 1:(c + 1) * L])
            qc[c, h] = jnp.concatenate([q_all[i][ls], gm[c, h]], axis=0)
            y0c[c, h] = y0_all[i][ls]
    zst = [st_ref[h] for h in heads]
    for c in range(nck):
        rs = slice(c * L, (c + 1) * L)
        for h in heads:
            qz = _dot(qc[c, h], zst[h])
            y_ref[rs, hsl[h]] = qz[0:L] + y0c[c, h]
            zst[h] = pcol[c, h] * zst[h] + qz[L:2 * L] + hm[c, h]
    for h in heads:
        st_ref[h] = zst[h]

    y = y_ref[...]
    mean = _segsum(y, seg) * (1.0 / HEAD_DIM)
    yc = y - mean
    var = _segsum(yc * yc, seg) * (1.0 / HEAD_DIM)
    yn = yc * lax.rsqrt(var + GN_EPS) * ln_w + ln_b
    bonus = _segsum(r * k_mod * r_k, seg) * v
    o_ref[...] = (yn + bonus) * g


def _rwkv(zr, mu, pv, wup, aup, gup, seg, *, tt):
    b, t, _ = zr.shape
    full = lambda bb, ti: (0, 0)
    return pl.pallas_call(
        functools.partial(_rwkv_kernel, tt=tt),
        grid=(b, t // tt),
        in_specs=[
            pl.BlockSpec((None, tt, N_RWKV_COLS), lambda bb, ti: (bb, ti, 0)),
            pl.BlockSpec((1, N_RWKV_COLS), full),
            pl.BlockSpec((8, RWKV_WIDTH), full),
            pl.BlockSpec((64, RWKV_WIDTH), full),
            pl.BlockSpec((64, RWKV_WIDTH), full),
            pl.BlockSpec((128, RWKV_WIDTH), full),
            pl.BlockSpec((4 * HEAD_DIM, 4 * HEAD_DIM), full),
        ],
        out_specs=pl.BlockSpec((None, tt, RWKV_WIDTH), lambda bb, ti: (bb, ti, 0)),
        out_shape=jax.ShapeDtypeStruct((b, t, RWKV_WIDTH), F32),
        scratch_shapes=[
            pltpu.VMEM((8, N_RWKV_COLS), F32),
            pltpu.VMEM((RWKV_HEADS, HEAD_DIM, HEAD_DIM), F32),
            pltpu.VMEM((tt, RWKV_WIDTH), F32),
        ],
        compiler_params=_cparams(("parallel", "arbitrary")),
        name="rwkv7",
    )(zr, mu, pv, wup, aup, gup, seg)


def _tail_kernel(x_ref, ya_ref, yb_ref, yc_ref, p_ref, g_ref, wm_ref, wi_ref, wo_ref, wg_ref, wp_ref, o_ref,
                 h_ref, *, tf):
    c0, c1 = LRU_WIDTH, LRU_WIDTH + MLSTM_WIDTH
    mix = (jnp.dot(ya_ref[...].astype(BF16), wm_ref[0:c0, :], preferred_element_type=F32)
           + jnp.dot(yb_ref[...].astype(BF16), wm_ref[c0:c1, :], preferred_element_type=F32)
           + jnp.dot(yc_ref[...].astype(BF16), wm_ref[c1:, :], preferred_element_type=F32))
    x = x_ref[...] + _rms(mix, g_ref[3:4, :])
    xn = _rms(x, g_ref[4:5, :]).astype(BF16)
    for c in range(D_FF // tf):
        gate = jnp.dot(xn, wi_ref[:, c * tf:(c + 1) * tf], preferred_element_type=F32)
        up = jnp.dot(xn, wi_ref[:, D_FF + c * tf:D_FF + (c + 1) * tf], preferred_element_type=F32)
        h_ref[:, c * tf:(c + 1) * tf] = (gate * jax.nn.sigmoid(gate) * up).astype(BF16)
    y = jnp.dot(h_ref[...], wo_ref[...], preferred_element_type=F32)
    x = x + 0.5 * _rms(y, g_ref[5:6, :])
    pgate = jax.nn.sigmoid(jnp.dot(_rms(x, g_ref[6:7, :]).astype(BF16), wg_ref[...], preferred_element_type=F32))
    pe = jnp.dot(p_ref[...].astype(BF16), wp_ref[...], preferred_element_type=F32)
    o_ref[...] = x + _rms(pgate * pe, g_ref[7:8, :])


def _tail(x, ya, yb, yc, p, g8, w_mix, w_in, w_out, w_gate, w_proj, *, tm, tf):
    n = x.shape[0]
    row = lambda i: (i, 0)
    const = lambda shape: pl.BlockSpec(shape, lambda i: (0, 0), pipeline_mode=pl.Buffered(1))
    return pl.pallas_call(
        functools.partial(_tail_kernel, tf=tf),
        grid=(n // tm,),
        in_specs=[
            pl.BlockSpec((tm, D_MODEL), row),
            pl.BlockSpec((tm, LRU_WIDTH), row),
            pl.BlockSpec((tm, MLSTM_WIDTH), row),
            pl.BlockSpec((tm, RWKV_WIDTH), row),
            pl.BlockSpec((tm, PLE_DIM), row),
            const((8, D_MODEL)),
            const((D_MODEL, D_MODEL)),
            const((D_MODEL, 2 * D_FF)),
            const((D_FF, D_MODEL)),
            const((D_MODEL, D_MODEL)),
            const((PLE_DIM, D_MODEL)),
        ],
        out_specs=pl.BlockSpec((tm, D_MODEL), row),
        out_shape=jax.ShapeDtypeStruct((n, D_MODEL), F32),
        scratch_shapes=[pltpu.VMEM((tm, D_FF), BF16)],
        compiler_params=_cparams(("parallel",)),
        name="tail",
    )(x, ya, yb, yc, p, g8, w_mix, w_in, w_out, w_gate, w_proj)


def _block_diag(w):
    nh, dd, _ = w.shape
    eye = jnp.eye(nh, dtype=w.dtype)
    return (eye[:, None, :, None] * w[:, :, None, :]).reshape(nh * dd, nh * dd)


def _tile(n, pref):
    return pref if n % pref == 0 else n


def kernel(x, p, norm_g, ffn_w_in, ffn_w_out, w_in, w_out, lru_conv_w, lru_conv_b, lru_w_a, lru_b_a, lru_w_x, lru_b_x, lru_lambda, m_b_i, m_b_f, m_norm, rw_mu, rw_w0, rw_w_up, rw_a0, rw_a_up, rw_g_up, rw_k_k, rw_k_a, rw_r_k, rw_ln_w, rw_ln_b, ple_w_proj, ple_w_gate):
    bsz, t, _ = x.shape
    n = bsz * t
    depth = norm_g.shape[0]
    tm_ffn = _tile(n, 512)
    tm = _tile(n, 512)
    tt_lru = _tile(t, 512)
    tt_m = _tile(t, 512)
    tt_r = _tile(t, 512)
    seg = _block_diag(jnp.ones((4, HEAD_DIM, HEAD_DIM), BF16))

    xf = x.reshape(n, D_MODEL)
    for l in range(depth):
        g = norm_g[l]
        wl = w_in[l]
        o_zr = 2 * LRU_WIDTH + 4 * MLSTM_WIDTH + 2 * MLSTM_HEADS
        w_gates = wl[:, o_zr - 2 * MLSTM_HEADS:o_zr]
        wcat = jnp.concatenate(
            [wl[:, o_zr:], jnp.pad(w_gates, ((0, 0), (0, ZG_COLS - 2 * MLSTM_HEADS))),
             wl[:, 0:ZL_COLS], wl[:, ZL_COLS:ZL_COLS + ZM_COLS]], axis=1).astype(BF16)

        xf = _ffn(xf, g[0:2], ffn_w_in[l, 0].astype(BF16), ffn_w_out[l, 0].astype(BF16), tm=tm_ffn, tf=256)

        zr, zl, zm, zg, ktg = _mixer_in(xf, g[2:3], wcat, tm=tm)

        lru_pv = jnp.concatenate([lru_conv_w[l], lru_conv_b[l][None], lru_b_a[l][None], lru_b_x[l][None],
                                  lru_lambda[l][None]], axis=0)
        ya = _lru(zl.reshape(bsz, t, ZL_COLS), lru_pv, _block_diag(lru_w_a[l]).astype(BF16),
                  _block_diag(lru_w_x[l]).astype(BF16), tt=tt_lru)

        gate_b = jnp.concatenate([m_b_i[l], m_b_f[l]])
        bcol = jnp.broadcast_to(gate_b[:, None], (8, 128))
        brow = jnp.broadcast_to(jnp.pad(gate_b, (0, ZG_COLS - 8))[None, :], (8, ZG_COLS))
        nrm = jnp.broadcast_to(m_norm[l][None, :], (8, MLSTM_WIDTH))
        yb = _mlstm(zm.reshape(bsz, t, ZM_COLS), zg.reshape(bsz, t, ZG_COLS), ktg, bcol, brow, nrm, tt=tt_m)

        rw_pv = jnp.stack([rw_w0[l], rw_a0[l], rw_k_k[l], rw_k_a[l], rw_r_k[l].reshape(-1), rw_ln_w[l],
                           rw_ln_b[l], jnp.zeros((RWKV_WIDTH,), F32)], axis=0)
        yc = _rwkv(zr.reshape(bsz, t, ZR_COLS), rw_mu[l][None, :], rw_pv, rw_w_up[l].astype(BF16),
                   rw_a_up[l].astype(BF16), rw_g_up[l].astype(BF16), seg, tt=tt_r)

        xf = _tail(xf, ya.reshape(n, LRU_WIDTH), yb.reshape(n, MLSTM_WIDTH), yc.reshape(n, RWKV_WIDTH),
                   p[l].reshape(n, PLE_DIM), g, w_out[l].astype(BF16), ffn_w_in[l, 1].astype(BF16),
                   ffn_w_out[l, 1].astype(BF16), ple_w_gate[l].astype(BF16), ple_w_proj[l].astype(BF16),
                   tm=tm_ffn, tf=256)
    return xf.reshape(bsz, t, D_MODEL)
```

```python
import functools

import jax
import jax.numpy as jnp
from jax import lax
from jax.experimental import pallas as pl
from jax.experimental.pallas import tpu as pltpu

F32 = jnp.float32
BF16 = jnp.bfloat16

D_MODEL = 1024
PLE_DIM = 256
D_FF = 2816
RMS_EPS = 1e-6
LRU_WIDTH = 384
LRU_HEADS = 6
CONV_WIDTH = 4
LRU_C = 8.0
MLSTM_HEADS = 4
MLSTM_WIDTH = 256
RWKV_HEADS = 6
RWKV_WIDTH = 384
N_RWKV_COLS = 1408
GN_EPS = 64e-5
HEAD_DIM = 64
CHUNK = 64

ZR_COLS = N_RWKV_COLS
ZL_COLS = 2 * LRU_WIDTH
ZM_COLS = 4 * MLSTM_WIDTH
ZG_COLS = 128
KT_ROWS = MLSTM_WIDTH + 8

VMEM_LIMIT = 56 * 1024 * 1024


def _cparams(sem):
    return pltpu.CompilerParams(dimension_semantics=sem, vmem_limit_bytes=VMEM_LIMIT)


def _rms(x, g):
    return x * lax.rsqrt(jnp.mean(x * x, axis=-1, keepdims=True) + RMS_EPS) * g


def _softplus(x):
    return jnp.maximum(x, 0.0) + jnp.log(1.0 + jnp.exp(-jnp.abs(x)))


def _log_sigmoid(x):
    return -_softplus(-x)


def _dot(a, b):
    return jnp.dot(a.astype(BF16), b.astype(BF16), preferred_element_type=F32)


def _dot_nt(a, b):
    return lax.dot_general(a.astype(BF16), b.astype(BF16), (((1,), (1,)), ((), ())),
                           preferred_element_type=F32)


def _perm_rows(m, tile):
    ng = tile // 8
    s, g0 = divmod(8 * m, ng)
    return pl.ds(8 * g0 + s, 8, stride=8)


def _segsum(x, seg):
    hi = x.astype(BF16)
    lo = (x - hi.astype(F32)).astype(BF16)
    w = seg.shape[0]
    head = (jnp.dot(hi[:, 0:w], seg, preferred_element_type=F32)
            + jnp.dot(lo[:, 0:w], seg, preferred_element_type=F32))
    tail = jnp.dot(jnp.concatenate([hi[:, w:], lo[:, w:]], axis=1), seg, preferred_element_type=F32)
    return jnp.concatenate([head, tail[:, 0:w // 2] + tail[:, w // 2:]], axis=1)


def _ffn_kernel(x_ref, g_ref, wi_ref, wo_ref, o_ref, h_ref, *, tf):
    x = x_ref[...]
    xn = _rms(x, g_ref[0:1, :]).astype(BF16)
    for c in range(D_FF // tf):
        gate = jnp.dot(xn, wi_ref[:, c * tf:(c + 1) * tf], preferred_element_type=F32)
        up = jnp.dot(xn, wi_ref[:, D_FF + c * tf:D_FF + (c + 1) * tf], preferred_element_type=F32)
        h_ref[:, c * tf:(c + 1) * tf] = (gate * jax.nn.sigmoid(gate) * up).astype(BF16)
    y = jnp.dot(h_ref[...], wo_ref[...], preferred_element_type=F32)
    o_ref[...] = x + 0.5 * _rms(y, g_ref[1:2, :])


def _ffn(x, g2, w_in, w_out, *, tm, tf):
    n = x.shape[0]
    const = dict(pipeline_mode=pl.Buffered(1))
    return pl.pallas_call(
        functools.partial(_ffn_kernel, tf=tf),
        grid=(n // tm,),
        in_specs=[
            pl.BlockSpec((tm, D_MODEL), lambda i: (i, 0)),
            pl.BlockSpec((2, D_MODEL), lambda i: (0, 0)),
            pl.BlockSpec((D_MODEL, 2 * D_FF), lambda i: (0, 0), **const),
            pl.BlockSpec((D_FF, D_MODEL), lambda i: (0, 0), **const),
        ],
        out_specs=pl.BlockSpec((tm, D_MODEL), lambda i: (i, 0)),
        out_shape=jax.ShapeDtypeStruct((n, D_MODEL), F32),
        scratch_shapes=[pltpu.VMEM((tm, D_FF), BF16)],
        compiler_params=_cparams(("parallel",)),
        name="ffn",
    )(x, g2, w_in, w_out)


def _mixin_kernel(x_ref, g_ref, w_ref, zr_ref, zl_ref, zm_ref, zg_ref, kt_ref):
    xn = _rms(x_ref[...], g_ref[...]).astype(BF16)
    c0, c1, c2 = ZR_COLS + ZG_COLS, ZR_COLS + ZG_COLS + ZL_COLS, ZR_COLS + ZG_COLS + ZL_COLS + ZM_COLS
    zrg = jnp.dot(xn, w_ref[:, 0:c0], preferred_element_type=F32)
    zg = zrg[:, ZR_COLS:c0]
    zr_ref[...] = zrg[:, 0:ZR_COLS]
    zg_ref[...] = zg
    zl = jnp.dot(xn, w_ref[:, c0:c1], preferred_element_type=F32)
    for m in range(zl.shape[0] // 8):
        rows = _perm_rows(m, zl.shape[0])
        for j in range(ZL_COLS // 128):
            zl_ref[j, rows, :] = zl[8 * m:8 * m + 8, 128 * j:128 * (j + 1)]
    zm = jnp.dot(xn, w_ref[:, c1:c2], preferred_element_type=F32)
    zm_ref[...] = zm
    kt_ref[0:MLSTM_WIDTH, :] = zm[:, MLSTM_WIDTH:2 * MLSTM_WIDTH].T
    kt_ref[MLSTM_WIDTH:KT_ROWS, :] = zg.T[0:KT_ROWS - MLSTM_WIDTH, :]


def _mixer_in(x, g, wcat, *, tm):
    n = x.shape[0]
    ncols = wcat.shape[1]
    row = lambda i: (i, 0)
    return pl.pallas_call(
        _mixin_kernel,
        grid=(n // tm,),
        in_specs=[
            pl.BlockSpec((tm, D_MODEL), row),
            pl.BlockSpec((1, D_MODEL), lambda i: (0, 0)),
            pl.BlockSpec((D_MODEL, ncols), lambda i: (0, 0)),
        ],
        out_specs=[
            pl.BlockSpec((tm, ZR_COLS), row),
            pl.BlockSpec((ZL_COLS // 128, tm, 128), lambda i: (0, i, 0)),
            pl.BlockSpec((tm, ZM_COLS), row),
            pl.BlockSpec((tm, ZG_COLS), row),
            pl.BlockSpec((KT_ROWS, tm), lambda i: (0, i)),
        ],
        out_shape=[
            jax.ShapeDtypeStruct((n, ZR_COLS), F32),
            jax.ShapeDtypeStruct((ZL_COLS // 128, n, 128), F32),
            jax.ShapeDtypeStruct((n, ZM_COLS), F32),
            jax.ShapeDtypeStruct((n, ZG_COLS), F32),
            jax.ShapeDtypeStruct((KT_ROWS, n), F32),
        ],
        compiler_params=_cparams(("parallel",)),
        name="mixer_in",
    )(x, g, wcat)


def _lru_kernel(z_ref, pv_ref, wa_ref, wx_ref, o_ref, xtail_ref, h_ref, *, tt):
    t = pl.program_id(1)

    @pl.when(t == 0)
    def _():
        xtail_ref[...] = jnp.zeros_like(xtail_ref)
        h_ref[...] = jnp.zeros_like(h_ref)

    ng = tt // 8
    nslab = LRU_WIDTH // 128
    taps = CONV_WIDTH - 1
    sub = lax.broadcasted_iota(jnp.int32, (8, LRU_WIDTH), 0)

    def load(first_slab, g):
        return jnp.concatenate([z_ref[first_slab + j, 8 * g:8 * g + 8, :] for j in range(nslab)], axis=1)

    x = [load(0, g) for g in range(ng)]
    gate = jnp.concatenate([load(nslab, g) for g in range(ng)], axis=0)
    pv = pv_ref[...]

    def before(g):
        if g >= 0:
            return x[g]
        cur = pltpu.roll(x[ng + g], 1, axis=0)
        prev = pltpu.roll(xtail_ref[taps + g], 1, axis=0)
        return jnp.where(sub == 0, prev, cur)

    xa = []
    for g in range(ng):
        acc = pv[taps + 1:taps + 2, :] + pv[taps:taps + 1, :] * x[g]
        for d in range(1, taps + 1):
            acc = acc + pv[taps - d:taps - d + 1, :] * before(g - d)
        xa.append(acc)
    for d in range(taps):
        xtail_ref[d] = x[ng - taps + d]
    xa = jnp.concatenate(xa, axis=0)

    r = jax.nn.sigmoid(_dot(xa, wa_ref[...]) + pv[5:6, :])
    i = jax.nn.sigmoid(_dot(xa, wx_ref[...]) + pv[6:7, :])
    log_a = (-LRU_C) * r * _softplus(-pv[7:8, :])
    a = jnp.exp(log_a)
    u = jnp.sqrt(1.0 - jnp.exp(2.0 * log_a)) * (i * xa)
    gg = jax.nn.gelu(gate, approximate=True)

    h_loc, a_cum = [], []
    h = u[0:8, :]
    ac = a[0:8, :]
    h_loc.append(h)
    a_cum.append(ac)
    for g in range(1, ng):
        a_g = a[8 * g:8 * g + 8, :]
        h = a_g * h + u[8 * g:8 * g + 8, :]
        ac = a_g * ac
        h_loc.append(h)
        a_cum.append(ac)
    ae, he = ac, h
    for d in (1, 2, 4):
        keep = sub >= d
        a_sh = jnp.where(keep, pltpu.roll(ae, d, axis=0), 1.0)
        h_sh = jnp.where(keep, pltpu.roll(he, d, axis=0), 0.0)
        he = ae * h_sh + he
        ae = ae * a_sh
    h0 = h_ref[0:1, :]
    h_end = he + ae * h0
    h_ref[...] = jnp.broadcast_to(h_end[7:8, :], h_ref.shape)
    c_in = jnp.where(sub == 0, h0, pltpu.roll(h_end, 1, axis=0))
    for g in range(ng):
        out = (h_loc[g] + a_cum[g] * c_in) * gg[8 * g:8 * g + 8, :]
        for j in range(nslab):
            o_ref[j, 8 * g:8 * g + 8, :] = out[:, 128 * j:128 * (j + 1)]


def _lru(zl, pv, wa, wx, *, tt):
    _, b, t, _ = zl.shape
    nslab = LRU_WIDTH // 128
    full = lambda bb, ti: (0, 0)
    return pl.pallas_call(
        functools.partial(_lru_kernel, tt=tt),
        grid=(b, t // tt),
        in_specs=[
            pl.BlockSpec((2 * nslab, None, tt, 128), lambda bb, ti: (0, bb, ti, 0)),
            pl.BlockSpec((8, LRU_WIDTH), full),
            pl.BlockSpec((LRU_WIDTH, LRU_WIDTH), full),
            pl.BlockSpec((LRU_WIDTH, LRU_WIDTH), full),
        ],
        out_specs=pl.BlockSpec((nslab, None, tt, 128), lambda bb, ti: (0, bb, ti, 0)),
        out_shape=jax.ShapeDtypeStruct((nslab, b, t, 128), F32),
        scratch_shapes=[pltpu.VMEM((CONV_WIDTH - 1, 8, LRU_WIDTH), F32), pltpu.VMEM((8, LRU_WIDTH), F32)],
        compiler_params=_cparams(("parallel", "arbitrary")),
        name="rglru",
    )(zl, pv, wa, wx)


def _mlstm_kernel(q_ref, k_ref, v_ref, og_ref, gc_ref, kt_ref, bcol_ref, brow_ref, nrm_ref, o_ref,
                  cn_ref, m_ref, *, tt):
    t = pl.program_id(1)

    @pl.when(t == 0)
    def _():
        cn_ref[...] = jnp.zeros_like(cn_ref)
        m_ref[...] = jnp.zeros_like(m_ref)

    L = CHUNK
    gr = kt_ref[MLSTM_WIDTH:MLSTM_WIDTH + 8, :] + bcol_ref[:, 0:1]
    br = _log_sigmoid(gr)
    pos = lax.broadcasted_iota(jnp.int32, (8, tt), 1) & (L - 1)
    d = 1
    while d < L:
        br = br + jnp.where(pos >= d, pltpu.roll(br, d, axis=1), 0.0)
        d *= 2
    bc = _log_sigmoid(gc_ref[...] + brow_ref[0:1, :])
    posc = lax.broadcasted_iota(jnp.int32, (tt, ZG_COLS), 0) & (L - 1)
    d = 1
    while d < L:
        bc = bc + jnp.where(posc >= d, pltpu.roll(bc, d, axis=0), 0.0)
        d *= 2

    nck = tt // L
    heads = range(MLSTM_HEADS)
    units = [(c, h) for c in range(nck) for h in heads]
    rsl = [slice(c * L, (c + 1) * L) for c in range(nck)]
    hsl = [slice(HEAD_DIM * h, HEAD_DIM * (h + 1)) for h in heads]
    psl = [slice(128 * (h // 2), 128 * (h // 2) + 128) for h in heads]
    jsl = [slice(128 * (c // 2), 128 * (c // 2) + 128) for c in range(nck)]

    lane = lax.broadcasted_iota(jnp.int32, (L, 128), 1)
    rowi = lax.broadcasted_iota(jnp.int32, (L, 128), 0)
    half = [lane < HEAD_DIM, lane >= HEAD_DIM]
    den_lane = [HEAD_DIM, HEAD_DIM - 1]
    ones_col = [jnp.where(lane == den_lane[e], 1.0, 0.0) for e in range(2)]
    cmask = [(lane >= L * cc) & (lane < L * (cc + 1)) & (lane - L * cc <= rowi) for cc in range(2)]
    lane1 = lax.broadcasted_iota(jnp.int32, (1, 128), 1)
    rmask = [lane1 < L, lane1 >= L]

    b_row = {(c, h): br[4 + h:5 + h, jsl[c]] for c, h in units}
    i_row = {(c, h): gr[h:h + 1, jsl[c]] for c, h in units}

    g = {(c, h): br[4 + h:5 + h, (c + 1) * L - 1:(c + 1) * L] for c, h in units}
    w_log = {u: jnp.where(rmask[u[0] % 2], g[u] - b_row[u] + i_row[u], -jnp.inf) for u in units}
    w_max = {u: jnp.max(w_log[u], axis=1, keepdims=True) for u in units}
    m_in, m_out = {}, {}
    for h in heads:
        m_st = m_ref[h, 0:1, 0:1]
        for c in range(nck):
            m_in[c, h] = m_st
            m_st = jnp.maximum(g[c, h] + m_st, w_max[c, h])
            m_out[c, h] = m_st
        m_ref[h] = jnp.broadcast_to(m_st, (8, 128))
    dec = {u: jnp.exp(g[u] + m_in[u] - m_out[u]) for u in units}
    wk = {u: jnp.exp(w_log[u] - m_out[u]) for u in units}

    q_m = {(c, h): (jnp.where(half[h % 2], q_ref[rsl[c], psl[h]], 0.0) * (HEAD_DIM ** -0.5)).astype(BF16)
           for c, h in units}
    k2 = {(c, h): k_ref[jsl[c], psl[h]].astype(BF16) for c, h in units}
    vx = {(c, h): jnp.where(jnp.concatenate([half[h % 2]] * 2, axis=0), v_ref[jsl[c], psl[h]],
                            jnp.concatenate([ones_col[h % 2]] * 2, axis=0)).astype(BF16) for c, h in units}
    bcb = {(c, h): jnp.broadcast_to(bc[rsl[c], 4 + h:5 + h], (L, 128)) for c, h in units}
    dmat = {u: jnp.where(cmask[u[0] % 2], bcb[u] - b_row[u] + i_row[u], -jnp.inf) for u in units}
    m_loc = {u: jnp.max(dmat[u], axis=1, keepdims=True) for u in units}
    p_in = {u: jnp.exp(dmat[u] - m_loc[u]) for u in units}
    qk = {u: _dot_nt(q_m[u], k2[u]) for u in units}
    nd = {u: _dot(p_in[u] * qk[u], vx[u]) for u in units}
    kv = {(c, h): _dot(kt_ref[hsl[h], jsl[c]] * wk[c, h], vx[c, h]) for c, h in units}

    cn_in = {}
    for h in heads:
        cn = cn_ref[h]
        for c in range(nck):
            cn_in[c, h] = cn
            cn = dec[c, h] * cn + kv[c, h]
        cn_ref[h] = cn
    cn_pair = {(c, p): jnp.concatenate([cn_in[c, 2 * p], cn_in[c, 2 * p + 1]], axis=0).astype(BF16)
               for c in range(nck) for p in range(MLSTM_HEADS // 2)}

    inter = {u: bcb[u] + m_in[u] for u in units}
    mj = {u: jnp.maximum(m_loc[u], inter[u]) for u in units}
    e_loc = {u: jnp.exp(m_loc[u] - mj[u]) for u in units}
    e_int = {u: jnp.exp(inter[u] - mj[u]) for u in units}
    e_neg = {u: jnp.exp(-mj[u]) for u in units}
    qc = {(c, h): _dot(q_m[c, h], cn_pair[c, h // 2]) for c, h in units}
    numden = {u: e_loc[u] * nd[u] + e_int[u] * qc[u] for u in units}
    den = {(c, h): numden[c, h][:, den_lane[h % 2]:den_lane[h % 2] + 1] for c, h in units}
    hv = {u: numden[u] / jnp.maximum(jnp.abs(den[u]), e_neg[u]) for u in units}
    ms = {(c, h): jnp.sum(jnp.where(half[h % 2], hv[c, h] * hv[c, h], 0.0), axis=1, keepdims=True)
          * (1.0 / HEAD_DIM) for c, h in units}
    hn = {u: hv[u] * lax.rsqrt(ms[u] + RMS_EPS) for u in units}
    for c in range(nck):
        for p in range(MLSTM_HEADS // 2):
            ps = psl[2 * p]
            hb = jnp.where(half[0], hn[c, 2 * p], hn[c, 2 * p + 1])
            o_ref[rsl[c], ps] = hb * nrm_ref[0:1, ps] * jax.nn.sigmoid(og_ref[rsl[c], ps])


def _mlstm(zm, zg, ktg, bcol, brow, nrm, *, tt):
    b, t, _ = zm.shape
    nt = t // tt
    colblk = lambda j: pl.BlockSpec((None, tt, MLSTM_WIDTH), lambda bb, ti, j=j: (bb, ti, j))
    full = lambda bb, ti: (0, 0)
    return pl.pallas_call(
        functools.partial(_mlstm_kernel, tt=tt),
        grid=(b, nt),
        in_specs=[
            colblk(0), colblk(1), colblk(2), colblk(3),
            pl.BlockSpec((None, tt, ZG_COLS), lambda bb, ti: (bb, ti, 0)),
            pl.BlockSpec((KT_ROWS, tt), lambda bb, ti: (0, bb * nt + ti)),
            pl.BlockSpec((8, 128), full),
            pl.BlockSpec((8, 128), full),
            pl.BlockSpec((8, MLSTM_WIDTH), full),
        ],
        out_specs=pl.BlockSpec((None, tt, MLSTM_WIDTH), lambda bb, ti: (bb, ti, 0)),
        out_shape=jax.ShapeDtypeStruct((b, t, MLSTM_WIDTH), F32),
        scratch_shapes=[pltpu.VMEM((MLSTM_HEADS, HEAD_DIM, 128), F32), pltpu.VMEM((MLSTM_HEADS, 8, 128), F32)],
        compiler_params=_cparams(("parallel", "arbitrary")),
        name="mlstm",
    )(zm, zm, zm, zm, zg, ktg, bcol, brow, nrm)


def _rwkv_kernel(z_ref, mu_ref, pv_ref, wup_ref, aup_ref, gup_ref, seg_ref, o_ref,
                 prev_ref, st_ref, y_ref, *, tt):
    t = pl.program_id(1)

    @pl.when(t == 0)
    def _():
        prev_ref[...] = jnp.zeros_like(prev_ref)
        st_ref[...] = jnp.zeros_like(st_ref)

    L = CHUNK
    W = RWKV_WIDTH
    nck = tt // L
    z = z_ref[...]
    row1 = lax.broadcasted_iota(jnp.int32, (tt, 1), 0)
    zprev = jnp.where(row1 == 0, prev_ref[0:1, :], pltpu.roll(z, 1, axis=0))
    prev_ref[...] = jnp.broadcast_to(z[tt - 1:tt, :], prev_ref.shape)
    zs = z + (zprev - z) * mu_ref[...]
    r = zs[:, 0:W]
    k = zs[:, W:2 * W]
    v = zs[:, 2 * W:3 * W]
    wd = zs[:, 3 * W:3 * W + 64]
    ad = zs[:, 3 * W + 64:3 * W + 128]
    gd = zs[:, 3 * W + 128:3 * W + 256]
    pv = pv_ref[...]
    w0, a0, k_k, k_a, r_k, ln_w, ln_b = (pv[j:j + 1, :] for j in range(7))
    seg = seg_ref[...]

    log_w = -_softplus(-(w0 + _dot(jnp.tanh(wd), wup_ref[...]))) - 0.5
    ld = -jnp.exp(log_w)
    iclr = jax.nn.sigmoid(a0 + _dot(ad, aup_ref[...]))
    g = _dot(jax.nn.sigmoid(gd), gup_ref[...])
    kk = k * k_k
    kk = kk / jnp.maximum(jnp.sqrt(_segsum(kk * kk, seg)), 1e-12)
    k_mod = k * (1.0 + (iclr - 1.0) * k_a)
    a_vec = -kk
    b_vec = kk * iclr

    cum = ld
    posr = lax.broadcasted_iota(jnp.int32, (tt, W), 0) & (L - 1)
    d = 1
    while d < L:
        cum = cum + jnp.where(posr >= d, pltpu.roll(cum, d, axis=0), 0.0)
        d *= 2
    cum_last = jnp.concatenate(
        [jnp.broadcast_to(cum[(c + 1) * L - 1:(c + 1) * L, :], (L, W)) for c in range(nck)], axis=0)
    e_neg = jnp.exp(-cum)
    e_tail = jnp.exp(cum_last - cum)
    a_t = jnp.exp(cum - ld) * a_vec
    r_t = jnp.exp(cum) * r
    b_t = e_neg * b_vec
    k_t = e_neg * k_mod
    b_h = e_tail * b_vec
    k_h = e_tail * k_mod

    UR = 2 * L
    ri = lax.broadcasted_iota(jnp.int32, (UR, UR), 0)
    ci = lax.broadcasted_iota(jnp.int32, (UR, UR), 1)
    same = (ri // L) == (ci // L)
    strict = jnp.where(same & (ri > ci), 1.0, 0.0)
    incl = jnp.where(same & (ri >= ci), 1.0, 0.0)
    eye = jnp.where(ri == ci, 1.0, 0.0)

    heads = range(RWKV_HEADS)
    hsl = [slice(HEAD_DIM * h, HEAD_DIM * (h + 1)) for h in heads]
    bh_t, kh_t, cum_t = [], [], []
    for pair in range(RWKV_HEADS // 2):
        ps = slice(128 * pair, 128 * pair + 128)
        bp, kp, cp = b_h[:, ps].T.astype(BF16), k_h[:, ps].T.astype(BF16), cum[:, ps].T
        for e in range(2):
            es = slice(HEAD_DIM * e, HEAD_DIM * (e + 1))
            bh_t.append(bp[es])
            kh_t.append(kp[es])
            cum_t.append(cp[es])

    units = [(h, u) for u in range(tt // UR) for h in heads]
    nu = range(len(units))
    usl = [slice(UR * u, UR * (u + 1)) for _, u in units]
    at = [a_t[usl[i], hsl[h]].astype(BF16) for i, (h, _) in enumerate(units)]
    rt = [r_t[usl[i], hsl[h]] for i, (h, _) in enumerate(units)]
    bt = [b_t[usl[i], hsl[h]].astype(BF16) for i, (h, _) in enumerate(units)]
    kt = [k_t[usl[i], hsl[h]].astype(BF16) for i, (h, _) in enumerate(units)]
    vu = [v[usl[i], hsl[h]].astype(BF16) for i, (h, _) in enumerate(units)]
    mask4 = jnp.concatenate([jnp.concatenate([strict, strict], axis=1),
                             jnp.concatenate([incl, incl], axis=1)], axis=0)
    prod = [mask4 * _dot_nt(jnp.concatenate([at[i], rt[i].astype(BF16)], axis=0),
                            jnp.concatenate([bt[i], kt[i]], axis=0)) for i in nu]
    a_ab = [prod[i][0:UR, 0:UR] for i in nu]
    a_ak = [prod[i][0:UR, UR:2 * UR].astype(BF16) for i in nu]
    a_rbk = [prod[i][UR:2 * UR, :].astype(BF16) for i in nu]
    tm = [eye + a_ab[i] for i in nu]
    pw = [a_ab[i].astype(BF16) for i in nu]
    pw = [_dot(pw[i], pw[i]).astype(BF16) for i in nu]
    for _ in range(4):
        res = [_dot(pw[i], jnp.concatenate([pw[i], tm[i].astype(BF16)], axis=1)) for i in nu]
        pw = [res[i][:, 0:UR].astype(BF16) for i in nu]
        tm = [tm[i] + res[i][:, UR:2 * UR] for i in nu]
    tm = [tm[i] + _dot(pw[i], tm[i]) for i in nu]
    akv = [_dot(a_ak[i], vu[i]) for i in nu]
    tmb = [tm[i].astype(BF16) for i in nu]
    w_all = [_dot(tmb[i], at[i]).astype(BF16) for i in nu]
    u0_all = [_dot(tmb[i], akv[i]).astype(BF16) for i in nu]
    q_all = [(rt[i] + _dot(a_rbk[i][:, 0:UR], w_all[i])).astype(BF16) for i in nu]
    y0_all = [_dot(a_rbk[i], jnp.concatenate([u0_all[i], vu[i]], axis=0)) for i in nu]
    gm, hm, pcol, qc, y0c = {}, {}, {}, {}, {}
    for i, (h, u) in enumerate(units):
        for cc in range(UR // L):
            c = u * (UR // L) + cc
            ls = slice(cc * L, (cc + 1) * L)
            rs = slice(c * L, (c + 1) * L)
            gm[c, h] = _dot(bh_t[h][:, rs], w_all[i][ls]).astype(BF16)
            hm[c, h] = _dot(bh_t[h][:, rs], u0_all[i][ls]) + _dot(kh_t[h][:, rs], vu[i][ls])
            pcol[c, h] = jnp.exp(cum_t[h][:, (c + 1) * L - 1:(c + 1) * L])
            qc[c, h] = jnp.concatenate([q_all[i][ls], gm[c, h]], axis=0)
            y0c[c, h] = y0_all[i][ls]
    zst = [st_ref[h] for h in heads]
    for c in range(nck):
        rs = slice(c * L, (c + 1) * L)
        for h in heads:
            qz = _dot(qc[c, h], zst[h])
            y_ref[rs, hsl[h]] = qz[0:L] + y0c[c, h]
            zst[h] = pcol[c, h] * zst[h] + qz[L:2 * L] + hm[c, h]
    for h in heads:
        st_ref[h] = zst[h]

    y = y_ref[...]
    mean = _segsum(y, seg) * (1.0 / HEAD_DIM)
    yc = y - mean
    var = _segsum(yc * yc, seg) * (1.0 / HEAD_DIM)
    yn = yc * lax.rsqrt(var + GN_EPS) * ln_w + ln_b
    bonus = _segsum(r * k_mod * r_k, seg) * v
    o_ref[...] = (yn + bonus) * g


def _rwkv(zr, mu, pv, wup, aup, gup, seg, *, tt):
    b, t, _ = zr.shape
    full = lambda bb, ti: (0, 0)
    return pl.pallas_call(
        functools.partial(_rwkv_kernel, tt=tt),
        grid=(b, t // tt),
        in_specs=[
            pl.BlockSpec((None, tt, N_RWKV_COLS), lambda bb, ti: (bb, ti, 0)),
            pl.BlockSpec((1, N_RWKV_COLS), full),
            pl.BlockSpec((8, RWKV_WIDTH), full),
            pl.BlockSpec((64, RWKV_WIDTH), full),
            pl.BlockSpec((64, RWKV_WIDTH), full),
            pl.BlockSpec((128, RWKV_WIDTH), full),
            pl.BlockSpec((4 * HEAD_DIM, 4 * HEAD_DIM), full),
        ],
        out_specs=pl.BlockSpec((None, tt, RWKV_WIDTH), lambda bb, ti: (bb, ti, 0)),
        out_shape=jax.ShapeDtypeStruct((b, t, RWKV_WIDTH), F32),
        scratch_shapes=[
            pltpu.VMEM((8, N_RWKV_COLS), F32),
            pltpu.VMEM((RWKV_HEADS, HEAD_DIM, HEAD_DIM), F32),
            pltpu.VMEM((tt, RWKV_WIDTH), F32),
        ],
        compiler_params=_cparams(("parallel", "arbitrary")),
        name="rwkv7",
    )(zr, mu, pv, wup, aup, gup, seg)


def _tail_kernel(x_ref, ya_ref, yb_ref, yc_ref, p_ref, g_ref, wm_ref, wi_ref, wo_ref, wg_ref, wp_ref, o_ref,
                 h_ref, *, tf):
    c0, c1 = LRU_WIDTH, LRU_WIDTH + MLSTM_WIDTH
    tm = x_ref.shape[0]
    ya = jnp.concatenate(
        [jnp.concatenate([ya_ref[j, _perm_rows(m, tm), :] for j in range(LRU_WIDTH // 128)], axis=1)
         for m in range(tm // 8)], axis=0)
    mix = (jnp.dot(ya.astype(BF16), wm_ref[0:c0, :], preferred_element_type=F32)
           + jnp.dot(yb_ref[...].astype(BF16), wm_ref[c0:c1, :], preferred_element_type=F32)
           + jnp.dot(yc_ref[...].astype(BF16), wm_ref[c1:, :], preferred_element_type=F32))
    x = x_ref[...] + _rms(mix, g_ref[3:4, :])
    xn = _rms(x, g_ref[4:5, :]).astype(BF16)
    for c in range(D_FF // tf):
        gate = jnp.dot(xn, wi_ref[:, c * tf:(c + 1) * tf], preferred_element_type=F32)
        up = jnp.dot(xn, wi_ref[:, D_FF + c * tf:D_FF + (c + 1) * tf], preferred_element_type=F32)
        h_ref[:, c * tf:(c + 1) * tf] = (gate * jax.nn.sigmoid(gate) * up).astype(BF16)
    y = jnp.dot(h_ref[...], wo_ref[...], preferred_element_type=F32)
    x = x + 0.5 * _rms(y, g_ref[5:6, :])
    pgate = jax.nn.sigmoid(jnp.dot(_rms(x, g_ref[6:7, :]).astype(BF16), wg_ref[...], preferred_element_type=F32))
    pe = jnp.dot(p_ref[...].astype(BF16), wp_ref[...], preferred_element_type=F32)
    o_ref[...] = x + _rms(pgate * pe, g_ref[7:8, :])


def _tail(x, ya, yb, yc, p, g8, w_mix, w_in, w_out, w_gate, w_proj, *, tm, tf):
    n = x.shape[0]
    row = lambda i: (i, 0)
    const = lambda shape: pl.BlockSpec(shape, lambda i: (0, 0), pipeline_mode=pl.Buffered(1))
    return pl.pallas_call(
        functools.partial(_tail_kernel, tf=tf),
        grid=(n // tm,),
        in_specs=[
            pl.BlockSpec((tm, D_MODEL), row),
            pl.BlockSpec((LRU_WIDTH // 128, tm, 128), lambda i: (0, i, 0)),
            pl.BlockSpec((tm, MLSTM_WIDTH), row),
            pl.BlockSpec((tm, RWKV_WIDTH), row),
            pl.BlockSpec((tm, PLE_DIM), row),
            const((8, D_MODEL)),
            const((D_MODEL, D_MODEL)),
            const((D_MODEL, 2 * D_FF)),
            const((D_FF, D_MODEL)),
            const((D_MODEL, D_MODEL)),
            const((PLE_DIM, D_MODEL)),
        ],
        out_specs=pl.BlockSpec((tm, D_MODEL), row),
        out_shape=jax.ShapeDtypeStruct((n, D_MODEL), F32),
        scratch_shapes=[pltpu.VMEM((tm, D_FF), BF16)],
        compiler_params=_cparams(("parallel",)),
        name="tail",
    )(x, ya, yb, yc, p, g8, w_mix, w_in, w_out, w_gate, w_proj)


def _block_diag(w):
    nh, dd, _ = w.shape
    eye = jnp.eye(nh, dtype=w.dtype)
    return (eye[:, None, :, None] * w[:, :, None, :]).reshape(nh * dd, nh * dd)


def _tile(n, pref):
    return pref if n % pref == 0 else n


def kernel(x, p, norm_g, ffn_w_in, ffn_w_out, w_in, w_out, lru_conv_w, lru_conv_b, lru_w_a, lru_b_a, lru_w_x, lru_b_x, lru_lambda, m_b_i, m_b_f, m_norm, rw_mu, rw_w0, rw_w_up, rw_a0, rw_a_up, rw_g_up, rw_k_k, rw_k_a, rw_r_k, rw_ln_w, rw_ln_b, ple_w_proj, ple_w_gate):
    bsz, t, _ = x.shape
    n = bsz * t
    depth = norm_g.shape[0]
    tm_ffn = _tile(n, 512)
    tm = _tile(n, 512)
    tt_lru = _tile(t, 512)
    tt_m = _tile(t, 1024)
    tt_r = _tile(t, 512)
    seg = _block_diag(jnp.ones((4, HEAD_DIM, HEAD_DIM), BF16))

    xf = x.reshape(n, D_MODEL)
    for l in range(depth):
        g = norm_g[l]
        wl = w_in[l]
        o_zr = 2 * LRU_WIDTH + 4 * MLSTM_WIDTH + 2 * MLSTM_HEADS
        w_gates = wl[:, o_zr - 2 * MLSTM_HEADS:o_zr]
        wcat = jnp.concatenate(
            [wl[:, o_zr:], jnp.pad(w_gates, ((0, 0), (0, ZG_COLS - 2 * MLSTM_HEADS))),
             wl[:, 0:ZL_COLS], wl[:, ZL_COLS:ZL_COLS + ZM_COLS]], axis=1).astype(BF16)

        xf = _ffn(xf, g[0:2], ffn_w_in[l, 0].astype(BF16), ffn_w_out[l, 0].astype(BF16), tm=tm_ffn, tf=256)

        zr, zl, zm, zg, ktg = _mixer_in(xf, g[2:3], wcat, tm=tm)

        lru_pv = jnp.concatenate([lru_conv_w[l], lru_conv_b[l][None], lru_b_a[l][None], lru_b_x[l][None],
                                  lru_lambda[l][None]], axis=0)
        ya = _lru(zl.reshape(ZL_COLS // 128, bsz, t, 128), lru_pv, _block_diag(lru_w_a[l]).astype(BF16),
                  _block_diag(lru_w_x[l]).astype(BF16), tt=tt_lru)

        gate_b = jnp.concatenate([m_b_i[l], m_b_f[l]])
        bcol = jnp.broadcast_to(gate_b[:, None], (8, 128))
        brow = jnp.broadcast_to(jnp.pad(gate_b, (0, ZG_COLS - 8))[None, :], (8, ZG_COLS))
        nrm = jnp.broadcast_to(m_norm[l][None, :], (8, MLSTM_WIDTH))
        yb = _mlstm(zm.reshape(bsz, t, ZM_COLS), zg.reshape(bsz, t, ZG_COLS), ktg, bcol, brow, nrm, tt=tt_m)

        rw_pv = jnp.stack([rw_w0[l], rw_a0[l], rw_k_k[l], rw_k_a[l], rw_r_k[l].reshape(-1), rw_ln_w[l],
                           rw_ln_b[l], jnp.zeros((RWKV_WIDTH,), F32)], axis=0)
        yc = _rwkv(zr.reshape(bsz, t, ZR_COLS), rw_mu[l][None, :], rw_pv, rw_w_up[l].astype(BF16),
                   rw_a_up[l].astype(BF16), rw_g_up[l].astype(BF16), seg, tt=tt_r)

        xf = _tail(xf, ya.reshape(LRU_WIDTH // 128, n, 128), yb.reshape(n, MLSTM_WIDTH), yc.reshape(n, RWKV_WIDTH),
                   p[l].reshape(n, PLE_DIM), g, w_out[l].astype(BF16), ffn_w_in[l, 1].astype(BF16),
                   ffn_w_out[l, 1].astype(BF16), ple_w_gate[l].astype(BF16), ple_w_proj[l].astype(BF16),
                   tm=tm_ffn, tf=256)
    return xf.reshape(bsz, t, D_MODEL)
```

```python
import functools

import jax
import jax.numpy as jnp
from jax import lax
from jax.experimental import pallas as pl
from jax.experimental.pallas import tpu as pltpu

F32 = jnp.float32
BF16 = jnp.bfloat16

D_MODEL = 1024
PLE_DIM = 256
D_FF = 2816
RMS_EPS = 1e-6
LRU_WIDTH = 384
LRU_HEADS = 6
CONV_WIDTH = 4
LRU_C = 8.0
MLSTM_HEADS = 4
MLSTM_WIDTH = 256
RWKV_HEADS = 6
RWKV_WIDTH = 384
N_RWKV_COLS = 1408
GN_EPS = 64e-5
HEAD_DIM = 64
CHUNK = 64

ZR_COLS = N_RWKV_COLS
ZL_COLS = 2 * LRU_WIDTH
ZM_COLS = 4 * MLSTM_WIDTH
ZG_COLS = 128
KT_ROWS = MLSTM_WIDTH + 8

VMEM_LIMIT = 56 * 1024 * 1024


def _cparams(sem):
    return pltpu.CompilerParams(dimension_semantics=sem, vmem_limit_bytes=VMEM_LIMIT)


def _rms(x, g):
    return x * lax.rsqrt(jnp.mean(x * x, axis=-1, keepdims=True) + RMS_EPS) * g


def _softplus(x):
    return jnp.maximum(x, 0.0) + jnp.log(1.0 + jnp.exp(-jnp.abs(x)))


def _log_sigmoid(x):
    return -_softplus(-x)


def _dot(a, b):
    return jnp.dot(a.astype(BF16), b.astype(BF16), preferred_element_type=F32)


def _dot_nt(a, b):
    return lax.dot_general(a.astype(BF16), b.astype(BF16), (((1,), (1,)), ((), ())),
                           preferred_element_type=F32)


def _perm_rows(m, tile):
    ng = tile // 8
    s, g0 = divmod(8 * m, ng)
    return pl.ds(8 * g0 + s, 8, stride=8)


def _segsum(x, seg):
    hi = x.astype(BF16)
    lo = (x - hi.astype(F32)).astype(BF16)
    w = seg.shape[0]
    head = (jnp.dot(hi[:, 0:w], seg, preferred_element_type=F32)
            + jnp.dot(lo[:, 0:w], seg, preferred_element_type=F32))
    tail = jnp.dot(jnp.concatenate([hi[:, w:], lo[:, w:]], axis=1), seg, preferred_element_type=F32)
    return jnp.concatenate([head, tail[:, 0:w // 2] + tail[:, w // 2:]], axis=1)


def _ffn_kernel(x_ref, g_ref, wi_ref, wo_ref, o_ref, h_ref, *, tf):
    x = x_ref[...]
    xn = _rms(x, g_ref[0:1, :]).astype(BF16)
    for c in range(D_FF // tf):
        gate = jnp.dot(xn, wi_ref[:, c * tf:(c + 1) * tf], preferred_element_type=F32)
        up = jnp.dot(xn, wi_ref[:, D_FF + c * tf:D_FF + (c + 1) * tf], preferred_element_type=F32)
        h_ref[:, c * tf:(c + 1) * tf] = (gate * jax.nn.sigmoid(gate) * up).astype(BF16)
    y = jnp.dot(h_ref[...], wo_ref[...], preferred_element_type=F32)
    o_ref[...] = x + 0.5 * _rms(y, g_ref[1:2, :])


def _ffn(x, g2, w_in, w_out, *, tm, tf):
    n = x.shape[0]
    const = dict(pipeline_mode=pl.Buffered(1))
    return pl.pallas_call(
        functools.partial(_ffn_kernel, tf=tf),
        grid=(n // tm,),
        in_specs=[
            pl.BlockSpec((tm, D_MODEL), lambda i: (i, 0)),
            pl.BlockSpec((2, D_MODEL), lambda i: (0, 0)),
            pl.BlockSpec((D_MODEL, 2 * D_FF), lambda i: (0, 0), **const),
            pl.BlockSpec((D_FF, D_MODEL), lambda i: (0, 0), **const),
        ],
        out_specs=pl.BlockSpec((tm, D_MODEL), lambda i: (i, 0)),
        out_shape=jax.ShapeDtypeStruct((n, D_MODEL), F32),
        scratch_shapes=[pltpu.VMEM((tm, D_FF), BF16)],
        compiler_params=_cparams(("parallel",)),
        name="ffn",
    )(x, g2, w_in, w_out)


def _mixin_kernel(x_ref, g_ref, w_ref, zr_ref, zl_ref, zm_ref, zg_ref, kt_ref):
    xn = _rms(x_ref[...], g_ref[...]).astype(BF16)
    c0, c1, c2 = ZR_COLS + ZG_COLS, ZR_COLS + ZG_COLS + ZL_COLS, ZR_COLS + ZG_COLS + ZL_COLS + ZM_COLS
    zrg = jnp.dot(xn, w_ref[:, 0:c0], preferred_element_type=F32)
    zg = zrg[:, ZR_COLS:c0]
    zr_ref[...] = zrg[:, 0:ZR_COLS]
    zg_ref[...] = zg
    zl = jnp.dot(xn, w_ref[:, c0:c1], preferred_element_type=F32)
    for m in range(zl.shape[0] // 8):
        rows = _perm_rows(m, zl.shape[0])
        for j in range(ZL_COLS // 128):
            zl_ref[j, rows, :] = zl[8 * m:8 * m + 8, 128 * j:128 * (j + 1)]
    zm = jnp.dot(xn, w_ref[:, c1:c2], preferred_element_type=F32)
    zm_ref[...] = zm
    kt_ref[0:MLSTM_WIDTH, :] = zm[:, MLSTM_WIDTH:2 * MLSTM_WIDTH].T
    kt_ref[MLSTM_WIDTH:KT_ROWS, :] = zg.T[0:KT_ROWS - MLSTM_WIDTH, :]


def _mixer_in(x, g, wcat, *, tm):
    n = x.shape[0]
    ncols = wcat.shape[1]
    row = lambda i: (i, 0)
    return pl.pallas_call(
        _mixin_kernel,
        grid=(n // tm,),
        in_specs=[
            pl.BlockSpec((tm, D_MODEL), row),
            pl.BlockSpec((1, D_MODEL), lambda i: (0, 0)),
            pl.BlockSpec((D_MODEL, ncols), lambda i: (0, 0)),
        ],
        out_specs=[
            pl.BlockSpec((tm, ZR_COLS), row),
            pl.BlockSpec((ZL_COLS // 128, tm, 128), lambda i: (0, i, 0)),
            pl.BlockSpec((tm, ZM_COLS), row),
            pl.BlockSpec((tm, ZG_COLS), row),
            pl.BlockSpec((KT_ROWS, tm), lambda i: (0, i)),
        ],
        out_shape=[
            jax.ShapeDtypeStruct((n, ZR_COLS), F32),
            jax.ShapeDtypeStruct((ZL_COLS // 128, n, 128), F32),
            jax.ShapeDtypeStruct((n, ZM_COLS), F32),
            jax.ShapeDtypeStruct((n, ZG_COLS), F32),
            jax.ShapeDtypeStruct((KT_ROWS, n), F32),
        ],
        compiler_params=_cparams(("parallel",)),
        name="mixer_in",
    )(x, g, wcat)


def _lru_kernel(z_ref, pv_ref, wa_ref, wx_ref, o_ref, xtail_ref, h_ref, *, tt):
    t = pl.program_id(1)

    @pl.when(t == 0)
    def _():
        xtail_ref[...] = jnp.zeros_like(xtail_ref)
        h_ref[...] = jnp.zeros_like(h_ref)

    ng = tt // 8
    nslab = LRU_WIDTH // 128
    taps = CONV_WIDTH - 1
    sub = lax.broadcasted_iota(jnp.int32, (8, LRU_WIDTH), 0)

    def load(first_slab, g):
        return jnp.concatenate([z_ref[first_slab + j, 8 * g:8 * g + 8, :] for j in range(nslab)], axis=1)

    x = [load(0, g) for g in range(ng)]
    gate = jnp.concatenate([load(nslab, g) for g in range(ng)], axis=0)
    pv = pv_ref[...]

    def before(g):
        if g >= 0:
            return x[g]
        cur = pltpu.roll(x[ng + g], 1, axis=0)
        prev = pltpu.roll(xtail_ref[taps + g], 1, axis=0)
        return jnp.where(sub == 0, prev, cur)

    xa = []
    for g in range(ng):
        acc = pv[taps + 1:taps + 2, :] + pv[taps:taps + 1, :] * x[g]
        for d in range(1, taps + 1):
            acc = acc + pv[taps - d:taps - d + 1, :] * before(g - d)
        xa.append(acc)
    for d in range(taps):
        xtail_ref[d] = x[ng - taps + d]
    xa = jnp.concatenate(xa, axis=0)

    r = jax.nn.sigmoid(_dot(xa, wa_ref[...]) + pv[5:6, :])
    i = jax.nn.sigmoid(_dot(xa, wx_ref[...]) + pv[6:7, :])
    log_a = (-LRU_C) * r * _softplus(-pv[7:8, :])
    a = jnp.exp(log_a)
    u = jnp.sqrt(1.0 - jnp.exp(2.0 * log_a)) * (i * xa)
    gg = jax.nn.gelu(gate, approximate=True)

    h_loc, a_cum = [], []
    h = u[0:8, :]
    ac = a[0:8, :]
    h_loc.append(h)
    a_cum.append(ac)
    for g in range(1, ng):
        a_g = a[8 * g:8 * g + 8, :]
        h = a_g * h + u[8 * g:8 * g + 8, :]
        ac = a_g * ac
        h_loc.append(h)
        a_cum.append(ac)
    ae, he = ac, h
    for d in (1, 2, 4):
        keep = sub >= d
        a_sh = jnp.where(keep, pltpu.roll(ae, d, axis=0), 1.0)
        h_sh = jnp.where(keep, pltpu.roll(he, d, axis=0), 0.0)
        he = ae * h_sh + he
        ae = ae * a_sh
    h0 = h_ref[0:1, :]
    h_end = he + ae * h0
    h_ref[...] = jnp.broadcast_to(h_end[7:8, :], h_ref.shape)
    c_in = jnp.where(sub == 0, h0, pltpu.roll(h_end, 1, axis=0))
    for g in range(ng):
        out = (h_loc[g] + a_cum[g] * c_in) * gg[8 * g:8 * g + 8, :]
        for j in range(nslab):
            o_ref[j, 8 * g:8 * g + 8, :] = out[:, 128 * j:128 * (j + 1)]


def _lru(zl, pv, wa, wx, *, tt):
    _, b, t, _ = zl.shape
    nslab = LRU_WIDTH // 128
    full = lambda bb, ti: (0, 0)
    return pl.pallas_call(
        functools.partial(_lru_kernel, tt=tt),
        grid=(b, t // tt),
        in_specs=[
            pl.BlockSpec((2 * nslab, None, tt, 128), lambda bb, ti: (0, bb, ti, 0)),
            pl.BlockSpec((8, LRU_WIDTH), full),
            pl.BlockSpec((LRU_WIDTH, LRU_WIDTH), full),
            pl.BlockSpec((LRU_WIDTH, LRU_WIDTH), full),
        ],
        out_specs=pl.BlockSpec((nslab, None, tt, 128), lambda bb, ti: (0, bb, ti, 0)),
        out_shape=jax.ShapeDtypeStruct((nslab, b, t, 128), F32),
        scratch_shapes=[pltpu.VMEM((CONV_WIDTH - 1, 8, LRU_WIDTH), F32), pltpu.VMEM((8, LRU_WIDTH), F32)],
        compiler_params=_cparams(("parallel", "arbitrary")),
        name="rglru",
    )(zl, pv, wa, wx)


def _mlstm_kernel(q_ref, k_ref, v_ref, og_ref, gc_ref, kt_ref, bcol_ref, brow_ref, nrm_ref, o_ref,
                  cn_ref, m_ref, *, tt):
    t = pl.program_id(1)

    @pl.when(t == 0)
    def _():
        cn_ref[...] = jnp.zeros_like(cn_ref)
        m_ref[...] = jnp.zeros_like(m_ref)

    L = CHUNK
    gr = kt_ref[MLSTM_WIDTH:MLSTM_WIDTH + 8, :] + bcol_ref[:, 0:1]
    br = _log_sigmoid(gr)
    pos = lax.broadcasted_iota(jnp.int32, (8, tt), 1) & (L - 1)
    d = 1
    while d < L:
        br = br + jnp.where(pos >= d, pltpu.roll(br, d, axis=1), 0.0)
        d *= 2
    bc = _log_sigmoid(gc_ref[...] + brow_ref[0:1, :])
    posc = lax.broadcasted_iota(jnp.int32, (tt, ZG_COLS), 0) & (L - 1)
    d = 1
    while d < L:
        bc = bc + jnp.where(posc >= d, pltpu.roll(bc, d, axis=0), 0.0)
        d *= 2

    nck = tt // L
    heads = range(MLSTM_HEADS)
    pairs = range(MLSTM_HEADS // 2)
    units = [(c, h) for c in range(nck) for h in heads]
    punits = [(c, p) for c in range(nck) for p in pairs]
    rsl = [slice(c * L, (c + 1) * L) for c in range(nck)]
    psl = [slice(128 * p, 128 * p + 128) for p in pairs]
    jsl = [slice(128 * (c // 2), 128 * (c // 2) + 128) for c in range(nck)]

    lane = lax.broadcasted_iota(jnp.int32, (L, 128), 1)
    rowi = lax.broadcasted_iota(jnp.int32, (L, 128), 0)
    half0 = lane < HEAD_DIM
    causal = (lane & (HEAD_DIM - 1)) <= rowi
    lane1 = lax.broadcasted_iota(jnp.int32, (1, 128), 1)
    rmask = [lane1 < L, lane1 >= L]
    r2 = lax.broadcasted_iota(jnp.int32, (2 * L, 128), 0)
    l2 = lax.broadcasted_iota(jnp.int32, (2 * L, 128), 1)
    top = r2 < L
    diag = (r2 < L) == (l2 < HEAD_DIM)
    ones_bd = jnp.where(diag, 1.0, 0.0).astype(BF16)

    b_row = {(c, h): br[4 + h:5 + h, jsl[c]] for c, h in units}
    i_row = {(c, h): gr[h:h + 1, jsl[c]] for c, h in units}

    g = {(c, h): br[4 + h:5 + h, (c + 1) * L - 1:(c + 1) * L] for c, h in units}
    w_log = {u: jnp.where(rmask[u[0] % 2], g[u] - b_row[u] + i_row[u], -jnp.inf) for u in units}
    w_max = {u: jnp.max(w_log[u], axis=1, keepdims=True) for u in units}
    m_in, m_out = {}, {}
    for h in heads:
        m_st = m_ref[h, 0:1, 0:1]
        for c in range(nck):
            m_in[c, h] = m_st
            m_st = jnp.maximum(g[c, h] + m_st, w_max[c, h])
            m_out[c, h] = m_st
        m_ref[h] = jnp.broadcast_to(m_st, (8, 128))
    dec = {u: jnp.exp(g[u] + m_in[u] - m_out[u]) for u in units}
    wk = {u: jnp.exp(w_log[u] - m_out[u]) for u in units}

    def lanes_of_chunk(row128, c, e):
        return row128 if c % 2 == e else pltpu.roll(row128, HEAD_DIM, axis=1)

    def pair_row(d, c, p):
        return jnp.where(rmask[0], lanes_of_chunk(d[c, 2 * p], c, 0), lanes_of_chunk(d[c, 2 * p + 1], c, 1))

    def pair_tile(d, c, p):
        return jnp.where(half0, d[c, 2 * p], d[c, 2 * p + 1])

    q_p = {(c, p): (q_ref[rsl[c], psl[p]] * (HEAD_DIM ** -0.5)).astype(BF16) for c, p in punits}
    k_bd = {}
    v_bd = {}
    for c, p in punits:
        k_c = k_ref[rsl[c], psl[p]]
        v_c = v_ref[rsl[c], psl[p]]
        k_bd[c, p] = jnp.where(diag, jnp.concatenate([k_c, k_c], axis=0), 0.0).astype(BF16)
        v_bd[c, p] = jnp.concatenate([jnp.where(diag, jnp.concatenate([v_c, v_c], axis=0), 0.0).astype(BF16),
                                      ones_bd], axis=1)
    bcb = {(c, p): jnp.where(half0, jnp.broadcast_to(bc[rsl[c], 4 + 2 * p:5 + 2 * p], (L, 128)),
                             jnp.broadcast_to(bc[rsl[c], 5 + 2 * p:6 + 2 * p], (L, 128))) for c, p in punits}
    dmat = {(c, p): jnp.where(causal, bcb[c, p] - pair_row(b_row, c, p) + pair_row(i_row, c, p), -jnp.inf)
            for c, p in punits}
    m_loc = {u: jnp.where(half0, jnp.max(jnp.where(half0, dmat[u], -jnp.inf), axis=1, keepdims=True),
                          jnp.max(jnp.where(half0, -jnp.inf, dmat[u]), axis=1, keepdims=True)) for u in punits}
    p_in = {u: jnp.exp(dmat[u] - m_loc[u]) for u in punits}
    qk = {u: _dot_nt(q_p[u], k_bd[u]) for u in punits}
    nd = {u: _dot(p_in[u] * qk[u], v_bd[u]) for u in punits}
    kv = {}
    for c, p in punits:
        kw = kt_ref[psl[p], jsl[c]] * jnp.where(top, wk[c, 2 * p], wk[c, 2 * p + 1])
        vv = jnp.concatenate([v_ref[jsl[c], psl[p]].astype(BF16), jnp.ones((2 * L, 128), BF16)], axis=1)
        kv[c, p] = jnp.where(jnp.concatenate([diag, diag], axis=1), _dot(kw, vv), 0.0)

    cn_in = {}
    for p in pairs:
        cn = cn_ref[p]
        for c in range(nck):
            cn_in[c, p] = cn.astype(BF16)
            cn = jnp.concatenate([dec[c, 2 * p] * cn[0:L], dec[c, 2 * p + 1] * cn[L:2 * L]], axis=0) + kv[c, p]
        cn_ref[p] = cn

    inter = {(c, p): bcb[c, p] + pair_tile(m_in, c, p) for c, p in punits}
    mj = {u: jnp.maximum(m_loc[u], inter[u]) for u in punits}
    e_loc = {u: jnp.exp(m_loc[u] - mj[u]) for u in punits}
    e_int = {u: jnp.exp(inter[u] - mj[u]) for u in punits}
    e_neg = {u: jnp.exp(-mj[u]) for u in punits}
    qc = {u: _dot(q_p[u], cn_in[u]) for u in punits}
    num = {u: e_loc[u] * nd[u][:, 0:128] + e_int[u] * qc[u][:, 0:128] for u in punits}
    den = {u: e_loc[u] * nd[u][:, 128:256] + e_int[u] * qc[u][:, 128:256] for u in punits}
    hv = {u: num[u] / jnp.maximum(jnp.abs(den[u]), e_neg[u]) for u in punits}
    hv2 = {u: hv[u] * hv[u] for u in punits}
    ms = {u: jnp.where(half0, jnp.sum(jnp.where(half0, hv2[u], 0.0), axis=1, keepdims=True),
                       jnp.sum(jnp.where(half0, 0.0, hv2[u]), axis=1, keepdims=True)) * (1.0 / HEAD_DIM)
          for u in punits}
    for c, p in punits:
        hn = hv[c, p] * lax.rsqrt(ms[c, p] + RMS_EPS)
        o_ref[rsl[c], psl[p]] = hn * nrm_ref[0:1, psl[p]] * jax.nn.sigmoid(og_ref[rsl[c], psl[p]])


def _mlstm(zm, zg, ktg, bcol, brow, nrm, *, tt):
    b, t, _ = zm.shape
    nt = t // tt
    colblk = lambda j: pl.BlockSpec((None, tt, MLSTM_WIDTH), lambda bb, ti, j=j: (bb, ti, j))
    full = lambda bb, ti: (0, 0)
    return pl.pallas_call(
        functools.partial(_mlstm_kernel, tt=tt),
        grid=(b, nt),
        in_specs=[
            colblk(0), colblk(1), colblk(2), colblk(3),
            pl.BlockSpec((None, tt, ZG_COLS), lambda bb, ti: (bb, ti, 0)),
            pl.BlockSpec((KT_ROWS, tt), lambda bb, ti: (0, bb * nt + ti)),
            pl.BlockSpec((8, 128), full),
            pl.BlockSpec((8, 128), full),
            pl.BlockSpec((8, MLSTM_WIDTH), full),
        ],
        out_specs=pl.BlockSpec((None, tt, MLSTM_WIDTH), lambda bb, ti: (bb, ti, 0)),
        out_shape=jax.ShapeDtypeStruct((b, t, MLSTM_WIDTH), F32),
        scratch_shapes=[pltpu.VMEM((MLSTM_HEADS // 2, 2 * HEAD_DIM, 256), F32),
                        pltpu.VMEM((MLSTM_HEADS, 8, 128), F32)],
        compiler_params=_cparams(("parallel", "arbitrary")),
        name="mlstm",
    )(zm, zm, zm, zm, zg, ktg, bcol, brow, nrm)


def _rwkv_kernel(z_ref, mu_ref, pv_ref, wup_ref, aup_ref, gup_ref, seg_ref, o_ref,
                 prev_ref, st_ref, y_ref, *, tt):
    t = pl.program_id(1)

    @pl.when(t == 0)
    def _():
        prev_ref[...] = jnp.zeros_like(prev_ref)
        st_ref[...] = jnp.zeros_like(st_ref)

    L = CHUNK
    W = RWKV_WIDTH
    nck = tt // L
    z = z_ref[...]
    row1 = lax.broadcasted_iota(jnp.int32, (tt, 1), 0)
    zprev = jnp.where(row1 == 0, prev_ref[0:1, :], pltpu.roll(z, 1, axis=0))
    prev_ref[...] = jnp.broadcast_to(z[tt - 1:tt, :], prev_ref.shape)
    zs = z + (zprev - z) * mu_ref[...]
    r = zs[:, 0:W]
    k = zs[:, W:2 * W]
    v = zs[:, 2 * W:3 * W]
    wd = zs[:, 3 * W:3 * W + 64]
    ad = zs[:, 3 * W + 64:3 * W + 128]
    gd = zs[:, 3 * W + 128:3 * W + 256]
    pv = pv_ref[...]
    w0, a0, k_k, k_a, r_k, ln_w, ln_b = (pv[j:j + 1, :] for j in range(7))
    seg = seg_ref[...]

    log_w = -_softplus(-(w0 + _dot(jnp.tanh(wd), wup_ref[...]))) - 0.5
    ld = -jnp.exp(log_w)
    iclr = jax.nn.sigmoid(a0 + _dot(ad, aup_ref[...]))
    g = _dot(jax.nn.sigmoid(gd), gup_ref[...])
    kk = k * k_k
    kk = kk / jnp.maximum(jnp.sqrt(_segsum(kk * kk, seg)), 1e-12)
    k_mod = k * (1.0 + (iclr - 1.0) * k_a)
    a_vec = -kk
    b_vec = kk * iclr

    cum = ld
    posr = lax.broadcasted_iota(jnp.int32, (tt, W), 0) & (L - 1)
    d = 1
    while d < L:
        cum = cum + jnp.where(posr >= d, pltpu.roll(cum, d, axis=0), 0.0)
        d *= 2
    cum_last = jnp.concatenate(
        [jnp.broadcast_to(cum[(c + 1) * L - 1:(c + 1) * L, :], (L, W)) for c in range(nck)], axis=0)
    e_neg = jnp.exp(-cum)
    e_tail = jnp.exp(cum_last - cum)
    a_t = jnp.exp(cum - ld) * a_vec
    r_t = jnp.exp(cum) * r
    b_t = e_neg * b_vec
    k_t = e_neg * k_mod
    b_h = e_tail * b_vec
    k_h = e_tail * k_mod

    UR = 2 * L
    ri = lax.broadcasted_iota(jnp.int32, (UR, UR), 0)
    ci = lax.broadcasted_iota(jnp.int32, (UR, UR), 1)
    same = (ri // L) == (ci // L)
    strict = jnp.where(same & (ri > ci), 1.0, 0.0)
    incl = jnp.where(same & (ri >= ci), 1.0, 0.0)
    eye = jnp.where(ri == ci, 1.0, 0.0)

    heads = range(RWKV_HEADS)
    hsl = [slice(HEAD_DIM * h, HEAD_DIM * (h + 1)) for h in heads]
    bh_t, kh_t, cum_t = [], [], []
    for pair in range(RWKV_HEADS // 2):
        ps = slice(128 * pair, 128 * pair + 128)
        bp, kp, cp = b_h[:, ps].T.astype(BF16), k_h[:, ps].T.astype(BF16), cum[:, ps].T
        for e in range(2):
            es = slice(HEAD_DIM * e, HEAD_DIM * (e + 1))
            bh_t.append(bp[es])
            kh_t.append(kp[es])
            cum_t.append(cp[es])

    units = [(h, u) for u in range(tt // UR) for h in heads]
    nu = range(len(units))
    usl = [slice(UR * u, UR * (u + 1)) for _, u in units]
    at = [a_t[usl[i], hsl[h]].astype(BF16) for i, (h, _) in enumerate(units)]
    rt = [r_t[usl[i], hsl[h]] for i, (h, _) in enumerate(units)]
    bt = [b_t[usl[i], hsl[h]].astype(BF16) for i, (h, _) in enumerate(units)]
    kt = [k_t[usl[i], hsl[h]].astype(BF16) for i, (h, _) in enumerate(units)]
    vu = [v[usl[i], hsl[h]].astype(BF16) for i, (h, _) in enumerate(units)]
    mask4 = jnp.concatenate([jnp.concatenate([strict, strict], axis=1),
                             jnp.concatenate([incl, incl], axis=1)], axis=0)
    prod = [mask4 * _dot_nt(jnp.concatenate([at[i], rt[i].astype(BF16)], axis=0),
                            jnp.concatenate([bt[i], kt[i]], axis=0)) for i in nu]
    a_ab = [prod[i][0:UR, 0:UR] for i in nu]
    a_ak = [prod[i][0:UR, UR:2 * UR].astype(BF16) for i in nu]
    a_rbk = [prod[i][UR:2 * UR, :].astype(BF16) for i in nu]
    tm = [eye + a_ab[i] for i in nu]
    pw = [a_ab[i].astype(BF16) for i in nu]
    pw = [_dot(pw[i], pw[i]).astype(BF16) for i in nu]
    for _ in range(4):
        res = [_dot(pw[i], jnp.concatenate([pw[i], tm[i].astype(BF16)], axis=1)) for i in nu]
        pw = [res[i][:, 0:UR].astype(BF16) for i in nu]
        tm = [tm[i] + res[i][:, UR:2 * UR] for i in nu]
    tm = [tm[i] + _dot(pw[i], tm[i]) for i in nu]
    akv = [_dot(a_ak[i], vu[i]) for i in nu]
    tmb = [tm[i].astype(BF16) for i in nu]
    w_all = [_dot(tmb[i], at[i]).astype(BF16) for i in nu]
    u0_all = [_dot(tmb[i], akv[i]).astype(BF16) for i in nu]
    q_all = [(rt[i] + _dot(a_rbk[i][:, 0:UR], w_all[i])).astype(BF16) for i in nu]
    y0_all = [_dot(a_rbk[i], jnp.concatenate([u0_all[i], vu[i]], axis=0)) for i in nu]
    gm, hm, pcol, qc, y0c = {}, {}, {}, {}, {}
    for i, (h, u) in enumerate(units):
        for cc in range(UR // L):
            c = u * (UR // L) + cc
            ls = slice(cc * L, (cc + 1) * L)
            rs = slice(c * L, (c + 1) * L)
            gm[c, h] = _dot(bh_t[h][:, rs], w_all[i][ls]).astype(BF16)
            hm[c, h] = _dot(bh_t[h][:, rs], u0_all[i][ls]) + _dot(kh_t[h][:, rs], vu[i][ls])
            pcol[c, h] = jnp.exp(cum_t[h][:, (c + 1) * L - 1:(c + 1) * L])
            qc[c, h] = jnp.concatenate([q_all[i][ls], gm[c, h]], axis=0)
            y0c[c, h] = y0_all[i][ls]
    zst = [st_ref[h] for h in heads]
    for c in range(nck):
        rs = slice(c * L, (c + 1) * L)
        for h in heads:
            qz = _dot(qc[c, h], zst[h])
            y_ref[rs, hsl[h]] = qz[0:L] + y0c[c, h]
            zst[h] = pcol[c, h] * zst[h] + qz[L:2 * L] + hm[c, h]
    for h in heads:
        st_ref[h] = zst[h]

    y = y_ref[...]
    mean = _segsum(y, seg) * (1.0 / HEAD_DIM)
    yc = y - mean
    var = _segsum(yc * yc, seg) * (1.0 / HEAD_DIM)
    yn = yc * lax.rsqrt(var + GN_EPS) * ln_w + ln_b
    bonus = _segsum(r * k_mod * r_k, seg) * v
    o_ref[...] = (yn + bonus) * g


def _rwkv(zr, mu, pv, wup, aup, gup, seg, *, tt):
    b, t, _ = zr.shape
    full = lambda bb, ti: (0, 0)
    return pl.pallas_call(
        functools.partial(_rwkv_kernel, tt=tt),
        grid=(b, t // tt),
        in_specs=[
            pl.BlockSpec((None, tt, N_RWKV_COLS), lambda bb, ti: (bb, ti, 0)),
            pl.BlockSpec((1, N_RWKV_COLS), full),
            pl.BlockSpec((8, RWKV_WIDTH), full),
            pl.BlockSpec((64, RWKV_WIDTH), full),
            pl.BlockSpec((64, RWKV_WIDTH), full),
            pl.BlockSpec((128, RWKV_WIDTH), full),
            pl.BlockSpec((4 * HEAD_DIM, 4 * HEAD_DIM), full),
        ],
        out_specs=pl.BlockSpec((None, tt, RWKV_WIDTH), lambda bb, ti: (bb, ti, 0)),
        out_shape=jax.ShapeDtypeStruct((b, t, RWKV_WIDTH), F32),
        scratch_shapes=[
            pltpu.VMEM((8, N_RWKV_COLS), F32),
            pltpu.VMEM((RWKV_HEADS, HEAD_DIM, HEAD_DIM), F32),
            pltpu.VMEM((tt, RWKV_WIDTH), F32),
        ],
        compiler_params=_cparams(("parallel", "arbitrary")),
        name="rwkv7",
    )(zr, mu, pv, wup, aup, gup, seg)


def _tail_kernel(x_ref, ya_ref, yb_ref, yc_ref, p_ref, g_ref, wm_ref, wi_ref, wo_ref, wg_ref, wp_ref, o_ref,
                 h_ref, *, tf):
    c0, c1 = LRU_WIDTH, LRU_WIDTH + MLSTM_WIDTH
    tm = x_ref.shape[0]
    ya = jnp.concatenate(
        [jnp.concatenate([ya_ref[j, _perm_rows(m, tm), :] for j in range(LRU_WIDTH // 128)], axis=1)
         for m in range(tm // 8)], axis=0)
    mix = (jnp.dot(ya.astype(BF16), wm_ref[0:c0, :], preferred_element_type=F32)
           + jnp.dot(yb_ref[...].astype(BF16), wm_ref[c0:c1, :], preferred_element_type=F32)
           + jnp.dot(yc_ref[...].astype(BF16), wm_ref[c1:, :], preferred_element_type=F32))
    x = x_ref[...] + _rms(mix, g_ref[3:4, :])
    xn = _rms(x, g_ref[4:5, :]).astype(BF16)
    for c in range(D_FF // tf):
        gate = jnp.dot(xn, wi_ref[:, c * tf:(c + 1) * tf], preferred_element_type=F32)
        up = jnp.dot(xn, wi_ref[:, D_FF + c * tf:D_FF + (c + 1) * tf], preferred_element_type=F32)
        h_ref[:, c * tf:(c + 1) * tf] = (gate * jax.nn.sigmoid(gate) * up).astype(BF16)
    y = jnp.dot(h_ref[...], wo_ref[...], preferred_element_type=F32)
    x = x + 0.5 * _rms(y, g_ref[5:6, :])
    pgate = jax.nn.sigmoid(jnp.dot(_rms(x, g_ref[6:7, :]).astype(BF16), wg_ref[...], preferred_element_type=F32))
    pe = jnp.dot(p_ref[...].astype(BF16), wp_ref[...], preferred_element_type=F32)
    o_ref[...] = x + _rms(pgate * pe, g_ref[7:8, :])


def _tail(x, ya, yb, yc, p, g8, w_mix, w_in, w_out, w_gate, w_proj, *, tm, tf):
    n = x.shape[0]
    row = lambda i: (i, 0)
    const = lambda shape: pl.BlockSpec(shape, lambda i: (0, 0), pipeline_mode=pl.Buffered(1))
    return pl.pallas_call(
        functools.partial(_tail_kernel, tf=tf),
        grid=(n // tm,),
        in_specs=[
            pl.BlockSpec((tm, D_MODEL), row),
            pl.BlockSpec((LRU_WIDTH // 128, tm, 128), lambda i: (0, i, 0)),
            pl.BlockSpec((tm, MLSTM_WIDTH), row),
            pl.BlockSpec((tm, RWKV_WIDTH), row),
            pl.BlockSpec((tm, PLE_DIM), row),
            const((8, D_MODEL)),
            const((D_MODEL, D_MODEL)),
            const((D_MODEL, 2 * D_FF)),
            const((D_FF, D_MODEL)),
            const((D_MODEL, D_MODEL)),
            const((PLE_DIM, D_MODEL)),
        ],
        out_specs=pl.BlockSpec((tm, D_MODEL), row),
        out_shape=jax.ShapeDtypeStruct((n, D_MODEL), F32),
        scratch_shapes=[pltpu.VMEM((tm, D_FF), BF16)],
        compiler_params=_cparams(("parallel",)),
        name="tail",
    )(x, ya, yb, yc, p, g8, w_mix, w_in, w_out, w_gate, w_proj)


def _block_diag(w):
    nh, dd, _ = w.shape
    eye = jnp.eye(nh, dtype=w.dtype)
    return (eye[:, None, :, None] * w[:, :, None, :]).reshape(nh * dd, nh * dd)


def _tile(n, pref):
    return pref if n % pref == 0 else n


def kernel(x, p, norm_g, ffn_w_in, ffn_w_out, w_in, w_out, lru_conv_w, lru_conv_b, lru_w_a, lru_b_a, lru_w_x, lru_b_x, lru_lambda, m_b_i, m_b_f, m_norm, rw_mu, rw_w0, rw_w_up, rw_a0, rw_a_up, rw_g_up, rw_k_k, rw_k_a, rw_r_k, rw_ln_w, rw_ln_b, ple_w_proj, ple_w_gate):
    bsz, t, _ = x.shape
    n = bsz * t
    depth = norm_g.shape[0]
    tm_ffn = _tile(n, 512)
    tm = _tile(n, 512)
    tt_lru = _tile(t, 512)
    tt_m = _tile(t, 1024)
    tt_r = _tile(t, 512)
    seg = _block_diag(jnp.ones((4, HEAD_DIM, HEAD_DIM), BF16))

    xf = x.reshape(n, D_MODEL)
    for l in range(depth):
        g = norm_g[l]
        wl = w_in[l]
        o_zr = 2 * LRU_WIDTH + 4 * MLSTM_WIDTH + 2 * MLSTM_HEADS
        w_gates = wl[:, o_zr - 2 * MLSTM_HEADS:o_zr]
        wcat = jnp.concatenate(
            [wl[:, o_zr:], jnp.pad(w_gates, ((0, 0), (0, ZG_COLS - 2 * MLSTM_HEADS))),
             wl[:, 0:ZL_COLS], wl[:, ZL_COLS:ZL_COLS + ZM_COLS]], axis=1).astype(BF16)

        xf = _ffn(xf, g[0:2], ffn_w_in[l, 0].astype(BF16), ffn_w_out[l, 0].astype(BF16), tm=tm_ffn, tf=256)

        zr, zl, zm, zg, ktg = _mixer_in(xf, g[2:3], wcat, tm=tm)

        lru_pv = jnp.concatenate([lru_conv_w[l], lru_conv_b[l][None], lru_b_a[l][None], lru_b_x[l][None],
                                  lru_lambda[l][None]], axis=0)
        ya = _lru(zl.reshape(ZL_COLS // 128, bsz, t, 128), lru_pv, _block_diag(lru_w_a[l]).astype(BF16),
                  _block_diag(lru_w_x[l]).astype(BF16), tt=tt_lru)

        gate_b = jnp.concatenate([m_b_i[l], m_b_f[l]])
        bcol = jnp.broadcast_to(gate_b[:, None], (8, 128))
        brow = jnp.broadcast_to(jnp.pad(gate_b, (0, ZG_COLS - 8))[None, :], (8, ZG_COLS))
        nrm = jnp.broadcast_to(m_norm[l][None, :], (8, MLSTM_WIDTH))
        yb = _mlstm(zm.reshape(bsz, t, ZM_COLS), zg.reshape(bsz, t, ZG_COLS), ktg, bcol, brow, nrm, tt=tt_m)

        rw_pv = jnp.stack([rw_w0[l], rw_a0[l], rw_k_k[l], rw_k_a[l], rw_r_k[l].reshape(-1), rw_ln_w[l],
                           rw_ln_b[l], jnp.zeros((RWKV_WIDTH,), F32)], axis=0)
        yc = _rwkv(zr.reshape(bsz, t, ZR_COLS), rw_mu[l][None, :], rw_pv, rw_w_up[l].astype(BF16),
                   rw_a_up[l].astype(BF16), rw_g_up[l].astype(BF16), seg, tt=tt_r)

        xf = _tail(xf, ya.reshape(LRU_WIDTH // 128, n, 128), yb.reshape(n, MLSTM_WIDTH), yc.reshape(n, RWKV_WIDTH),
                   p[l].reshape(n, PLE_DIM), g, w_out[l].astype(BF16), ffn_w_in[l, 1].astype(BF16),
                   ffn_w_out[l, 1].astype(BF16), ple_w_gate[l].astype(BF16), ple_w_proj[l].astype(BF16),
                   tm=tm_ffn, tf=256)
    return xf.reshape(bsz, t, D_MODEL)
```

```python
import functools

import jax
import jax.numpy as jnp
from jax import lax
from jax.experimental import pallas as pl
from jax.experimental.pallas import tpu as pltpu

F32 = jnp.float32
BF16 = jnp.bfloat16

D_MODEL = 1024
PLE_DIM = 256
D_FF = 2816
RMS_EPS = 1e-6
LRU_WIDTH = 384
LRU_HEADS = 6
CONV_WIDTH = 4
LRU_C = 8.0
MLSTM_HEADS = 4
MLSTM_WIDTH = 256
RWKV_HEADS = 6
RWKV_WIDTH = 384
N_RWKV_COLS = 1408
GN_EPS = 64e-5
HEAD_DIM = 64
CHUNK = 64

ZR_COLS = N_RWKV_COLS
ZL_COLS = 2 * LRU_WIDTH
ZM_COLS = 4 * MLSTM_WIDTH
ZG_COLS = 128
KT_ROWS = MLSTM_WIDTH + 8

VMEM_LIMIT = 56 * 1024 * 1024


def _cparams(sem):
    return pltpu.CompilerParams(dimension_semantics=sem, vmem_limit_bytes=VMEM_LIMIT)


def _rms(x, g):
    return x * lax.rsqrt(jnp.mean(x * x, axis=-1, keepdims=True) + RMS_EPS) * g


def _softplus(x):
    return jnp.maximum(x, 0.0) + jnp.log(1.0 + jnp.exp(-jnp.abs(x)))


def _log_sigmoid(x):
    return -_softplus(-x)


def _dot(a, b):
    return jnp.dot(a.astype(BF16), b.astype(BF16), preferred_element_type=F32)


def _dot_nt(a, b):
    return lax.dot_general(a.astype(BF16), b.astype(BF16), (((1,), (1,)), ((), ())),
                           preferred_element_type=F32)


def _perm_rows(m, tile):
    ng = tile // 8
    s, g0 = divmod(8 * m, ng)
    return pl.ds(8 * g0 + s, 8, stride=8)


def _segsum(x, seg):
    hi = x.astype(BF16)
    lo = (x - hi.astype(F32)).astype(BF16)
    w = seg.shape[0]
    head = (jnp.dot(hi[:, 0:w], seg, preferred_element_type=F32)
            + jnp.dot(lo[:, 0:w], seg, preferred_element_type=F32))
    tail = jnp.dot(jnp.concatenate([hi[:, w:], lo[:, w:]], axis=1), seg, preferred_element_type=F32)
    return jnp.concatenate([head, tail[:, 0:w // 2] + tail[:, w // 2:]], axis=1)


def _ffn_kernel(x_ref, g_ref, wi_ref, wo_ref, o_ref, h_ref, *, tf):
    x = x_ref[...]
    xn = _rms(x, g_ref[0:1, :]).astype(BF16)
    for c in range(D_FF // tf):
        gate = jnp.dot(xn, wi_ref[:, c * tf:(c + 1) * tf], preferred_element_type=F32)
        up = jnp.dot(xn, wi_ref[:, D_FF + c * tf:D_FF + (c + 1) * tf], preferred_element_type=F32)
        h_ref[:, c * tf:(c + 1) * tf] = (gate * jax.nn.sigmoid(gate) * up).astype(BF16)
    y = jnp.dot(h_ref[...], wo_ref[...], preferred_element_type=F32)
    o_ref[...] = x + 0.5 * _rms(y, g_ref[1:2, :])


def _ffn(x, g2, w_in, w_out, *, tm, tf):
    n = x.shape[0]
    const = dict(pipeline_mode=pl.Buffered(1))
    return pl.pallas_call(
        functools.partial(_ffn_kernel, tf=tf),
        grid=(n // tm,),
        in_specs=[
            pl.BlockSpec((tm, D_MODEL), lambda i: (i, 0)),
            pl.BlockSpec((2, D_MODEL), lambda i: (0, 0)),
            pl.BlockSpec((D_MODEL, 2 * D_FF), lambda i: (0, 0), **const),
            pl.BlockSpec((D_FF, D_MODEL), lambda i: (0, 0), **const),
        ],
        out_specs=pl.BlockSpec((tm, D_MODEL), lambda i: (i, 0)),
        out_shape=jax.ShapeDtypeStruct((n, D_MODEL), F32),
        scratch_shapes=[pltpu.VMEM((tm, D_FF), BF16)],
        compiler_params=_cparams(("parallel",)),
        name="ffn",
    )(x, g2, w_in, w_out)


def _mixin_kernel(x_ref, g_ref, w_ref, zr_ref, zl_ref, zm_ref, zg_ref, kt_ref):
    xn = _rms(x_ref[...], g_ref[...]).astype(BF16)
    c0, c1, c2 = ZR_COLS + ZG_COLS, ZR_COLS + ZG_COLS + ZL_COLS, ZR_COLS + ZG_COLS + ZL_COLS + ZM_COLS
    zrg = jnp.dot(xn, w_ref[:, 0:c0], preferred_element_type=F32)
    zg = zrg[:, ZR_COLS:c0]
    zr_ref[...] = zrg[:, 0:ZR_COLS]
    zg_ref[...] = zg
    zl = jnp.dot(xn, w_ref[:, c0:c1], preferred_element_type=F32)
    for m in range(zl.shape[0] // 8):
        rows = _perm_rows(m, zl.shape[0])
        for j in range(ZL_COLS // 128):
            zl_ref[j, rows, :] = zl[8 * m:8 * m + 8, 128 * j:128 * (j + 1)]
    zm = jnp.dot(xn, w_ref[:, c1:c2], preferred_element_type=F32)
    zm_ref[...] = zm
    kt_ref[0:MLSTM_WIDTH, :] = zm[:, MLSTM_WIDTH:2 * MLSTM_WIDTH].T
    kt_ref[MLSTM_WIDTH:KT_ROWS, :] = zg.T[0:KT_ROWS - MLSTM_WIDTH, :]


def _mixer_in(x, g, wcat, *, tm):
    n = x.shape[0]
    ncols = wcat.shape[1]
    row = lambda i: (i, 0)
    return pl.pallas_call(
        _mixin_kernel,
        grid=(n // tm,),
        in_specs=[
            pl.BlockSpec((tm, D_MODEL), row),
            pl.BlockSpec((1, D_MODEL), lambda i: (0, 0)),
            pl.BlockSpec((D_MODEL, ncols), lambda i: (0, 0)),
        ],
        out_specs=[
            pl.BlockSpec((tm, ZR_COLS), row),
            pl.BlockSpec((ZL_COLS // 128, tm, 128), lambda i: (0, i, 0)),
            pl.BlockSpec((tm, ZM_COLS), row),
            pl.BlockSpec((tm, ZG_COLS), row),
            pl.BlockSpec((KT_ROWS, tm), lambda i: (0, i)),
        ],
        out_shape=[
            jax.ShapeDtypeStruct((n, ZR_COLS), F32),
            jax.ShapeDtypeStruct((ZL_COLS // 128, n, 128), F32),
            jax.ShapeDtypeStruct((n, ZM_COLS), F32),
            jax.ShapeDtypeStruct((n, ZG_COLS), F32),
            jax.ShapeDtypeStruct((KT_ROWS, n), F32),
        ],
        compiler_params=_cparams(("parallel",)),
        name="mixer_in",
    )(x, g, wcat)


def _lru_kernel(z_ref, pv_ref, wa_ref, wx_ref, o_ref, xtail_ref, h_ref, *, tt):
    t = pl.program_id(1)

    @pl.when(t == 0)
    def _():
        xtail_ref[...] = jnp.zeros_like(xtail_ref)
        h_ref[...] = jnp.zeros_like(h_ref)

    ng = tt // 8
    nslab = LRU_WIDTH // 128
    taps = CONV_WIDTH - 1
    sub = lax.broadcasted_iota(jnp.int32, (8, LRU_WIDTH), 0)

    def load(first_slab, g):
        return jnp.concatenate([z_ref[first_slab + j, 8 * g:8 * g + 8, :] for j in range(nslab)], axis=1)

    x = [load(0, g) for g in range(ng)]
    gate = jnp.concatenate([load(nslab, g) for g in range(ng)], axis=0)
    pv = pv_ref[...]

    def before(g):
        if g >= 0:
            return x[g]
        cur = pltpu.roll(x[ng + g], 1, axis=0)
        prev = pltpu.roll(xtail_ref[taps + g], 1, axis=0)
        return jnp.where(sub == 0, prev, cur)

    xa = []
    for g in range(ng):
        acc = pv[taps + 1:taps + 2, :] + pv[taps:taps + 1, :] * x[g]
        for d in range(1, taps + 1):
            acc = acc + pv[taps - d:taps - d + 1, :] * before(g - d)
        xa.append(acc)
    for d in range(taps):
        xtail_ref[d] = x[ng - taps + d]
    xa = jnp.concatenate(xa, axis=0)

    r = jax.nn.sigmoid(_dot(xa, wa_ref[...]) + pv[5:6, :])
    i = jax.nn.sigmoid(_dot(xa, wx_ref[...]) + pv[6:7, :])
    log_a = (-LRU_C) * r * _softplus(-pv[7:8, :])
    a = jnp.exp(log_a)
    u = jnp.sqrt(1.0 - jnp.exp(2.0 * log_a)) * (i * xa)
    gg = jax.nn.gelu(gate, approximate=True)

    h_loc, a_cum = [], []
    h = u[0:8, :]
    ac = a[0:8, :]
    h_loc.append(h)
    a_cum.append(ac)
    for g in range(1, ng):
        a_g = a[8 * g:8 * g + 8, :]
        h = a_g * h + u[8 * g:8 * g + 8, :]
        ac = a_g * ac
        h_loc.append(h)
        a_cum.append(ac)
    ae, he = ac, h
    for d in (1, 2, 4):
        keep = sub >= d
        a_sh = jnp.where(keep, pltpu.roll(ae, d, axis=0), 1.0)
        h_sh = jnp.where(keep, pltpu.roll(he, d, axis=0), 0.0)
        he = ae * h_sh + he
        ae = ae * a_sh
    h0 = h_ref[0:1, :]
    h_end = he + ae * h0
    h_ref[...] = jnp.broadcast_to(h_end[7:8, :], h_ref.shape)
    c_in = jnp.where(sub == 0, h0, pltpu.roll(h_end, 1, axis=0))
    for g in range(ng):
        out = (h_loc[g] + a_cum[g] * c_in) * gg[8 * g:8 * g + 8, :]
        for j in range(nslab):
            o_ref[j, 8 * g:8 * g + 8, :] = out[:, 128 * j:128 * (j + 1)]


def _lru(zl, pv, wa, wx, *, tt):
    _, b, t, _ = zl.shape
    nslab = LRU_WIDTH // 128
    full = lambda bb, ti: (0, 0)
    return pl.pallas_call(
        functools.partial(_lru_kernel, tt=tt),
        grid=(b, t // tt),
        in_specs=[
            pl.BlockSpec((2 * nslab, None, tt, 128), lambda bb, ti: (0, bb, ti, 0)),
            pl.BlockSpec((8, LRU_WIDTH), full),
            pl.BlockSpec((LRU_WIDTH, LRU_WIDTH), full),
            pl.BlockSpec((LRU_WIDTH, LRU_WIDTH), full),
        ],
        out_specs=pl.BlockSpec((nslab, None, tt, 128), lambda bb, ti: (0, bb, ti, 0)),
        out_shape=jax.ShapeDtypeStruct((nslab, b, t, 128), F32),
        scratch_shapes=[pltpu.VMEM((CONV_WIDTH - 1, 8, LRU_WIDTH), F32), pltpu.VMEM((8, LRU_WIDTH), F32)],
        compiler_params=_cparams(("parallel", "arbitrary")),
        name="rglru",
    )(zl, pv, wa, wx)


def _mlstm_kernel(q_ref, k_ref, v_ref, og_ref, gc_ref, kt_ref, bcol_ref, brow_ref, nrm_ref, o_ref,
                  cn_ref, m_ref, *, tt):
    t = pl.program_id(1)

    @pl.when(t == 0)
    def _():
        cn_ref[...] = jnp.zeros_like(cn_ref)
        m_ref[...] = jnp.zeros_like(m_ref)

    L = CHUNK
    gr = kt_ref[MLSTM_WIDTH:MLSTM_WIDTH + 8, :] + bcol_ref[:, 0:1]
    br = _log_sigmoid(gr)
    pos = lax.broadcasted_iota(jnp.int32, (8, tt), 1) & (L - 1)
    d = 1
    while d < L:
        br = br + jnp.where(pos >= d, pltpu.roll(br, d, axis=1), 0.0)
        d *= 2
    bc = _log_sigmoid(gc_ref[...] + brow_ref[0:1, :])
    posc = lax.broadcasted_iota(jnp.int32, (tt, ZG_COLS), 0) & (L - 1)
    d = 1
    while d < L:
        bc = bc + jnp.where(posc >= d, pltpu.roll(bc, d, axis=0), 0.0)
        d *= 2

    nck = tt // L
    heads = range(MLSTM_HEADS)
    pairs = range(MLSTM_HEADS // 2)
    units = [(c, h) for c in range(nck) for h in heads]
    punits = [(c, p) for c in range(nck) for p in pairs]
    rsl = [slice(c * L, (c + 1) * L) for c in range(nck)]
    psl = [slice(128 * p, 128 * p + 128) for p in pairs]
    jsl = [slice(128 * (c // 2), 128 * (c // 2) + 128) for c in range(nck)]

    lane = lax.broadcasted_iota(jnp.int32, (L, 128), 1)
    rowi = lax.broadcasted_iota(jnp.int32, (L, 128), 0)
    half0 = lane < HEAD_DIM
    causal = (lane & (HEAD_DIM - 1)) <= rowi
    lane1 = lax.broadcasted_iota(jnp.int32, (1, 128), 1)
    rmask = [lane1 < L, lane1 >= L]
    r2 = lax.broadcasted_iota(jnp.int32, (2 * L, 128), 0)
    l2 = lax.broadcasted_iota(jnp.int32, (2 * L, 128), 1)
    top = r2 < L
    diag = (r2 < L) == (l2 < HEAD_DIM)
    ones_bd = jnp.where(diag, 1.0, 0.0).astype(BF16)

    b_row = {(c, h): br[4 + h:5 + h, jsl[c]] for c, h in units}
    i_row = {(c, h): gr[h:h + 1, jsl[c]] for c, h in units}

    g = {(c, h): br[4 + h:5 + h, (c + 1) * L - 1:(c + 1) * L] for c, h in units}
    w_log = {u: jnp.where(rmask[u[0] % 2], g[u] - b_row[u] + i_row[u], -jnp.inf) for u in units}
    w_max = {u: jnp.max(w_log[u], axis=1, keepdims=True) for u in units}
    m_in, m_out = {}, {}
    for h in heads:
        m_st = m_ref[h, 0:1, 0:1]
        for c in range(nck):
            m_in[c, h] = m_st
            m_st = jnp.maximum(g[c, h] + m_st, w_max[c, h])
            m_out[c, h] = m_st
        m_ref[h] = jnp.broadcast_to(m_st, (8, 128))
    dec = {u: jnp.exp(g[u] + m_in[u] - m_out[u]) for u in units}
    wk = {u: jnp.exp(w_log[u] - m_out[u]) for u in units}

    def lanes_of_chunk(row128, c, e):
        return row128 if c % 2 == e else pltpu.roll(row128, HEAD_DIM, axis=1)

    def pair_row(d, c, p):
        return jnp.where(rmask[0], lanes_of_chunk(d[c, 2 * p], c, 0), lanes_of_chunk(d[c, 2 * p + 1], c, 1))

    def pair_tile(d, c, p):
        return jnp.where(half0, d[c, 2 * p], d[c, 2 * p + 1])

    q_p = {(c, p): (q_ref[rsl[c], psl[p]] * (HEAD_DIM ** -0.5)).astype(BF16) for c, p in punits}
    k_bd = {}
    v_bd = {}
    for c, p in punits:
        k_c = k_ref[rsl[c], psl[p]]
        v_c = v_ref[rsl[c], psl[p]]
        k_bd[c, p] = jnp.where(diag, jnp.concatenate([k_c, k_c], axis=0), 0.0).astype(BF16)
        v_bd[c, p] = jnp.concatenate([jnp.where(diag, jnp.concatenate([v_c, v_c], axis=0), 0.0).astype(BF16),
                                      ones_bd], axis=1)
    bcb = {(c, p): jnp.where(half0, jnp.broadcast_to(bc[rsl[c], 4 + 2 * p:5 + 2 * p], (L, 128)),
                             jnp.broadcast_to(bc[rsl[c], 5 + 2 * p:6 + 2 * p], (L, 128))) for c, p in punits}
    dmat = {(c, p): jnp.where(causal, bcb[c, p] - pair_row(b_row, c, p) + pair_row(i_row, c, p), -jnp.inf)
            for c, p in punits}
    m_loc = {u: jnp.where(half0, jnp.max(jnp.where(half0, dmat[u], -jnp.inf), axis=1, keepdims=True),
                          jnp.max(jnp.where(half0, -jnp.inf, dmat[u]), axis=1, keepdims=True)) for u in punits}
    p_in = {u: jnp.exp(dmat[u] - m_loc[u]) for u in punits}
    qk = {u: _dot_nt(q_p[u], k_bd[u]) for u in punits}
    nd = {u: _dot(p_in[u] * qk[u], v_bd[u]) for u in punits}
    kv = {}
    for c, p in punits:
        kw = kt_ref[psl[p], jsl[c]] * jnp.where(top, wk[c, 2 * p], wk[c, 2 * p + 1])
        vv = jnp.concatenate([v_ref[jsl[c], psl[p]].astype(BF16), jnp.ones((2 * L, 128), BF16)], axis=1)
        kv[c, p] = jnp.where(jnp.concatenate([diag, diag], axis=1), _dot(kw, vv), 0.0)

    cn_in = {}
    for p in pairs:
        cn = cn_ref[p]
        for c in range(nck):
            cn_in[c, p] = cn.astype(BF16)
            cn = jnp.concatenate([dec[c, 2 * p] * cn[0:L], dec[c, 2 * p + 1] * cn[L:2 * L]], axis=0) + kv[c, p]
        cn_ref[p] = cn

    inter = {(c, p): bcb[c, p] + pair_tile(m_in, c, p) for c, p in punits}
    mj = {u: jnp.maximum(m_loc[u], inter[u]) for u in punits}
    e_loc = {u: jnp.exp(m_loc[u] - mj[u]) for u in punits}
    e_int = {u: jnp.exp(inter[u] - mj[u]) for u in punits}
    e_neg = {u: jnp.exp(-mj[u]) for u in punits}
    qc = {u: _dot(q_p[u], cn_in[u]) for u in punits}
    num = {u: e_loc[u] * nd[u][:, 0:128] + e_int[u] * qc[u][:, 0:128] for u in punits}
    den = {u: e_loc[u] * nd[u][:, 128:256] + e_int[u] * qc[u][:, 128:256] for u in punits}
    hv = {u: num[u] / jnp.maximum(jnp.abs(den[u]), e_neg[u]) for u in punits}
    hv2 = {u: hv[u] * hv[u] for u in punits}
    ms = {u: jnp.where(half0, jnp.sum(jnp.where(half0, hv2[u], 0.0), axis=1, keepdims=True),
                       jnp.sum(jnp.where(half0, 0.0, hv2[u]), axis=1, keepdims=True)) * (1.0 / HEAD_DIM)
          for u in punits}
    for c, p in punits:
        hn = hv[c, p] * lax.rsqrt(ms[c, p] + RMS_EPS)
        o_ref[rsl[c], psl[p]] = hn * nrm_ref[0:1, psl[p]] * jax.nn.sigmoid(og_ref[rsl[c], psl[p]])


def _mlstm(zm, zg, ktg, bcol, brow, nrm, *, tt):
    b, t, _ = zm.shape
    nt = t // tt
    colblk = lambda j: pl.BlockSpec((None, tt, MLSTM_WIDTH), lambda bb, ti, j=j: (bb, ti, j))
    full = lambda bb, ti: (0, 0)
    return pl.pallas_call(
        functools.partial(_mlstm_kernel, tt=tt),
        grid=(b, nt),
        in_specs=[
            colblk(0), colblk(1), colblk(2), colblk(3),
            pl.BlockSpec((None, tt, ZG_COLS), lambda bb, ti: (bb, ti, 0)),
            pl.BlockSpec((KT_ROWS, tt), lambda bb, ti: (0, bb * nt + ti)),
            pl.BlockSpec((8, 128), full),
            pl.BlockSpec((8, 128), full),
            pl.BlockSpec((8, MLSTM_WIDTH), full),
        ],
        out_specs=pl.BlockSpec((None, tt, MLSTM_WIDTH), lambda bb, ti: (bb, ti, 0)),
        out_shape=jax.ShapeDtypeStruct((b, t, MLSTM_WIDTH), F32),
        scratch_shapes=[pltpu.VMEM((MLSTM_HEADS // 2, 2 * HEAD_DIM, 256), F32),
                        pltpu.VMEM((MLSTM_HEADS, 8, 128), F32)],
        compiler_params=_cparams(("parallel", "arbitrary")),
        name="mlstm",
    )(zm, zm, zm, zm, zg, ktg, bcol, brow, nrm)


def _interleave(*gens):
    gens = list(gens)
    while gens:
        for gen in list(gens):
            try:
                next(gen)
            except StopIteration:
                gens.remove(gen)


def _rwkv_kernel(z_ref, mu_ref, pv_ref, wup_ref, aup_ref, gup_ref, seg_ref, o_ref,
                 prev_ref, st_ref, y_ref, *, tt, sb):
    t = pl.program_id(1)

    @pl.when(t == 0)
    def _():
        prev_ref[...] = jnp.zeros_like(prev_ref)
        st_ref[...] = jnp.zeros_like(st_ref)

    L = CHUNK
    W = RWKV_WIDTH
    UR = 2 * L
    SB = sb
    nrb = tt // SB
    nsub = SB // UR
    pv = pv_ref[...]
    w0, a0, k_k, k_a, r_k, ln_w, ln_b = (pv[j:j + 1, :] for j in range(7))
    seg = seg_ref[...]
    mu = mu_ref[...]
    heads = range(RWKV_HEADS)
    hsl = [slice(HEAD_DIM * h, HEAD_DIM * (h + 1)) for h in heads]

    ri = lax.broadcasted_iota(jnp.int32, (UR, UR), 0)
    ci = lax.broadcasted_iota(jnp.int32, (UR, UR), 1)
    same = (ri // L) == (ci // L)
    strict = jnp.where(same & (ri > ci), 1.0, 0.0)
    incl = jnp.where(same & (ri >= ci), 1.0, 0.0)
    eye = jnp.where(ri == ci, 1.0, 0.0)
    mask4 = jnp.concatenate([jnp.concatenate([strict, strict], axis=1),
                             jnp.concatenate([incl, incl], axis=1)], axis=0)
    posr = lax.broadcasted_iota(jnp.int32, (SB, W), 0) & (L - 1)
    row1 = lax.broadcasted_iota(jnp.int32, (SB, 1), 0)
    units = [(h, u) for u in range(nsub) for h in heads]
    nu = range(len(units))
    usl = [slice(UR * u, UR * (u + 1)) for _, u in units]

    prep, mats = {}, {}
    zst = [st_ref[h] for h in heads]

    def prepare(rb):
        rows = slice(SB * rb, SB * (rb + 1))
        z = z_ref[rows, :]
        first = prev_ref[0:1, :] if rb == 0 else z_ref[SB * rb - 1:SB * rb, :]
        zprev = jnp.where(row1 == 0, first, pltpu.roll(z, 1, axis=0))
        yield
        zs = z + (zprev - z) * mu
        yield
        r = zs[:, 0:W]
        k = zs[:, W:2 * W]
        v = zs[:, 2 * W:3 * W]
        wd = zs[:, 3 * W:3 * W + 64]
        ad = zs[:, 3 * W + 64:3 * W + 128]
        gd = zs[:, 3 * W + 128:3 * W + 256]
        log_w = -_softplus(-(w0 + _dot(jnp.tanh(wd), wup_ref[...]))) - 0.5
        yield
        ld = -jnp.exp(log_w)
        yield
        iclr = jax.nn.sigmoid(a0 + _dot(ad, aup_ref[...]))
        yield
        g = _dot(jax.nn.sigmoid(gd), gup_ref[...])
        yield
        kk = k * k_k
        ss = _segsum(kk * kk, seg)
        yield
        kk = kk / jnp.maximum(jnp.sqrt(ss), 1e-12)
        yield
        k_mod = k * (1.0 + (iclr - 1.0) * k_a)
        a_vec = -kk
        b_vec = kk * iclr
        yield
        cum = ld
        d = 1
        while d < L:
            cum = cum + jnp.where(posr >= d, pltpu.roll(cum, d, axis=0), 0.0)
            d *= 2
            yield
        cum_last = jnp.concatenate(
            [jnp.broadcast_to(cum[(c + 1) * L - 1:(c + 1) * L, :], (L, W)) for c in range(SB // L)], axis=0)
        e_neg = jnp.exp(-cum)
        yield
        e_tail = jnp.exp(cum_last - cum)
        yield
        a_t = jnp.exp(cum - ld) * a_vec
        yield
        r_t = jnp.exp(cum) * r
        yield
        b_t = e_neg * b_vec
        k_t = e_neg * k_mod
        yield
        b_h = e_tail * b_vec
        k_h = e_tail * k_mod
        yield
        bh_t, kh_t, cum_t = [], [], []
        for pair in range(RWKV_HEADS // 2):
            ps = slice(128 * pair, 128 * pair + 128)
            bp, kp, cp = b_h[:, ps].T.astype(BF16), k_h[:, ps].T.astype(BF16), cum[:, ps].T
            for e in range(2):
                es = slice(HEAD_DIM * e, HEAD_DIM * (e + 1))
                bh_t.append(bp[es])
                kh_t.append(kp[es])
                cum_t.append(cp[es])
            yield
        prep[rb] = dict(
            at=[a_t[usl[i], hsl[h]].astype(BF16) for i, (h, _) in enumerate(units)],
            rt=[r_t[usl[i], hsl[h]] for i, (h, _) in enumerate(units)],
            bt=[b_t[usl[i], hsl[h]].astype(BF16) for i, (h, _) in enumerate(units)],
            kt=[k_t[usl[i], hsl[h]].astype(BF16) for i, (h, _) in enumerate(units)],
            vu=[v[usl[i], hsl[h]].astype(BF16) for i, (h, _) in enumerate(units)],
            bh_t=bh_t, kh_t=kh_t, cum_t=cum_t,
            bonus_in=r * k_mod * r_k, v=v, g=g)
        yield

    def products(rb):
        p = prep[rb]
        at, rt, bt, kt, vu = p["at"], p["rt"], p["bt"], p["kt"], p["vu"]
        prod = [mask4 * _dot_nt(jnp.concatenate([at[i], rt[i].astype(BF16)], axis=0),
                                jnp.concatenate([bt[i], kt[i]], axis=0)) for i in nu]
        yield
        a_ab = [prod[i][0:UR, 0:UR] for i in nu]
        a_ak = [prod[i][0:UR, UR:2 * UR].astype(BF16) for i in nu]
        a_rbk = [prod[i][UR:2 * UR, :].astype(BF16) for i in nu]
        yield
        tm = [eye + a_ab[i] for i in nu]
        pw = [a_ab[i].astype(BF16) for i in nu]
        yield
        pw = [_dot(pw[i], pw[i]).astype(BF16) for i in nu]
        yield
        for _ in range(4):
            res = [_dot(pw[i], jnp.concatenate([pw[i], tm[i].astype(BF16)], axis=1)) for i in nu]
            yield
            pw = [res[i][:, 0:UR].astype(BF16) for i in nu]
            tm = [tm[i] + res[i][:, UR:2 * UR] for i in nu]
            yield
        tm = [tm[i] + _dot(pw[i], tm[i]) for i in nu]
        yield
        akv = [_dot(a_ak[i], vu[i]) for i in nu]
        yield
        tmb = [tm[i].astype(BF16) for i in nu]
        w_all = [_dot(tmb[i], at[i]).astype(BF16) for i in nu]
        yield
        u0_all = [_dot(tmb[i], akv[i]).astype(BF16) for i in nu]
        yield
        q_all = [(rt[i] + _dot(a_rbk[i][:, 0:UR], w_all[i])).astype(BF16) for i in nu]
        yield
        y0_all = [_dot(a_rbk[i], jnp.concatenate([u0_all[i], vu[i]], axis=0)) for i in nu]
        yield
        qc, hm, pcol, y0c = {}, {}, {}, {}
        for i, (h, u) in enumerate(units):
            for cc in range(UR // L):
                c = u * (UR // L) + cc
                ls = slice(cc * L, (cc + 1) * L)
                rs = slice(c * L, (c + 1) * L)
                gm = _dot(p["bh_t"][h][:, rs], w_all[i][ls]).astype(BF16)
                hm[c, h] = _dot(p["bh_t"][h][:, rs], u0_all[i][ls]) + _dot(p["kh_t"][h][:, rs], vu[i][ls])
                pcol[c, h] = jnp.exp(p["cum_t"][h][:, (c + 1) * L - 1:(c + 1) * L])
                qc[c, h] = jnp.concatenate([q_all[i][ls], gm], axis=0)
                y0c[c, h] = y0_all[i][ls]
            if h == RWKV_HEADS - 1:
                yield
        mats[rb] = dict(qc=qc, hm=hm, pcol=pcol, y0c=y0c)
        yield

    def finish(rb):
        m, p = mats[rb], prep[rb]
        for cc in range(SB // L):
            rs = slice(SB * rb + cc * L, SB * rb + (cc + 1) * L)
            for h in heads:
                qz = _dot(m["qc"][cc, h], zst[h])
                y_ref[rs, hsl[h]] = qz[0:L] + m["y0c"][cc, h]
                zst[h] = m["pcol"][cc, h] * zst[h] + qz[L:2 * L] + m["hm"][cc, h]
                yield
        rows = slice(SB * rb, SB * (rb + 1))
        y = y_ref[rows, :]
        mean = _segsum(y, seg) * (1.0 / HEAD_DIM)
        yield
        yc = y - mean
        var = _segsum(yc * yc, seg) * (1.0 / HEAD_DIM)
        yield
        yn = yc * lax.rsqrt(var + GN_EPS) * ln_w + ln_b
        yield
        bonus = _segsum(p["bonus_in"], seg) * p["v"]
        yield
        o_ref[rows, :] = (yn + bonus) * p["g"]
        yield

    _interleave(prepare(0))
    for step in range(1, nrb + 2):
        stage = []
        if step < nrb:
            stage.append(prepare(step))
        if 0 <= step - 1 < nrb:
            stage.append(products(step - 1))
        if 0 <= step - 2 < nrb:
            stage.append(finish(step - 2))
        _interleave(*stage)
    for h in heads:
        st_ref[h] = zst[h]
    prev_ref[...] = jnp.broadcast_to(z_ref[tt - 1:tt, :], prev_ref.shape)


def _rwkv(zr, mu, pv, wup, aup, gup, seg, *, tt, sb):
    b, t, _ = zr.shape
    full = lambda bb, ti: (0, 0)
    return pl.pallas_call(
        functools.partial(_rwkv_kernel, tt=tt, sb=sb),
        grid=(b, t // tt),
        in_specs=[
            pl.BlockSpec((None, tt, N_RWKV_COLS), lambda bb, ti: (bb, ti, 0)),
            pl.BlockSpec((1, N_RWKV_COLS), full),
            pl.BlockSpec((8, RWKV_WIDTH), full),
            pl.BlockSpec((64, RWKV_WIDTH), full),
            pl.BlockSpec((64, RWKV_WIDTH), full),
            pl.BlockSpec((128, RWKV_WIDTH), full),
            pl.BlockSpec((4 * HEAD_DIM, 4 * HEAD_DIM), full),
        ],
        out_specs=pl.BlockSpec((None, tt, RWKV_WIDTH), lambda bb, ti: (bb, ti, 0)),
        out_shape=jax.ShapeDtypeStruct((b, t, RWKV_WIDTH), F32),
        scratch_shapes=[
            pltpu.VMEM((8, N_RWKV_COLS), F32),
            pltpu.VMEM((RWKV_HEADS, HEAD_DIM, HEAD_DIM), F32),
            pltpu.VMEM((tt, RWKV_WIDTH), F32),
        ],
        compiler_params=_cparams(("parallel", "arbitrary")),
        name="rwkv7",
    )(zr, mu, pv, wup, aup, gup, seg)


def _tail_kernel(x_ref, ya_ref, yb_ref, yc_ref, p_ref, g_ref, wm_ref, wi_ref, wo_ref, wg_ref, wp_ref, o_ref,
                 h_ref, *, tf):
    c0, c1 = LRU_WIDTH, LRU_WIDTH + MLSTM_WIDTH
    tm = x_ref.shape[0]
    ya = jnp.concatenate(
        [jnp.concatenate([ya_ref[j, _perm_rows(m, tm), :] for j in range(LRU_WIDTH // 128)], axis=1)
         for m in range(tm // 8)], axis=0)
    mix = (jnp.dot(ya.astype(BF16), wm_ref[0:c0, :], preferred_element_type=F32)
           + jnp.dot(yb_ref[...].astype(BF16), wm_ref[c0:c1, :], preferred_element_type=F32)
           + jnp.dot(yc_ref[...].astype(BF16), wm_ref[c1:, :], preferred_element_type=F32))
    x = x_ref[...] + _rms(mix, g_ref[3:4, :])
    xn = _rms(x, g_ref[4:5, :]).astype(BF16)
    for c in range(D_FF // tf):
        gate = jnp.dot(xn, wi_ref[:, c * tf:(c + 1) * tf], preferred_element_type=F32)
        up = jnp.dot(xn, wi_ref[:, D_FF + c * tf:D_FF + (c + 1) * tf], preferred_element_type=F32)
        h_ref[:, c * tf:(c + 1) * tf] = (gate * jax.nn.sigmoid(gate) * up).astype(BF16)
    y = jnp.dot(h_ref[...], wo_ref[...], preferred_element_type=F32)
    x = x + 0.5 * _rms(y, g_ref[5:6, :])
    pgate = jax.nn.sigmoid(jnp.dot(_rms(x, g_ref[6:7, :]).astype(BF16), wg_ref[...], preferred_element_type=F32))
    pe = jnp.dot(p_ref[...].astype(BF16), wp_ref[...], preferred_element_type=F32)
    o_ref[...] = x + _rms(pgate * pe, g_ref[7:8, :])


def _tail(x, ya, yb, yc, p, g8, w_mix, w_in, w_out, w_gate, w_proj, *, tm, tf):
    n = x.shape[0]
    row = lambda i: (i, 0)
    const = lambda shape: pl.BlockSpec(shape, lambda i: (0, 0), pipeline_mode=pl.Buffered(1))
    return pl.pallas_call(
        functools.partial(_tail_kernel, tf=tf),
        grid=(n // tm,),
        in_specs=[
            pl.BlockSpec((tm, D_MODEL), row),
            pl.BlockSpec((LRU_WIDTH // 128, tm, 128), lambda i: (0, i, 0)),
            pl.BlockSpec((tm, MLSTM_WIDTH), row),
            pl.BlockSpec((tm, RWKV_WIDTH), row),
            pl.BlockSpec((tm, PLE_DIM), row),
            const((8, D_MODEL)),
            const((D_MODEL, D_MODEL)),
            const((D_MODEL, 2 * D_FF)),
            const((D_FF, D_MODEL)),
            const((D_MODEL, D_MODEL)),
            const((PLE_DIM, D_MODEL)),
        ],
        out_specs=pl.BlockSpec((tm, D_MODEL), row),
        out_shape=jax.ShapeDtypeStruct((n, D_MODEL), F32),
        scratch_shapes=[pltpu.VMEM((tm, D_FF), BF16)],
        compiler_params=_cparams(("parallel",)),
        name="tail",
    )(x, ya, yb, yc, p, g8, w_mix, w_in, w_out, w_gate, w_proj)


def _block_diag(w):
    nh, dd, _ = w.shape
    eye = jnp.eye(nh, dtype=w.dtype)
    return (eye[:, None, :, None] * w[:, :, None, :]).reshape(nh * dd, nh * dd)


def _tile(n, pref):
    return pref if n % pref == 0 else n


def kernel(x, p, norm_g, ffn_w_in, ffn_w_out, w_in, w_out, lru_conv_w, lru_conv_b, lru_w_a, lru_b_a, lru_w_x, lru_b_x, lru_lambda, m_b_i, m_b_f, m_norm, rw_mu, rw_w0, rw_w_up, rw_a0, rw_a_up, rw_g_up, rw_k_k, rw_k_a, rw_r_k, rw_ln_w, rw_ln_b, ple_w_proj, ple_w_gate):
    bsz, t, _ = x.shape
    n = bsz * t
    depth = norm_g.shape[0]
    tm_ffn = _tile(n, 512)
    tm = _tile(n, 512)
    tt_lru = _tile(t, 512)
    tt_m = _tile(t, 1024)
    tt_r = _tile(t, 1024)
    seg = _block_diag(jnp.ones((4, HEAD_DIM, HEAD_DIM), BF16))

    xf = x.reshape(n, D_MODEL)
    for l in range(depth):
        g = norm_g[l]
        wl = w_in[l]
        o_zr = 2 * LRU_WIDTH + 4 * MLSTM_WIDTH + 2 * MLSTM_HEADS
        w_gates = wl[:, o_zr - 2 * MLSTM_HEADS:o_zr]
        wcat = jnp.concatenate(
            [wl[:, o_zr:], jnp.pad(w_gates, ((0, 0), (0, ZG_COLS - 2 * MLSTM_HEADS))),
             wl[:, 0:ZL_COLS], wl[:, ZL_COLS:ZL_COLS + ZM_COLS]], axis=1).astype(BF16)

        xf = _ffn(xf, g[0:2], ffn_w_in[l, 0].astype(BF16), ffn_w_out[l, 0].astype(BF16), tm=tm_ffn, tf=256)

        zr, zl, zm, zg, ktg = _mixer_in(xf, g[2:3], wcat, tm=tm)

        lru_pv = jnp.concatenate([lru_conv_w[l], lru_conv_b[l][None], lru_b_a[l][None], lru_b_x[l][None],
                                  lru_lambda[l][None]], axis=0)
        ya = _lru(zl.reshape(ZL_COLS // 128, bsz, t, 128), lru_pv, _block_diag(lru_w_a[l]).astype(BF16),
                  _block_diag(lru_w_x[l]).astype(BF16), tt=tt_lru)

        gate_b = jnp.concatenate([m_b_i[l], m_b_f[l]])
        bcol = jnp.broadcast_to(gate_b[:, None], (8, 128))
        brow = jnp.broadcast_to(jnp.pad(gate_b, (0, ZG_COLS - 8))[None, :], (8, ZG_COLS))
        nrm = jnp.broadcast_to(m_norm[l][None, :], (8, MLSTM_WIDTH))
        yb = _mlstm(zm.reshape(bsz, t, ZM_COLS), zg.reshape(bsz, t, ZG_COLS), ktg, bcol, brow, nrm, tt=tt_m)

        rw_pv = jnp.stack([rw_w0[l], rw_a0[l], rw_k_k[l], rw_k_a[l], rw_r_k[l].reshape(-1), rw_ln_w[l],
                           rw_ln_b[l], jnp.zeros((RWKV_WIDTH,), F32)], axis=0)
        yc = _rwkv(zr.reshape(bsz, t, ZR_COLS), rw_mu[l][None, :], rw_pv, rw_w_up[l].astype(BF16),
                   rw_a_up[l].astype(BF16), rw_g_up[l].astype(BF16), seg, tt=tt_r, sb=min(tt_r, 256))

        xf = _tail(xf, ya.reshape(LRU_WIDTH // 128, n, 128), yb.reshape(n, MLSTM_WIDTH), yc.reshape(n, RWKV_WIDTH),
                   p[l].reshape(n, PLE_DIM), g, w_out[l].astype(BF16), ffn_w_in[l, 1].astype(BF16),
                   ffn_w_out[l, 1].astype(BF16), ple_w_gate[l].astype(BF16), ple_w_proj[l].astype(BF16),
                   tm=tm_ffn, tf=256)
    return xf.reshape(bsz, t, D_MODEL)
```

```python
import functools

import jax
import jax.numpy as jnp
from jax import lax
from jax.experimental import pallas as pl
from jax.experimental.pallas import tpu as pltpu

F32 = jnp.float32
BF16 = jnp.bfloat16

D_MODEL = 1024
PLE_DIM = 256
D_FF = 2816
RMS_EPS = 1e-6
LRU_WIDTH = 384
LRU_HEADS = 6
CONV_WIDTH = 4
LRU_C = 8.0
MLSTM_HEADS = 4
MLSTM_WIDTH = 256
RWKV_HEADS = 6
RWKV_WIDTH = 384
N_RWKV_COLS = 1408
GN_EPS = 64e-5
HEAD_DIM = 64
CHUNK = 64

ZR_COLS = N_RWKV_COLS
ZL_COLS = 2 * LRU_WIDTH
ZM_COLS = 4 * MLSTM_WIDTH
ZG_COLS = 128
KT_ROWS = MLSTM_WIDTH + 8

VMEM_LIMIT = 56 * 1024 * 1024


def _cparams(sem):
    return pltpu.CompilerParams(dimension_semantics=sem, vmem_limit_bytes=VMEM_LIMIT)


def _rms(x, g):
    return x * lax.rsqrt(jnp.mean(x * x, axis=-1, keepdims=True) + RMS_EPS) * g


def _softplus(x):
    return jnp.maximum(x, 0.0) + jnp.log(1.0 + jnp.exp(-jnp.abs(x)))


def _log_sigmoid(x):
    return -_softplus(-x)


def _dot(a, b):
    return jnp.dot(a.astype(BF16), b.astype(BF16), preferred_element_type=F32)


def _dot_nt(a, b):
    return lax.dot_general(a.astype(BF16), b.astype(BF16), (((1,), (1,)), ((), ())),
                           preferred_element_type=F32)


def _interleave(*gens, skew=0):
    pending, active, rounds = list(gens), [], 0
    while pending or active:
        if pending and skew == 0:
            active, pending = active + pending, []
        elif pending and rounds % skew == 0:
            active.append(pending.pop(0))
        for gen in list(active):
            try:
                next(gen)
            except StopIteration:
                active.remove(gen)
        rounds += 1


def _perm_rows(m, tile):
    ng = tile // 8
    s, g0 = divmod(8 * m, ng)
    return pl.ds(8 * g0 + s, 8, stride=8)


def _segsum(x, seg):
    hi = x.astype(BF16)
    lo = (x - hi.astype(F32)).astype(BF16)
    w = seg.shape[0]
    head = (jnp.dot(hi[:, 0:w], seg, preferred_element_type=F32)
            + jnp.dot(lo[:, 0:w], seg, preferred_element_type=F32))
    tail = jnp.dot(jnp.concatenate([hi[:, w:], lo[:, w:]], axis=1), seg, preferred_element_type=F32)
    return jnp.concatenate([head, tail[:, 0:w // 2] + tail[:, w // 2:]], axis=1)


def _ffn_kernel(x_ref, g_ref, wi_ref, wo_ref, o_ref, h_ref, *, tf):
    x = x_ref[...]
    xn = _rms(x, g_ref[0:1, :]).astype(BF16)
    for c in range(D_FF // tf):
        gate = jnp.dot(xn, wi_ref[:, c * tf:(c + 1) * tf], preferred_element_type=F32)
        up = jnp.dot(xn, wi_ref[:, D_FF + c * tf:D_FF + (c + 1) * tf], preferred_element_type=F32)
        h_ref[:, c * tf:(c + 1) * tf] = (gate * jax.nn.sigmoid(gate) * up).astype(BF16)
    y = jnp.dot(h_ref[...], wo_ref[...], preferred_element_type=F32)
    o_ref[...] = x + 0.5 * _rms(y, g_ref[1:2, :])


def _ffn(x, g2, w_in, w_out, *, tm, tf):
    n = x.shape[0]
    const = dict(pipeline_mode=pl.Buffered(1))
    return pl.pallas_call(
        functools.partial(_ffn_kernel, tf=tf),
        grid=(n // tm,),
        in_specs=[
            pl.BlockSpec((tm, D_MODEL), lambda i: (i, 0)),
            pl.BlockSpec((2, D_MODEL), lambda i: (0, 0)),
            pl.BlockSpec((D_MODEL, 2 * D_FF), lambda i: (0, 0), **const),
            pl.BlockSpec((D_FF, D_MODEL), lambda i: (0, 0), **const),
        ],
        out_specs=pl.BlockSpec((tm, D_MODEL), lambda i: (i, 0)),
        out_shape=jax.ShapeDtypeStruct((n, D_MODEL), F32),
        scratch_shapes=[pltpu.VMEM((tm, D_FF), BF16)],
        compiler_params=_cparams(("parallel",)),
        name="ffn",
    )(x, g2, w_in, w_out)


def _mixin_kernel(x_ref, g_ref, w_ref, zr_ref, zl_ref, zm_ref, zg_ref, kt_ref):
    xn = _rms(x_ref[...], g_ref[...]).astype(BF16)
    c0, c1, c2 = ZR_COLS + ZG_COLS, ZR_COLS + ZG_COLS + ZL_COLS, ZR_COLS + ZG_COLS + ZL_COLS + ZM_COLS
    zrg = jnp.dot(xn, w_ref[:, 0:c0], preferred_element_type=F32)
    zg = zrg[:, ZR_COLS:c0]
    zr_ref[...] = zrg[:, 0:ZR_COLS]
    zg_ref[...] = zg
    zl = jnp.dot(xn, w_ref[:, c0:c1], preferred_element_type=F32)
    for m in range(zl.shape[0] // 8):
        rows = _perm_rows(m, zl.shape[0])
        for j in range(ZL_COLS // 128):
            zl_ref[j, rows, :] = zl[8 * m:8 * m + 8, 128 * j:128 * (j + 1)]
    zm = jnp.dot(xn, w_ref[:, c1:c2], preferred_element_type=F32)
    zm_ref[...] = zm
    kt_ref[0:MLSTM_WIDTH, :] = zm[:, MLSTM_WIDTH:2 * MLSTM_WIDTH].T
    kt_ref[MLSTM_WIDTH:KT_ROWS, :] = zg.T[0:KT_ROWS - MLSTM_WIDTH, :]


def _mixer_in(x, g, wcat, *, tm):
    n = x.shape[0]
    ncols = wcat.shape[1]
    row = lambda i: (i, 0)
    return pl.pallas_call(
        _mixin_kernel,
        grid=(n // tm,),
        in_specs=[
            pl.BlockSpec((tm, D_MODEL), row),
            pl.BlockSpec((1, D_MODEL), lambda i: (0, 0)),
            pl.BlockSpec((D_MODEL, ncols), lambda i: (0, 0)),
        ],
        out_specs=[
            pl.BlockSpec((tm, ZR_COLS), row),
            pl.BlockSpec((ZL_COLS // 128, tm, 128), lambda i: (0, i, 0)),
            pl.BlockSpec((tm, ZM_COLS), row),
            pl.BlockSpec((tm, ZG_COLS), row),
            pl.BlockSpec((KT_ROWS, tm), lambda i: (0, i)),
        ],
        out_shape=[
            jax.ShapeDtypeStruct((n, ZR_COLS), F32),
            jax.ShapeDtypeStruct((ZL_COLS // 128, n, 128), F32),
            jax.ShapeDtypeStruct((n, ZM_COLS), F32),
            jax.ShapeDtypeStruct((n, ZG_COLS), F32),
            jax.ShapeDtypeStruct((KT_ROWS, n), F32),
        ],
        compiler_params=_cparams(("parallel",)),
        name="mixer_in",
    )(x, g, wcat)


def _lru_kernel(z_ref, pv_ref, wa_ref, wx_ref, o_ref, xtail_ref, h_ref, *, tt):
    t = pl.program_id(1)

    @pl.when(t == 0)
    def _():
        xtail_ref[...] = jnp.zeros_like(xtail_ref)
        h_ref[...] = jnp.zeros_like(h_ref)

    ng = tt // 8
    nslab = LRU_WIDTH // 128
    taps = CONV_WIDTH - 1
    sub = lax.broadcasted_iota(jnp.int32, (8, LRU_WIDTH), 0)

    def load(first_slab, g):
        return jnp.concatenate([z_ref[first_slab + j, 8 * g:8 * g + 8, :] for j in range(nslab)], axis=1)

    x = [load(0, g) for g in range(ng)]
    pv = pv_ref[...]

    def before(g):
        if g >= 0:
            return x[g]
        cur = pltpu.roll(x[ng + g], 1, axis=0)
        prev = pltpu.roll(xtail_ref[taps + g], 1, axis=0)
        return jnp.where(sub == 0, prev, cur)

    nhalf = 2
    parts = {}

    def front(i):
        pieces = range(ng // nhalf * i, ng // nhalf * (i + 1))
        xa = []
        for g in pieces:
            acc = pv[taps + 1:taps + 2, :] + pv[taps:taps + 1, :] * x[g]
            for d in range(1, taps + 1):
                acc = acc + pv[taps - d:taps - d + 1, :] * before(g - d)
            xa.append(acc)
        xa = jnp.concatenate(xa, axis=0)
        yield
        r = jax.nn.sigmoid(_dot(xa, wa_ref[...]) + pv[5:6, :])
        i_gate = jax.nn.sigmoid(_dot(xa, wx_ref[...]) + pv[6:7, :])
        yield
        log_a = (-LRU_C) * r * _softplus(-pv[7:8, :])
        a_i = jnp.exp(log_a)
        yield
        u_i = jnp.sqrt(1.0 - jnp.exp(2.0 * log_a)) * (i_gate * xa)
        yield
        gate = jnp.concatenate([load(nslab, g) for g in pieces], axis=0)
        parts[i] = (a_i, u_i, jax.nn.gelu(gate, approximate=True))
        yield

    _interleave(*[front(i) for i in range(nhalf)], skew=1)
    for d in range(taps):
        xtail_ref[d] = x[ng - taps + d]
    a = jnp.concatenate([parts[i][0] for i in range(nhalf)], axis=0)
    u = jnp.concatenate([parts[i][1] for i in range(nhalf)], axis=0)
    gg = jnp.concatenate([parts[i][2] for i in range(nhalf)], axis=0)

    h_loc, a_cum = [], []
    h = u[0:8, :]
    ac = a[0:8, :]
    h_loc.append(h)
    a_cum.append(ac)
    for g in range(1, ng):
        a_g = a[8 * g:8 * g + 8, :]
        h = a_g * h + u[8 * g:8 * g + 8, :]
        ac = a_g * ac
        h_loc.append(h)
        a_cum.append(ac)
    ae, he = ac, h
    for d in (1, 2, 4):
        keep = sub >= d
        a_sh = jnp.where(keep, pltpu.roll(ae, d, axis=0), 1.0)
        h_sh = jnp.where(keep, pltpu.roll(he, d, axis=0), 0.0)
        he = ae * h_sh + he
        ae = ae * a_sh
    h0 = h_ref[0:1, :]
    h_end = he + ae * h0
    h_ref[...] = jnp.broadcast_to(h_end[7:8, :], h_ref.shape)
    c_in = jnp.where(sub == 0, h0, pltpu.roll(h_end, 1, axis=0))
    for g in range(ng):
        out = (h_loc[g] + a_cum[g] * c_in) * gg[8 * g:8 * g + 8, :]
        for j in range(nslab):
            o_ref[j, 8 * g:8 * g + 8, :] = out[:, 128 * j:128 * (j + 1)]


def _lru(zl, pv, wa, wx, *, tt):
    _, b, t, _ = zl.shape
    nslab = LRU_WIDTH // 128
    full = lambda bb, ti: (0, 0)
    return pl.pallas_call(
        functools.partial(_lru_kernel, tt=tt),
        grid=(b, t // tt),
        in_specs=[
            pl.BlockSpec((2 * nslab, None, tt, 128), lambda bb, ti: (0, bb, ti, 0)),
            pl.BlockSpec((8, LRU_WIDTH), full),
            pl.BlockSpec((LRU_WIDTH, LRU_WIDTH), full),
            pl.BlockSpec((LRU_WIDTH, LRU_WIDTH), full),
        ],
        out_specs=pl.BlockSpec((nslab, None, tt, 128), lambda bb, ti: (0, bb, ti, 0)),
        out_shape=jax.ShapeDtypeStruct((nslab, b, t, 128), F32),
        scratch_shapes=[pltpu.VMEM((CONV_WIDTH - 1, 8, LRU_WIDTH), F32), pltpu.VMEM((8, LRU_WIDTH), F32)],
        compiler_params=_cparams(("parallel", "arbitrary")),
        name="rglru",
    )(zl, pv, wa, wx)


def _mlstm_kernel(q_ref, k_ref, v_ref, og_ref, gc_ref, kt_ref, bcol_ref, brow_ref, nrm_ref, o_ref,
                  cn_ref, m_ref, *, tt):
    t = pl.program_id(1)

    @pl.when(t == 0)
    def _():
        cn_ref[...] = jnp.zeros_like(cn_ref)
        m_ref[...] = jnp.zeros_like(m_ref)

    L = CHUNK
    gr = kt_ref[MLSTM_WIDTH:MLSTM_WIDTH + 8, :] + bcol_ref[:, 0:1]
    br = _log_sigmoid(gr)
    pos = lax.broadcasted_iota(jnp.int32, (8, tt), 1) & (L - 1)
    d = 1
    while d < L:
        br = br + jnp.where(pos >= d, pltpu.roll(br, d, axis=1), 0.0)
        d *= 2
    bc = _log_sigmoid(gc_ref[...] + brow_ref[0:1, :])
    posc = lax.broadcasted_iota(jnp.int32, (tt, ZG_COLS), 0) & (L - 1)
    d = 1
    while d < L:
        bc = bc + jnp.where(posc >= d, pltpu.roll(bc, d, axis=0), 0.0)
        d *= 2

    nck = tt // L
    heads = range(MLSTM_HEADS)
    pairs = range(MLSTM_HEADS // 2)
    units = [(c, h) for c in range(nck) for h in heads]
    punits = [(c, p) for c in range(nck) for p in pairs]
    rsl = [slice(c * L, (c + 1) * L) for c in range(nck)]
    psl = [slice(128 * p, 128 * p + 128) for p in pairs]
    jsl = [slice(128 * (c // 2), 128 * (c // 2) + 128) for c in range(nck)]

    lane = lax.broadcasted_iota(jnp.int32, (L, 128), 1)
    rowi = lax.broadcasted_iota(jnp.int32, (L, 128), 0)
    half0 = lane < HEAD_DIM
    causal = (lane & (HEAD_DIM - 1)) <= rowi
    lane1 = lax.broadcasted_iota(jnp.int32, (1, 128), 1)
    rmask = [lane1 < L, lane1 >= L]
    r2 = lax.broadcasted_iota(jnp.int32, (2 * L, 128), 0)
    l2 = lax.broadcasted_iota(jnp.int32, (2 * L, 128), 1)
    top = r2 < L
    diag = (r2 < L) == (l2 < HEAD_DIM)
    ones_bd = jnp.where(diag, 1.0, 0.0).astype(BF16)

    b_row = {(c, h): br[4 + h:5 + h, jsl[c]] for c, h in units}
    i_row = {(c, h): gr[h:h + 1, jsl[c]] for c, h in units}

    g = {(c, h): br[4 + h:5 + h, (c + 1) * L - 1:(c + 1) * L] for c, h in units}
    w_log = {u: jnp.where(rmask[u[0] % 2], g[u] - b_row[u] + i_row[u], -jnp.inf) for u in units}
    w_max = {u: jnp.max(w_log[u], axis=1, keepdims=True) for u in units}
    m_in, m_out = {}, {}
    for h in heads:
        m_st = m_ref[h, 0:1, 0:1]
        for c in range(nck):
            m_in[c, h] = m_st
            m_st = jnp.maximum(g[c, h] + m_st, w_max[c, h])
            m_out[c, h] = m_st
        m_ref[h] = jnp.broadcast_to(m_st, (8, 128))
    dec = {u: jnp.exp(g[u] + m_in[u] - m_out[u]) for u in units}
    wk = {u: jnp.exp(w_log[u] - m_out[u]) for u in units}

    def lanes_of_chunk(row128, c, e):
        return row128 if c % 2 == e else pltpu.roll(row128, HEAD_DIM, axis=1)

    def pair_row(d, c, p):
        return jnp.where(rmask[0], lanes_of_chunk(d[c, 2 * p], c, 0), lanes_of_chunk(d[c, 2 * p + 1], c, 1))

    def pair_tile(d, c, p):
        return jnp.where(half0, d[c, 2 * p], d[c, 2 * p + 1])

    q_p = {(c, p): (q_ref[rsl[c], psl[p]] * (HEAD_DIM ** -0.5)).astype(BF16) for c, p in punits}
    k_bd = {}
    v_bd = {}
    for c, p in punits:
        k_c = k_ref[rsl[c], psl[p]]
        v_c = v_ref[rsl[c], psl[p]]
        k_bd[c, p] = jnp.where(diag, jnp.concatenate([k_c, k_c], axis=0), 0.0).astype(BF16)
        v_bd[c, p] = jnp.concatenate([jnp.where(diag, jnp.concatenate([v_c, v_c], axis=0), 0.0).astype(BF16),
                                      ones_bd], axis=1)
    bcb = {(c, p): jnp.where(half0, jnp.broadcast_to(bc[rsl[c], 4 + 2 * p:5 + 2 * p], (L, 128)),
                             jnp.broadcast_to(bc[rsl[c], 5 + 2 * p:6 + 2 * p], (L, 128))) for c, p in punits}
    dmat = {(c, p): jnp.where(causal, bcb[c, p] - pair_row(b_row, c, p) + pair_row(i_row, c, p), -jnp.inf)
            for c, p in punits}
    m_loc = {u: jnp.where(half0, jnp.max(jnp.where(half0, dmat[u], -jnp.inf), axis=1, keepdims=True),
                          jnp.max(jnp.where(half0, -jnp.inf, dmat[u]), axis=1, keepdims=True)) for u in punits}
    p_in = {u: jnp.exp(dmat[u] - m_loc[u]) for u in punits}
    qk = {u: _dot_nt(q_p[u], k_bd[u]) for u in punits}
    nd = {u: _dot(p_in[u] * qk[u], v_bd[u]) for u in punits}
    kv = {}
    for c, p in punits:
        kw = kt_ref[psl[p], jsl[c]] * jnp.where(top, wk[c, 2 * p], wk[c, 2 * p + 1])
        vv = jnp.concatenate([v_ref[jsl[c], psl[p]].astype(BF16), jnp.ones((2 * L, 128), BF16)], axis=1)
        kv[c, p] = jnp.where(jnp.concatenate([diag, diag], axis=1), _dot(kw, vv), 0.0)

    cn_in = {}
    for p in pairs:
        cn = cn_ref[p]
        for c in range(nck):
            cn_in[c, p] = cn.astype(BF16)
            cn = jnp.concatenate([dec[c, 2 * p] * cn[0:L], dec[c, 2 * p + 1] * cn[L:2 * L]], axis=0) + kv[c, p]
        cn_ref[p] = cn

    inter = {(c, p): bcb[c, p] + pair_tile(m_in, c, p) for c, p in punits}
    mj = {u: jnp.maximum(m_loc[u], inter[u]) for u in punits}
    e_loc = {u: jnp.exp(m_loc[u] - mj[u]) for u in punits}
    e_int = {u: jnp.exp(inter[u] - mj[u]) for u in punits}
    e_neg = {u: jnp.exp(-mj[u]) for u in punits}
    qc = {u: _dot(q_p[u], cn_in[u]) for u in punits}
    num = {u: e_loc[u] * nd[u][:, 0:128] + e_int[u] * qc[u][:, 0:128] for u in punits}
    den = {u: e_loc[u] * nd[u][:, 128:256] + e_int[u] * qc[u][:, 128:256] for u in punits}
    hv = {u: num[u] / jnp.maximum(jnp.abs(den[u]), e_neg[u]) for u in punits}
    hv2 = {u: hv[u] * hv[u] for u in punits}
    ms = {u: jnp.where(half0, jnp.sum(jnp.where(half0, hv2[u], 0.0), axis=1, keepdims=True),
                       jnp.sum(jnp.where(half0, 0.0, hv2[u]), axis=1, keepdims=True)) * (1.0 / HEAD_DIM)
          for u in punits}
    for c, p in punits:
        hn = hv[c, p] * lax.rsqrt(ms[c, p] + RMS_EPS)
        o_ref[rsl[c], psl[p]] = hn * nrm_ref[0:1, psl[p]] * jax.nn.sigmoid(og_ref[rsl[c], psl[p]])


def _mlstm(zm, zg, ktg, bcol, brow, nrm, *, tt):
    b, t, _ = zm.shape
    nt = t // tt
    colblk = lambda j: pl.BlockSpec((None, tt, MLSTM_WIDTH), lambda bb, ti, j=j: (bb, ti, j))
    full = lambda bb, ti: (0, 0)
    return pl.pallas_call(
        functools.partial(_mlstm_kernel, tt=tt),
        grid=(b, nt),
        in_specs=[
            colblk(0), colblk(1), colblk(2), colblk(3),
            pl.BlockSpec((None, tt, ZG_COLS), lambda bb, ti: (bb, ti, 0)),
            pl.BlockSpec((KT_ROWS, tt), lambda bb, ti: (0, bb * nt + ti)),
            pl.BlockSpec((8, 128), full),
            pl.BlockSpec((8, 128), full),
            pl.BlockSpec((8, MLSTM_WIDTH), full),
        ],
        out_specs=pl.BlockSpec((None, tt, MLSTM_WIDTH), lambda bb, ti: (bb, ti, 0)),
        out_shape=jax.ShapeDtypeStruct((b, t, MLSTM_WIDTH), F32),
        scratch_shapes=[pltpu.VMEM((MLSTM_HEADS // 2, 2 * HEAD_DIM, 256), F32),
                        pltpu.VMEM((MLSTM_HEADS, 8, 128), F32)],
        compiler_params=_cparams(("parallel", "arbitrary")),
        name="mlstm",
    )(zm, zm, zm, zm, zg, ktg, bcol, brow, nrm)


def _rwkv_kernel(z_ref, mu_ref, pv_ref, wup_ref, aup_ref, gup_ref, seg_ref, o_ref,
                 prev_ref, st_ref, y_ref, *, tt, sb):
    t = pl.program_id(1)

    @pl.when(t == 0)
    def _():
        prev_ref[...] = jnp.zeros_like(prev_ref)
        st_ref[...] = jnp.zeros_like(st_ref)

    L = CHUNK
    W = RWKV_WIDTH
    UR = 2 * L
    SB = sb
    nrb = tt // SB
    nsub = SB // UR
    pv = pv_ref[...]
    w0, a0, k_k, k_a, r_k, ln_w, ln_b = (pv[j:j + 1, :] for j in range(7))
    seg = seg_ref[...]
    mu = mu_ref[...]
    heads = range(RWKV_HEADS)
    hsl = [slice(HEAD_DIM * h, HEAD_DIM * (h + 1)) for h in heads]

    ri = lax.broadcasted_iota(jnp.int32, (UR, UR), 0)
    ci = lax.broadcasted_iota(jnp.int32, (UR, UR), 1)
    same = (ri // L) == (ci // L)
    strict = jnp.where(same & (ri > ci), 1.0, 0.0)
    incl = jnp.where(same & (ri >= ci), 1.0, 0.0)
    eye = jnp.where(ri == ci, 1.0, 0.0)
    mask4 = jnp.concatenate([jnp.concatenate([strict, strict], axis=1),
                             jnp.concatenate([incl, incl], axis=1)], axis=0)
    posr = lax.broadcasted_iota(jnp.int32, (SB, W), 0) & (L - 1)
    row1 = lax.broadcasted_iota(jnp.int32, (SB, 1), 0)
    units = [(h, u) for u in range(nsub) for h in heads]
    nu = range(len(units))
    usl = [slice(UR * u, UR * (u + 1)) for _, u in units]

    prep, mats = {}, {}
    zst = [st_ref[h] for h in heads]

    def prepare(rb):
        rows = slice(SB * rb, SB * (rb + 1))
        z = z_ref[rows, :]
        first = prev_ref[0:1, :] if rb == 0 else z_ref[SB * rb - 1:SB * rb, :]
        zprev = jnp.where(row1 == 0, first, pltpu.roll(z, 1, axis=0))
        yield
        zs = z + (zprev - z) * mu
        yield
        r = zs[:, 0:W]
        k = zs[:, W:2 * W]
        v = zs[:, 2 * W:3 * W]
        wd = zs[:, 3 * W:3 * W + 64]
        ad = zs[:, 3 * W + 64:3 * W + 128]
        gd = zs[:, 3 * W + 128:3 * W + 256]
        log_w = -_softplus(-(w0 + _dot(jnp.tanh(wd), wup_ref[...]))) - 0.5
        yield
        ld = -jnp.exp(log_w)
        yield
        iclr = jax.nn.sigmoid(a0 + _dot(ad, aup_ref[...]))
        yield
        g = _dot(jax.nn.sigmoid(gd), gup_ref[...])
        yield
        kk = k * k_k
        ss = _segsum(kk * kk, seg)
        yield
        kk = kk / jnp.maximum(jnp.sqrt(ss), 1e-12)
        yield
        k_mod = k * (1.0 + (iclr - 1.0) * k_a)
        a_vec = -kk
        b_vec = kk * iclr
        yield
        cum = ld
        d = 1
        while d < L:
            cum = cum + jnp.where(posr >= d, pltpu.roll(cum, d, axis=0), 0.0)
            d *= 2
            yield
        cum_last = jnp.concatenate(
            [jnp.broadcast_to(cum[(c + 1) * L - 1:(c + 1) * L, :], (L, W)) for c in range(SB // L)], axis=0)
        e_neg = jnp.exp(-cum)
        yield
        e_tail = jnp.exp(cum_last - cum)
        yield
        a_t = jnp.exp(cum - ld) * a_vec
        yield
        r_t = jnp.exp(cum) * r
        yield
        b_t = e_neg * b_vec
        k_t = e_neg * k_mod
        yield
        b_h = e_tail * b_vec
        k_h = e_tail * k_mod
        yield
        bh_t, kh_t, cum_t = [], [], []
        for pair in range(RWKV_HEADS // 2):
            ps = slice(128 * pair, 128 * pair + 128)
            bp, kp, cp = b_h[:, ps].T.astype(BF16), k_h[:, ps].T.astype(BF16), cum[:, ps].T
            for e in range(2):
                es = slice(HEAD_DIM * e, HEAD_DIM * (e + 1))
                bh_t.append(bp[es])
                kh_t.append(kp[es])
                cum_t.append(cp[es])
            yield
        prep[rb] = dict(
            at=[a_t[usl[i], hsl[h]].astype(BF16) for i, (h, _) in enumerate(units)],
            rt=[r_t[usl[i], hsl[h]] for i, (h, _) in enumerate(units)],
            bt=[b_t[usl[i], hsl[h]].astype(BF16) for i, (h, _) in enumerate(units)],
            kt=[k_t[usl[i], hsl[h]].astype(BF16) for i, (h, _) in enumerate(units)],
            vu=[v[usl[i], hsl[h]].astype(BF16) for i, (h, _) in enumerate(units)],
            bh_t=bh_t, kh_t=kh_t, cum_t=cum_t,
            bonus_in=r * k_mod * r_k, v=v, g=g)
        yield

    def products(rb):
        p = prep[rb]
        at, rt, bt, kt, vu = p["at"], p["rt"], p["bt"], p["kt"], p["vu"]
        prod = [mask4 * _dot_nt(jnp.concatenate([at[i], rt[i].astype(BF16)], axis=0),
                                jnp.concatenate([bt[i], kt[i]], axis=0)) for i in nu]
        yield
        a_ab = [prod[i][0:UR, 0:UR] for i in nu]
        a_ak = [prod[i][0:UR, UR:2 * UR].astype(BF16) for i in nu]
        a_rbk = [prod[i][UR:2 * UR, :].astype(BF16) for i in nu]
        yield
        tm = [eye + a_ab[i] for i in nu]
        pw = [a_ab[i].astype(BF16) for i in nu]
        yield
        pw = [_dot(pw[i], pw[i]).astype(BF16) for i in nu]
        yield
        for _ in range(4):
            res = [_dot(pw[i], jnp.concatenate([pw[i], tm[i].astype(BF16)], axis=1)) for i in nu]
            yield
            pw = [res[i][:, 0:UR].astype(BF16) for i in nu]
            tm = [tm[i] + res[i][:, UR:2 * UR] for i in nu]
            yield
        tm = [tm[i] + _dot(pw[i], tm[i]) for i in nu]
        yield
        akv = [_dot(a_ak[i], vu[i]) for i in nu]
        yield
        tmb = [tm[i].astype(BF16) for i in nu]
        w_all = [_dot(tmb[i], at[i]).astype(BF16) for i in nu]
        yield
        u0_all = [_dot(tmb[i], akv[i]).astype(BF16) for i in nu]
        yield
        q_all = [(rt[i] + _dot(a_rbk[i][:, 0:UR], w_all[i])).astype(BF16) for i in nu]
        yield
        y0_all = [_dot(a_rbk[i], jnp.concatenate([u0_all[i], vu[i]], axis=0)) for i in nu]
        yield
        qc, hm, pcol, y0c = {}, {}, {}, {}
        for i, (h, u) in enumerate(units):
            for cc in range(UR // L):
                c = u * (UR // L) + cc
                ls = slice(cc * L, (cc + 1) * L)
                rs = slice(c * L, (c + 1) * L)
                gm = _dot(p["bh_t"][h][:, rs], w_all[i][ls]).astype(BF16)
                hm[c, h] = _dot(p["bh_t"][h][:, rs], u0_all[i][ls]) + _dot(p["kh_t"][h][:, rs], vu[i][ls])
                pcol[c, h] = jnp.exp(p["cum_t"][h][:, (c + 1) * L - 1:(c + 1) * L])
                qc[c, h] = jnp.concatenate([q_all[i][ls], gm], axis=0)
                y0c[c, h] = y0_all[i][ls]
            if h == RWKV_HEADS - 1:
                yield
        mats[rb] = dict(qc=qc, hm=hm, pcol=pcol, y0c=y0c)
        yield

    def finish(rb):
        m, p = mats[rb], prep[rb]
        for cc in range(SB // L):
            rs = slice(SB * rb + cc * L, SB * rb + (cc + 1) * L)
            for h in heads:
                qz = _dot(m["qc"][cc, h], zst[h])
                y_ref[rs, hsl[h]] = qz[0:L] + m["y0c"][cc, h]
                zst[h] = m["pcol"][cc, h] * zst[h] + qz[L:2 * L] + m["hm"][cc, h]
                yield
        rows = slice(SB * rb, SB * (rb + 1))
        y = y_ref[rows, :]
        mean = _segsum(y, seg) * (1.0 / HEAD_DIM)
        yield
        yc = y - mean
        var = _segsum(yc * yc, seg) * (1.0 / HEAD_DIM)
        yield
        yn = yc * lax.rsqrt(var + GN_EPS) * ln_w + ln_b
        yield
        bonus = _segsum(p["bonus_in"], seg) * p["v"]
        yield
        o_ref[rows, :] = (yn + bonus) * p["g"]
        yield

    _interleave(prepare(0))
    for step in range(1, nrb + 2):
        stage = []
        if step < nrb:
            stage.append(prepare(step))
        if 0 <= step - 1 < nrb:
            stage.append(products(step - 1))
        if 0 <= step - 2 < nrb:
            stage.append(finish(step - 2))
        _interleave(*stage)
    for h in heads:
        st_ref[h] = zst[h]
    prev_ref[...] = jnp.broadcast_to(z_ref[tt - 1:tt, :], prev_ref.shape)


def _rwkv(zr, mu, pv, wup, aup, gup, seg, *, tt, sb):
    b, t, _ = zr.shape
    full = lambda bb, ti: (0, 0)
    return pl.pallas_call(
        functools.partial(_rwkv_kernel, tt=tt, sb=sb),
        grid=(b, t // tt),
        in_specs=[
            pl.BlockSpec((None, tt, N_RWKV_COLS), lambda bb, ti: (bb, ti, 0)),
            pl.BlockSpec((1, N_RWKV_COLS), full),
            pl.BlockSpec((8, RWKV_WIDTH), full),
            pl.BlockSpec((64, RWKV_WIDTH), full),
            pl.BlockSpec((64, RWKV_WIDTH), full),
            pl.BlockSpec((128, RWKV_WIDTH), full),
            pl.BlockSpec((4 * HEAD_DIM, 4 * HEAD_DIM), full),
        ],
        out_specs=pl.BlockSpec((None, tt, RWKV_WIDTH), lambda bb, ti: (bb, ti, 0)),
        out_shape=jax.ShapeDtypeStruct((b, t, RWKV_WIDTH), F32),
        scratch_shapes=[
            pltpu.VMEM((8, N_RWKV_COLS), F32),
            pltpu.VMEM((RWKV_HEADS, HEAD_DIM, HEAD_DIM), F32),
            pltpu.VMEM((tt, RWKV_WIDTH), F32),
        ],
        compiler_params=_cparams(("parallel", "arbitrary")),
        name="rwkv7",
    )(zr, mu, pv, wup, aup, gup, seg)


def _tail_kernel(x_ref, ya_ref, yb_ref, yc_ref, p_ref, g_ref, wm_ref, wi_ref, wo_ref, wg_ref, wp_ref, o_ref,
                 h_ref, *, tf, nsplit, skew):
    c0, c1 = LRU_WIDTH, LRU_WIDTH + MLSTM_WIDTH
    tm = x_ref.shape[0]
    hr = tm // nsplit

    def part(i):
        rows = slice(hr * i, hr * (i + 1))
        ya = jnp.concatenate(
            [jnp.concatenate([ya_ref[j, _perm_rows(m, tm), :] for j in range(LRU_WIDTH // 128)], axis=1)
             for m in range(hr * i // 8, hr * (i + 1) // 8)], axis=0)
        mix = (jnp.dot(ya.astype(BF16), wm_ref[0:c0, :], preferred_element_type=F32)
               + jnp.dot(yb_ref[rows, :].astype(BF16), wm_ref[c0:c1, :], preferred_element_type=F32)
               + jnp.dot(yc_ref[rows, :].astype(BF16), wm_ref[c1:, :], preferred_element_type=F32))
        yield
        x = x_ref[rows, :] + _rms(mix, g_ref[3:4, :])
        yield
        xn = _rms(x, g_ref[4:5, :]).astype(BF16)
        yield
        for c in range(D_FF // tf):
            gate = jnp.dot(xn, wi_ref[:, c * tf:(c + 1) * tf], preferred_element_type=F32)
            up = jnp.dot(xn, wi_ref[:, D_FF + c * tf:D_FF + (c + 1) * tf], preferred_element_type=F32)
            h_ref[rows, c * tf:(c + 1) * tf] = (gate * jax.nn.sigmoid(gate) * up).astype(BF16)
            yield
        y = jnp.dot(h_ref[rows, :], wo_ref[...], preferred_element_type=F32)
        yield
        x = x + 0.5 * _rms(y, g_ref[5:6, :])
        yield
        pgate = jax.nn.sigmoid(
            jnp.dot(_rms(x, g_ref[6:7, :]).astype(BF16), wg_ref[...], preferred_element_type=F32))
        yield
        pe = jnp.dot(p_ref[rows, :].astype(BF16), wp_ref[...], preferred_element_type=F32)
        yield
        o_ref[rows, :] = x + _rms(pgate * pe, g_ref[7:8, :])
        yield

    _interleave(*[part(i) for i in range(nsplit)], skew=skew)


def _tail(x, ya, yb, yc, p, g8, w_mix, w_in, w_out, w_gate, w_proj, *, tm, tf):
    n = x.shape[0]
    row = lambda i: (i, 0)
    const = lambda shape: pl.BlockSpec(shape, lambda i: (0, 0), pipeline_mode=pl.Buffered(1))
    return pl.pallas_call(
        functools.partial(_tail_kernel, tf=tf, nsplit=2, skew=11),
        grid=(n // tm,),
        in_specs=[
            pl.BlockSpec((tm, D_MODEL), row),
            pl.BlockSpec((LRU_WIDTH // 128, tm, 128), lambda i: (0, i, 0)),
            pl.BlockSpec((tm, MLSTM_WIDTH), row),
            pl.BlockSpec((tm, RWKV_WIDTH), row),
            pl.BlockSpec((tm, PLE_DIM), row),
            const((8, D_MODEL)),
            const((D_MODEL, D_MODEL)),
            const((D_MODEL, 2 * D_FF)),
            const((D_FF, D_MODEL)),
            const((D_MODEL, D_MODEL)),
            const((PLE_DIM, D_MODEL)),
        ],
        out_specs=pl.BlockSpec((tm, D_MODEL), row),
        out_shape=jax.ShapeDtypeStruct((n, D_MODEL), F32),
        scratch_shapes=[pltpu.VMEM((tm, D_FF), BF16)],
        compiler_params=_cparams(("parallel",)),
        name="tail",
    )(x, ya, yb, yc, p, g8, w_mix, w_in, w_out, w_gate, w_proj)


def _block_diag(w):
    nh, dd, _ = w.shape
    eye = jnp.eye(nh, dtype=w.dtype)
    return (eye[:, None, :, None] * w[:, :, None, :]).reshape(nh * dd, nh * dd)


def _tile(n, pref):
    return pref if n % pref == 0 else n


def kernel(x, p, norm_g, ffn_w_in, ffn_w_out, w_in, w_out, lru_conv_w, lru_conv_b, lru_w_a, lru_b_a, lru_w_x, lru_b_x, lru_lambda, m_b_i, m_b_f, m_norm, rw_mu, rw_w0, rw_w_up, rw_a0, rw_a_up, rw_g_up, rw_k_k, rw_k_a, rw_r_k, rw_ln_w, rw_ln_b, ple_w_proj, ple_w_gate):
    bsz, t, _ = x.shape
    n = bsz * t
    depth = norm_g.shape[0]
    tm_ffn = _tile(n, 512)
    tm = _tile(n, 512)
    tt_lru = _tile(t, 512)
    tt_m = _tile(t, 1024)
    tt_r = _tile(t, 1024)
    seg = _block_diag(jnp.ones((4, HEAD_DIM, HEAD_DIM), BF16))

    xf = x.reshape(n, D_MODEL)
    for l in range(depth):
        g = norm_g[l]
        wl = w_in[l]
        o_zr = 2 * LRU_WIDTH + 4 * MLSTM_WIDTH + 2 * MLSTM_HEADS
        w_gates = wl[:, o_zr - 2 * MLSTM_HEADS:o_zr]
        wcat = jnp.concatenate(
            [wl[:, o_zr:], jnp.pad(w_gates, ((0, 0), (0, ZG_COLS - 2 * MLSTM_HEADS))),
             wl[:, 0:ZL_COLS], wl[:, ZL_COLS:ZL_COLS + ZM_COLS]], axis=1).astype(BF16)

        xf = _ffn(xf, g[0:2], ffn_w_in[l, 0].astype(BF16), ffn_w_out[l, 0].astype(BF16), tm=tm_ffn, tf=256)

        zr, zl, zm, zg, ktg = _mixer_in(xf, g[2:3], wcat, tm=tm)

        lru_pv = jnp.concatenate([lru_conv_w[l], lru_conv_b[l][None], lru_b_a[l][None], lru_b_x[l][None],
                                  lru_lambda[l][None]], axis=0)
        ya = _lru(zl.reshape(ZL_COLS // 128, bsz, t, 128), lru_pv, _block_diag(lru_w_a[l]).astype(BF16),
                  _block_diag(lru_w_x[l]).astype(BF16), tt=tt_lru)

        gate_b = jnp.concatenate([m_b_i[l], m_b_f[l]])
        bcol = jnp.broadcast_to(gate_b[:, None], (8, 128))
        brow = jnp.broadcast_to(jnp.pad(gate_b, (0, ZG_COLS - 8))[None, :], (8, ZG_COLS))
        nrm = jnp.broadcast_to(m_norm[l][None, :], (8, MLSTM_WIDTH))
        yb = _mlstm(zm.reshape(bsz, t, ZM_COLS), zg.reshape(bsz, t, ZG_COLS), ktg, bcol, brow, nrm, tt=tt_m)

        rw_pv = jnp.stack([rw_w0[l], rw_a0[l], rw_k_k[l], rw_k_a[l], rw_r_k[l].reshape(-1), rw_ln_w[l],
                           rw_ln_b[l], jnp.zeros((RWKV_WIDTH,), F32)], axis=0)
        yc = _rwkv(zr.reshape(bsz, t, ZR_COLS), rw_mu[l][None, :], rw_pv, rw_w_up[l].astype(BF16),
                   rw_a_up[l].astype(BF16), rw_g_up[l].astype(BF16), seg, tt=tt_r, sb=min(tt_r, 256))

        xf = _tail(xf, ya.reshape(LRU_WIDTH // 128, n, 128), yb.reshape(n, MLSTM_WIDTH), yc.reshape(n, RWKV_WIDTH),
                   p[l].reshape(n, PLE_DIM), g, w_out[l].astype(BF16), ffn_w_in[l, 1].astype(BF16),
                   ffn_w_out[l, 1].astype(BF16), ple_w_gate[l].astype(BF16), ple_w_proj[l].astype(BF16),
                   tm=tm_ffn, tf=256)
    return xf.reshape(bsz, t, D_MODEL)
```

```python
import functools

import jax
import jax.numpy as jnp
from jax import lax
from jax.experimental import pallas as pl
from jax.experimental.pallas import tpu as pltpu

F32 = jnp.float32
BF16 = jnp.bfloat16

D_MODEL = 1024
PLE_DIM = 256
D_FF = 2816
RMS_EPS = 1e-6
LRU_WIDTH = 384
LRU_HEADS = 6
CONV_WIDTH = 4
LRU_C = 8.0
MLSTM_HEADS = 4
MLSTM_WIDTH = 256
RWKV_HEADS = 6
RWKV_WIDTH = 384
N_RWKV_COLS = 1408
GN_EPS = 64e-5
HEAD_DIM = 64
CHUNK = 64

ZR_COLS = N_RWKV_COLS
ZL_COLS = 2 * LRU_WIDTH
ZM_COLS = 4 * MLSTM_WIDTH
ZG_COLS = 128
KT_ROWS = MLSTM_WIDTH + 8

VMEM_LIMIT = 56 * 1024 * 1024


def _cparams(sem):
    return pltpu.CompilerParams(dimension_semantics=sem, vmem_limit_bytes=VMEM_LIMIT)


def _rms(x, g):
    return x * lax.rsqrt(jnp.mean(x * x, axis=-1, keepdims=True) + RMS_EPS) * g


def _softplus(x):
    return jnp.maximum(x, 0.0) + jnp.log(1.0 + jnp.exp(-jnp.abs(x)))


def _log_sigmoid(x):
    return -_softplus(-x)


def _dot(a, b):
    return jnp.dot(a.astype(BF16), b.astype(BF16), preferred_element_type=F32)


def _dot_nt(a, b):
    return lax.dot_general(a.astype(BF16), b.astype(BF16), (((1,), (1,)), ((), ())),
                           preferred_element_type=F32)


def _interleave(*gens, skew=0):
    pending, active, rounds = list(gens), [], 0
    while pending or active:
        if pending and skew == 0:
            active, pending = active + pending, []
        elif pending and rounds % skew == 0:
            active.append(pending.pop(0))
        for gen in list(active):
            try:
                next(gen)
            except StopIteration:
                active.remove(gen)
        rounds += 1


def _perm_rows(m, tile):
    ng = tile // 8
    s, g0 = divmod(8 * m, ng)
    return pl.ds(8 * g0 + s, 8, stride=8)


def _segsum(x, seg):
    hi = x.astype(BF16)
    lo = (x - hi.astype(F32)).astype(BF16)
    w = seg.shape[0]
    head = (jnp.dot(hi[:, 0:w], seg, preferred_element_type=F32)
            + jnp.dot(lo[:, 0:w], seg, preferred_element_type=F32))
    tail = jnp.dot(jnp.concatenate([hi[:, w:], lo[:, w:]], axis=1), seg, preferred_element_type=F32)
    return jnp.concatenate([head, tail[:, 0:w // 2] + tail[:, w // 2:]], axis=1)


def _ffn_kernel(x_ref, g_ref, wi_ref, wo_ref, o_ref, h_ref, *, tf, nsplit, skew):
    hr = x_ref.shape[0] // nsplit

    def part(i):
        rows = slice(hr * i, hr * (i + 1))
        x = x_ref[rows, :]
        xn = _rms(x, g_ref[0:1, :]).astype(BF16)
        yield
        for c in range(D_FF // tf):
            gate = jnp.dot(xn, wi_ref[:, c * tf:(c + 1) * tf], preferred_element_type=F32)
            up = jnp.dot(xn, wi_ref[:, D_FF + c * tf:D_FF + (c + 1) * tf], preferred_element_type=F32)
            h_ref[rows, c * tf:(c + 1) * tf] = (gate * jax.nn.sigmoid(gate) * up).astype(BF16)
            yield
        y = jnp.dot(h_ref[rows, :], wo_ref[...], preferred_element_type=F32)
        yield
        o_ref[rows, :] = x + 0.5 * _rms(y, g_ref[1:2, :])
        yield

    _interleave(*[part(i) for i in range(nsplit)], skew=skew)


def _ffn(x, g2, w_in, w_out, *, tm, tf):
    n = x.shape[0]
    const = dict(pipeline_mode=pl.Buffered(1))
    return pl.pallas_call(
        functools.partial(_ffn_kernel, tf=tf, nsplit=2, skew=6),
        grid=(n // tm,),
        in_specs=[
            pl.BlockSpec((tm, D_MODEL), lambda i: (i, 0)),
            pl.BlockSpec((2, D_MODEL), lambda i: (0, 0)),
            pl.BlockSpec((D_MODEL, 2 * D_FF), lambda i: (0, 0), **const),
            pl.BlockSpec((D_FF, D_MODEL), lambda i: (0, 0), **const),
        ],
        out_specs=pl.BlockSpec((tm, D_MODEL), lambda i: (i, 0)),
        out_shape=jax.ShapeDtypeStruct((n, D_MODEL), F32),
        scratch_shapes=[pltpu.VMEM((tm, D_FF), BF16)],
        compiler_params=_cparams(("parallel",)),
        name="ffn",
    )(x, g2, w_in, w_out)


def _mixin_kernel(x_ref, g_ref, w_ref, zr_ref, zl_ref, zm_ref, zg_ref, kt_ref):
    xn = _rms(x_ref[...], g_ref[...]).astype(BF16)
    c0, c1, c2 = ZR_COLS + ZG_COLS, ZR_COLS + ZG_COLS + ZL_COLS, ZR_COLS + ZG_COLS + ZL_COLS + ZM_COLS
    zrg = jnp.dot(xn, w_ref[:, 0:c0], preferred_element_type=F32)
    zg = zrg[:, ZR_COLS:c0]
    zr_ref[...] = zrg[:, 0:ZR_COLS]
    zg_ref[...] = zg
    zl = jnp.dot(xn, w_ref[:, c0:c1], preferred_element_type=F32)
    for m in range(zl.shape[0] // 8):
        rows = _perm_rows(m, zl.shape[0])
        for j in range(ZL_COLS // 128):
            zl_ref[j, rows, :] = zl[8 * m:8 * m + 8, 128 * j:128 * (j + 1)]
    zm = jnp.dot(xn, w_ref[:, c1:c2], preferred_element_type=F32)
    zm_ref[...] = zm
    kt_ref[0:MLSTM_WIDTH, :] = zm[:, MLSTM_WIDTH:2 * MLSTM_WIDTH].T
    kt_ref[MLSTM_WIDTH:KT_ROWS, :] = zg.T[0:KT_ROWS - MLSTM_WIDTH, :]


def _mixer_in(x, g, wcat, *, tm):
    n = x.shape[0]
    ncols = wcat.shape[1]
    row = lambda i: (i, 0)
    return pl.pallas_call(
        _mixin_kernel,
        grid=(n // tm,),
        in_specs=[
            pl.BlockSpec((tm, D_MODEL), row),
            pl.BlockSpec((1, D_MODEL), lambda i: (0, 0)),
            pl.BlockSpec((D_MODEL, ncols), lambda i: (0, 0)),
        ],
        out_specs=[
            pl.BlockSpec((tm, ZR_COLS), row),
            pl.BlockSpec((ZL_COLS // 128, tm, 128), lambda i: (0, i, 0)),
            pl.BlockSpec((tm, ZM_COLS), row),
            pl.BlockSpec((tm, ZG_COLS), row),
            pl.BlockSpec((KT_ROWS, tm), lambda i: (0, i)),
        ],
        out_shape=[
            jax.ShapeDtypeStruct((n, ZR_COLS), F32),
            jax.ShapeDtypeStruct((ZL_COLS // 128, n, 128), F32),
            jax.ShapeDtypeStruct((n, ZM_COLS), F32),
            jax.ShapeDtypeStruct((n, ZG_COLS), F32),
            jax.ShapeDtypeStruct((KT_ROWS, n), F32),
        ],
        compiler_params=_cparams(("parallel",)),
        name="mixer_in",
    )(x, g, wcat)


def _lru_kernel(z_ref, pv_ref, wa_ref, wx_ref, o_ref, xtail_ref, h_ref, *, tt):
    t = pl.program_id(1)

    @pl.when(t == 0)
    def _():
        xtail_ref[...] = jnp.zeros_like(xtail_ref)
        h_ref[...] = jnp.zeros_like(h_ref)

    ng = tt // 8
    nslab = LRU_WIDTH // 128
    taps = CONV_WIDTH - 1
    sub = lax.broadcasted_iota(jnp.int32, (8, LRU_WIDTH), 0)

    def load(first_slab, g):
        return jnp.concatenate([z_ref[first_slab + j, 8 * g:8 * g + 8, :] for j in range(nslab)], axis=1)

    x = [load(0, g) for g in range(ng)]
    pv = pv_ref[...]

    def before(g):
        if g >= 0:
            return x[g]
        cur = pltpu.roll(x[ng + g], 1, axis=0)
        prev = pltpu.roll(xtail_ref[taps + g], 1, axis=0)
        return jnp.where(sub == 0, prev, cur)

    nhalf = 2
    parts = {}

    def front(i):
        pieces = range(ng // nhalf * i, ng // nhalf * (i + 1))
        xa = []
        for g in pieces:
            acc = pv[taps + 1:taps + 2, :] + pv[taps:taps + 1, :] * x[g]
            for d in range(1, taps + 1):
                acc = acc + pv[taps - d:taps - d + 1, :] * before(g - d)
            xa.append(acc)
        xa = jnp.concatenate(xa, axis=0)
        yield
        r = jax.nn.sigmoid(_dot(xa, wa_ref[...]) + pv[5:6, :])
        i_gate = jax.nn.sigmoid(_dot(xa, wx_ref[...]) + pv[6:7, :])
        yield
        log_a = (-LRU_C) * r * _softplus(-pv[7:8, :])
        a_i = jnp.exp(log_a)
        yield
        u_i = jnp.sqrt(1.0 - jnp.exp(2.0 * log_a)) * (i_gate * xa)
        yield
        gate = jnp.concatenate([load(nslab, g) for g in pieces], axis=0)
        parts[i] = (a_i, u_i, jax.nn.gelu(gate, approximate=True))
        yield

    _interleave(*[front(i) for i in range(nhalf)], skew=1)
    for d in range(taps):
        xtail_ref[d] = x[ng - taps + d]
    a = jnp.concatenate([parts[i][0] for i in range(nhalf)], axis=0)
    u = jnp.concatenate([parts[i][1] for i in range(nhalf)], axis=0)
    gg = jnp.concatenate([parts[i][2] for i in range(nhalf)], axis=0)

    h_loc, a_cum = [], []
    h = u[0:8, :]
    ac = a[0:8, :]
    h_loc.append(h)
    a_cum.append(ac)
    for g in range(1, ng):
        a_g = a[8 * g:8 * g + 8, :]
        h = a_g * h + u[8 * g:8 * g + 8, :]
        ac = a_g * ac
        h_loc.append(h)
        a_cum.append(ac)
    ae, he = ac, h
    for d in (1, 2, 4):
        keep = sub >= d
        a_sh = jnp.where(keep, pltpu.roll(ae, d, axis=0), 1.0)
        h_sh = jnp.where(keep, pltpu.roll(he, d, axis=0), 0.0)
        he = ae * h_sh + he
        ae = ae * a_sh
    h0 = h_ref[0:1, :]
    h_end = he + ae * h0
    h_ref[...] = jnp.broadcast_to(h_end[7:8, :], h_ref.shape)
    c_in = jnp.where(sub == 0, h0, pltpu.roll(h_end, 1, axis=0))
    for g in range(ng):
        out = (h_loc[g] + a_cum[g] * c_in) * gg[8 * g:8 * g + 8, :]
        for j in range(nslab):
            o_ref[j, 8 * g:8 * g + 8, :] = out[:, 128 * j:128 * (j + 1)]


def _lru(zl, pv, wa, wx, *, tt):
    _, b, t, _ = zl.shape
    nslab = LRU_WIDTH // 128
    full = lambda bb, ti: (0, 0)
    return pl.pallas_call(
        functools.partial(_lru_kernel, tt=tt),
        grid=(b, t // tt),
        in_specs=[
            pl.BlockSpec((2 * nslab, None, tt, 128), lambda bb, ti: (0, bb, ti, 0)),
            pl.BlockSpec((8, LRU_WIDTH), full),
            pl.BlockSpec((LRU_WIDTH, LRU_WIDTH), full),
            pl.BlockSpec((LRU_WIDTH, LRU_WIDTH), full),
        ],
        out_specs=pl.BlockSpec((nslab, None, tt, 128), lambda bb, ti: (0, bb, ti, 0)),
        out_shape=jax.ShapeDtypeStruct((nslab, b, t, 128), F32),
        scratch_shapes=[pltpu.VMEM((CONV_WIDTH - 1, 8, LRU_WIDTH), F32), pltpu.VMEM((8, LRU_WIDTH), F32)],
        compiler_params=_cparams(("parallel", "arbitrary")),
        name="rglru",
    )(zl, pv, wa, wx)


def _mlstm_kernel(q_ref, k_ref, v_ref, og_ref, gc_ref, kt_ref, bcol_ref, brow_ref, nrm_ref, o_ref,
                  cn_ref, m_ref, *, tt):
    t = pl.program_id(1)

    @pl.when(t == 0)
    def _():
        cn_ref[...] = jnp.zeros_like(cn_ref)
        m_ref[...] = jnp.zeros_like(m_ref)

    L = CHUNK
    gr = kt_ref[MLSTM_WIDTH:MLSTM_WIDTH + 8, :] + bcol_ref[:, 0:1]
    br = _log_sigmoid(gr)
    pos = lax.broadcasted_iota(jnp.int32, (8, tt), 1) & (L - 1)
    d = 1
    while d < L:
        br = br + jnp.where(pos >= d, pltpu.roll(br, d, axis=1), 0.0)
        d *= 2
    bc = _log_sigmoid(gc_ref[...] + brow_ref[0:1, :])
    posc = lax.broadcasted_iota(jnp.int32, (tt, ZG_COLS), 0) & (L - 1)
    d = 1
    while d < L:
        bc = bc + jnp.where(posc >= d, pltpu.roll(bc, d, axis=0), 0.0)
        d *= 2

    nck = tt // L
    heads = range(MLSTM_HEADS)
    pairs = range(MLSTM_HEADS // 2)
    units = [(c, h) for c in range(nck) for h in heads]
    punits = [(c, p) for c in range(nck) for p in pairs]
    rsl = [slice(c * L, (c + 1) * L) for c in range(nck)]
    psl = [slice(128 * p, 128 * p + 128) for p in pairs]
    jsl = [slice(128 * (c // 2), 128 * (c // 2) + 128) for c in range(nck)]

    lane = lax.broadcasted_iota(jnp.int32, (L, 128), 1)
    rowi = lax.broadcasted_iota(jnp.int32, (L, 128), 0)
    half0 = lane < HEAD_DIM
    causal = (lane & (HEAD_DIM - 1)) <= rowi
    lane1 = lax.broadcasted_iota(jnp.int32, (1, 128), 1)
    rmask = [lane1 < L, lane1 >= L]
    r2 = lax.broadcasted_iota(jnp.int32, (2 * L, 128), 0)
    l2 = lax.broadcasted_iota(jnp.int32, (2 * L, 128), 1)
    top = r2 < L
    diag = (r2 < L) == (l2 < HEAD_DIM)
    ones_bd = jnp.where(diag, 1.0, 0.0).astype(BF16)

    b_row = {(c, h): br[4 + h:5 + h, jsl[c]] for c, h in units}
    i_row = {(c, h): gr[h:h + 1, jsl[c]] for c, h in units}

    g = {(c, h): br[4 + h:5 + h, (c + 1) * L - 1:(c + 1) * L] for c, h in units}
    w_log = {u: jnp.where(rmask[u[0] % 2], g[u] - b_row[u] + i_row[u], -jnp.inf) for u in units}
    w_max = {u: jnp.max(w_log[u], axis=1, keepdims=True) for u in units}
    m_in, m_out = {}, {}
    for h in heads:
        m_st = m_ref[h, 0:1, 0:1]
        for c in range(nck):
            m_in[c, h] = m_st
            m_st = jnp.maximum(g[c, h] + m_st, w_max[c, h])
            m_out[c, h] = m_st
        m_ref[h] = jnp.broadcast_to(m_st, (8, 128))
    dec = {u: jnp.exp(g[u] + m_in[u] - m_out[u]) for u in units}
    wk = {u: jnp.exp(w_log[u] - m_out[u]) for u in units}

    def lanes_of_chunk(row128, c, e):
        return row128 if c % 2 == e else pltpu.roll(row128, HEAD_DIM, axis=1)

    def pair_row(d, c, p):
        return jnp.where(rmask[0], lanes_of_chunk(d[c, 2 * p], c, 0), lanes_of_chunk(d[c, 2 * p + 1], c, 1))

    def pair_tile(d, c, p):
        return jnp.where(half0, d[c, 2 * p], d[c, 2 * p + 1])

    q_p = {(c, p): (q_ref[rsl[c], psl[p]] * (HEAD_DIM ** -0.5)).astype(BF16) for c, p in punits}
    k_bd = {}
    v_bd = {}
    for c, p in punits:
        k_c = k_ref[rsl[c], psl[p]]
        v_c = v_ref[rsl[c], psl[p]]
        k_bd[c, p] = jnp.where(diag, jnp.concatenate([k_c, k_c], axis=0), 0.0).astype(BF16)
        v_bd[c, p] = jnp.concatenate([jnp.where(diag, jnp.concatenate([v_c, v_c], axis=0), 0.0).astype(BF16),
                                      ones_bd], axis=1)
    bcb = {(c, p): jnp.where(half0, jnp.broadcast_to(bc[rsl[c], 4 + 2 * p:5 + 2 * p], (L, 128)),
                             jnp.broadcast_to(bc[rsl[c], 5 + 2 * p:6 + 2 * p], (L, 128))) for c, p in punits}
    dmat = {(c, p): jnp.where(causal, bcb[c, p] - pair_row(b_row, c, p) + pair_row(i_row, c, p), -jnp.inf)
            for c, p in punits}
    m_loc = {u: jnp.where(half0, jnp.max(jnp.where(half0, dmat[u], -jnp.inf), axis=1, keepdims=True),
                          jnp.max(jnp.where(half0, -jnp.inf, dmat[u]), axis=1, keepdims=True)) for u in punits}
    p_in = {u: jnp.exp(dmat[u] - m_loc[u]) for u in punits}
    qk = {u: _dot_nt(q_p[u], k_bd[u]) for u in punits}
    nd = {u: _dot(p_in[u] * qk[u], v_bd[u]) for u in punits}
    kv = {}
    for c, p in punits:
        kw = kt_ref[psl[p], jsl[c]] * jnp.where(top, wk[c, 2 * p], wk[c, 2 * p + 1])
        vv = jnp.concatenate([v_ref[jsl[c], psl[p]].astype(BF16), jnp.ones((2 * L, 128), BF16)], axis=1)
        kv[c, p] = jnp.where(jnp.concatenate([diag, diag], axis=1), _dot(kw, vv), 0.0)

    cn_in = {}
    for p in pairs:
        cn = cn_ref[p]
        for c in range(nck):
            cn_in[c, p] = cn.astype(BF16)
            cn = jnp.concatenate([dec[c, 2 * p] * cn[0:L], dec[c, 2 * p + 1] * cn[L:2 * L]], axis=0) + kv[c, p]
        cn_ref[p] = cn

    inter = {(c, p): bcb[c, p] + pair_tile(m_in, c, p) for c, p in punits}
    mj = {u: jnp.maximum(m_loc[u], inter[u]) for u in punits}
    e_loc = {u: jnp.exp(m_loc[u] - mj[u]) for u in punits}
    e_int = {u: jnp.exp(inter[u] - mj[u]) for u in punits}
    e_neg = {u: jnp.exp(-mj[u]) for u in punits}
    qc = {u: _dot(q_p[u], cn_in[u]) for u in punits}
    num = {u: e_loc[u] * nd[u][:, 0:128] + e_int[u] * qc[u][:, 0:128] for u in punits}
    den = {u: e_loc[u] * nd[u][:, 128:256] + e_int[u] * qc[u][:, 128:256] for u in punits}
    hv = {u: num[u] / jnp.maximum(jnp.abs(den[u]), e_neg[u]) for u in punits}
    hv2 = {u: hv[u] * hv[u] for u in punits}
    ms = {u: jnp.where(half0, jnp.sum(jnp.where(half0, hv2[u], 0.0), axis=1, keepdims=True),
                       jnp.sum(jnp.where(half0, 0.0, hv2[u]), axis=1, keepdims=True)) * (1.0 / HEAD_DIM)
          for u in punits}
    for c, p in punits:
        hn = hv[c, p] * lax.rsqrt(ms[c, p] + RMS_EPS)
        o_ref[rsl[c], psl[p]] = hn * nrm_ref[0:1, psl[p]] * jax.nn.sigmoid(og_ref[rsl[c], psl[p]])


def _mlstm(zm, zg, ktg, bcol, brow, nrm, *, tt):
    b, t, _ = zm.shape
    nt = t // tt
    colblk = lambda j: pl.BlockSpec((None, tt, MLSTM_WIDTH), lambda bb, ti, j=j: (bb, ti, j))
    full = lambda bb, ti: (0, 0)
    return pl.pallas_call(
        functools.partial(_mlstm_kernel, tt=tt),
        grid=(b, nt),
        in_specs=[
            colblk(0), colblk(1), colblk(2), colblk(3),
            pl.BlockSpec((None, tt, ZG_COLS), lambda bb, ti: (bb, ti, 0)),
            pl.BlockSpec((KT_ROWS, tt), lambda bb, ti: (0, bb * nt + ti)),
            pl.BlockSpec((8, 128), full),
            pl.BlockSpec((8, 128), full),
            pl.BlockSpec((8, MLSTM_WIDTH), full),
        ],
        out_specs=pl.BlockSpec((None, tt, MLSTM_WIDTH), lambda bb, ti: (bb, ti, 0)),
        out_shape=jax.ShapeDtypeStruct((b, t, MLSTM_WIDTH), F32),
        scratch_shapes=[pltpu.VMEM((MLSTM_HEADS // 2, 2 * HEAD_DIM, 256), F32),
                        pltpu.VMEM((MLSTM_HEADS, 8, 128), F32)],
        compiler_params=_cparams(("parallel", "arbitrary")),
        name="mlstm",
    )(zm, zm, zm, zm, zg, ktg, bcol, brow, nrm)


def _rwkv_kernel(z_ref, mu_ref, pv_ref, wup_ref, aup_ref, gup_ref, seg_ref, o_ref,
                 prev_ref, st_ref, y_ref, *, tt, sb):
    t = pl.program_id(1)

    @pl.when(t == 0)
    def _():
        prev_ref[...] = jnp.zeros_like(prev_ref)
        st_ref[...] = jnp.zeros_like(st_ref)

    L = CHUNK
    W = RWKV_WIDTH
    UR = 2 * L
    SB = sb
    nrb = tt // SB
    nsub = SB // UR
    pv = pv_ref[...]
    w0, a0, k_k, k_a, r_k, ln_w, ln_b = (pv[j:j + 1, :] for j in range(7))
    seg = seg_ref[...]
    mu = mu_ref[...]
    heads = range(RWKV_HEADS)
    hsl = [slice(HEAD_DIM * h, HEAD_DIM * (h + 1)) for h in heads]

    ri = lax.broadcasted_iota(jnp.int32, (UR, UR), 0)
    ci = lax.broadcasted_iota(jnp.int32, (UR, UR), 1)
    same = (ri // L) == (ci // L)
    strict = jnp.where(same & (ri > ci), 1.0, 0.0)
    incl = jnp.where(same & (ri >= ci), 1.0, 0.0)
    eye = jnp.where(ri == ci, 1.0, 0.0)
    mask4 = jnp.concatenate([jnp.concatenate([strict, strict], axis=1),
                             jnp.concatenate([incl, incl], axis=1)], axis=0)
    posr = lax.broadcasted_iota(jnp.int32, (SB, W), 0) & (L - 1)
    row1 = lax.broadcasted_iota(jnp.int32, (SB, 1), 0)
    units = [(h, u) for u in range(nsub) for h in heads]
    nu = range(len(units))
    usl = [slice(UR * u, UR * (u + 1)) for _, u in units]

    prep, mats = {}, {}
    zst = [st_ref[h] for h in heads]

    def prepare(rb):
        rows = slice(SB * rb, SB * (rb + 1))
        z = z_ref[rows, :]
        first = prev_ref[0:1, :] if rb == 0 else z_ref[SB * rb - 1:SB * rb, :]
        zprev = jnp.where(row1 == 0, first, pltpu.roll(z, 1, axis=0))
        yield
        zs = z + (zprev - z) * mu
        yield
        r = zs[:, 0:W]
        k = zs[:, W:2 * W]
        v = zs[:, 2 * W:3 * W]
        wd = zs[:, 3 * W:3 * W + 64]
        ad = zs[:, 3 * W + 64:3 * W + 128]
        gd = zs[:, 3 * W + 128:3 * W + 256]
        log_w = -_softplus(-(w0 + _dot(jnp.tanh(wd), wup_ref[...]))) - 0.5
        yield
        ld = -jnp.exp(log_w)
        yield
        iclr = jax.nn.sigmoid(a0 + _dot(ad, aup_ref[...]))
        yield
        g = _dot(jax.nn.sigmoid(gd), gup_ref[...])
        yield
        kk = k * k_k
        ss = _segsum(kk * kk, seg)
        yield
        kk = kk / jnp.maximum(jnp.sqrt(ss), 1e-12)
        yield
        k_mod = k * (1.0 + (iclr - 1.0) * k_a)
        a_vec = -kk
        b_vec = kk * iclr
        yield
        cum = ld
        d = 1
        while d < L:
            cum = cum + jnp.where(posr >= d, pltpu.roll(cum, d, axis=0), 0.0)
            d *= 2
            yield
        cum_last = jnp.concatenate(
            [jnp.broadcast_to(cum[(c + 1) * L - 1:(c + 1) * L, :], (L, W)) for c in range(SB // L)], axis=0)
        e_neg = jnp.exp(-cum)
        yield
        e_tail = jnp.exp(cum_last - cum)
        yield
        a_t = jnp.exp(cum - ld) * a_vec
        yield
        r_t = jnp.exp(cum) * r
        yield
        b_t = e_neg * b_vec
        k_t = e_neg * k_mod
        yield
        b_h = e_tail * b_vec
        k_h = e_tail * k_mod
        yield
        bh_t, kh_t, cum_t = [], [], []
        for pair in range(RWKV_HEADS // 2):
            ps = slice(128 * pair, 128 * pair + 128)
            bp, kp, cp = b_h[:, ps].T.astype(BF16), k_h[:, ps].T.astype(BF16), cum[:, ps].T
            for e in range(2):
                es = slice(HEAD_DIM * e, HEAD_DIM * (e + 1))
                bh_t.append(bp[es])
                kh_t.append(kp[es])
                cum_t.append(cp[es])
            yield
        prep[rb] = dict(
            at=[a_t[usl[i], hsl[h]].astype(BF16) for i, (h, _) in enumerate(units)],
            rt=[r_t[usl[i], hsl[h]] for i, (h, _) in enumerate(units)],
            bt=[b_t[usl[i], hsl[h]].astype(BF16) for i, (h, _) in enumerate(units)],
            kt=[k_t[usl[i], hsl[h]].astype(BF16) for i, (h, _) in enumerate(units)],
            vu=[v[usl[i], hsl[h]].astype(BF16) for i, (h, _) in enumerate(units)],
            bh_t=bh_t, kh_t=kh_t, cum_t=cum_t,
            bonus_in=r * k_mod * r_k, v=v, g=g)
        yield

    def products(rb):
        p = prep[rb]
        at, rt, bt, kt, vu = p["at"], p["rt"], p["bt"], p["kt"], p["vu"]
        prod = [mask4 * _dot_nt(jnp.concatenate([at[i], rt[i].astype(BF16)], axis=0),
                                jnp.concatenate([bt[i], kt[i]], axis=0)) for i in nu]
        yield
        a_ab = [prod[i][0:UR, 0:UR] for i in nu]
        a_ak = [prod[i][0:UR, UR:2 * UR].astype(BF16) for i in nu]
        a_rbk = [prod[i][UR:2 * UR, :].astype(BF16) for i in nu]
        yield
        tm = [eye + a_ab[i] for i in nu]
        pw = [a_ab[i].astype(BF16) for i in nu]
        yield
        pw = [_dot(pw[i], pw[i]).astype(BF16) for i in nu]
        yield
        for _ in range(4):
            res = [_dot(pw[i], jnp.concatenate([pw[i], tm[i].astype(BF16)], axis=1)) for i in nu]
            yield
            pw = [res[i][:, 0:UR].astype(BF16) for i in nu]
            tm = [tm[i] + res[i][:, UR:2 * UR] for i in nu]
            yield
        tm = [tm[i] + _dot(pw[i], tm[i]) for i in nu]
        yield
        akv = [_dot(a_ak[i], vu[i]) for i in nu]
        yield
        tmb = [tm[i].astype(BF16) for i in nu]
        w_all = [_dot(tmb[i], at[i]).astype(BF16) for i in nu]
        yield
        u0_all = [_dot(tmb[i], akv[i]).astype(BF16) for i in nu]
        yield
        q_all = [(rt[i] + _dot(a_rbk[i][:, 0:UR], w_all[i])).astype(BF16) for i in nu]
        yield
        y0_all = [_dot(a_rbk[i], jnp.concatenate([u0_all[i], vu[i]], axis=0)) for i in nu]
        yield
        qc, hm, pcol, y0c = {}, {}, {}, {}
        for i, (h, u) in enumerate(units):
            for cc in range(UR // L):
                c = u * (UR // L) + cc
                ls = slice(cc * L, (cc + 1) * L)
                rs = slice(c * L, (c + 1) * L)
                gm = _dot(p["bh_t"][h][:, rs], w_all[i][ls]).astype(BF16)
                hm[c, h] = _dot(p["bh_t"][h][:, rs], u0_all[i][ls]) + _dot(p["kh_t"][h][:, rs], vu[i][ls])
                pcol[c, h] = jnp.exp(p["cum_t"][h][:, (c + 1) * L - 1:(c + 1) * L])
                qc[c, h] = jnp.concatenate([q_all[i][ls], gm], axis=0)
                y0c[c, h] = y0_all[i][ls]
            if h == RWKV_HEADS - 1:
                yield
        mats[rb] = dict(qc=qc, hm=hm, pcol=pcol, y0c=y0c)
        yield

    def finish(rb):
        m, p = mats[rb], prep[rb]
        for cc in range(SB // L):
            rs = slice(SB * rb + cc * L, SB * rb + (cc + 1) * L)
            for h in heads:
                qz = _dot(m["qc"][cc, h], zst[h])
                y_ref[rs, hsl[h]] = qz[0:L] + m["y0c"][cc, h]
                zst[h] = m["pcol"][cc, h] * zst[h] + qz[L:2 * L] + m["hm"][cc, h]
                yield
        rows = slice(SB * rb, SB * (rb + 1))
        y = y_ref[rows, :]
        mean = _segsum(y, seg) * (1.0 / HEAD_DIM)
        yield
        yc = y - mean
        var = _segsum(yc * yc, seg) * (1.0 / HEAD_DIM)
        yield
        yn = yc * lax.rsqrt(var + GN_EPS) * ln_w + ln_b
        yield
        bonus = _segsum(p["bonus_in"], seg) * p["v"]
        yield
        o_ref[rows, :] = (yn + bonus) * p["g"]
        yield

    _interleave(prepare(0))
    for step in range(1, nrb + 2):
        stage = []
        if step < nrb:
            stage.append(prepare(step))
        if 0 <= step - 1 < nrb:
            stage.append(products(step - 1))
        if 0 <= step - 2 < nrb:
            stage.append(finish(step - 2))
        _interleave(*stage)
    for h in heads:
        st_ref[h] = zst[h]
    prev_ref[...] = jnp.broadcast_to(z_ref[tt - 1:tt, :], prev_ref.shape)


def _rwkv(zr, mu, pv, wup, aup, gup, seg, *, tt, sb):
    b, t, _ = zr.shape
    full = lambda bb, ti: (0, 0)
    return pl.pallas_call(
        functools.partial(_rwkv_kernel, tt=tt, sb=sb),
        grid=(b, t // tt),
        in_specs=[
            pl.BlockSpec((None, tt, N_RWKV_COLS), lambda bb, ti: (bb, ti, 0)),
            pl.BlockSpec((1, N_RWKV_COLS), full),
            pl.BlockSpec((8, RWKV_WIDTH), full),
            pl.BlockSpec((64, RWKV_WIDTH), full),
            pl.BlockSpec((64, RWKV_WIDTH), full),
            pl.BlockSpec((128, RWKV_WIDTH), full),
            pl.BlockSpec((4 * HEAD_DIM, 4 * HEAD_DIM), full),
        ],
        out_specs=pl.BlockSpec((None, tt, RWKV_WIDTH), lambda bb, ti: (bb, ti, 0)),
        out_shape=jax.ShapeDtypeStruct((b, t, RWKV_WIDTH), F32),
        scratch_shapes=[
            pltpu.VMEM((8, N_RWKV_COLS), F32),
            pltpu.VMEM((RWKV_HEADS, HEAD_DIM, HEAD_DIM), F32),
            pltpu.VMEM((tt, RWKV_WIDTH), F32),
        ],
        compiler_params=_cparams(("parallel", "arbitrary")),
        name="rwkv7",
    )(zr, mu, pv, wup, aup, gup, seg)


def _tail_kernel(x_ref, ya_ref, yb_ref, yc_ref, p_ref, g_ref, wm_ref, wi_ref, wo_ref, wg_ref, wp_ref, o_ref,
                 h_ref, *, tf, nsplit, skew):
    c0, c1 = LRU_WIDTH, LRU_WIDTH + MLSTM_WIDTH
    tm = x_ref.shape[0]
    hr = tm // nsplit

    def part(i):
        rows = slice(hr * i, hr * (i + 1))
        ya = jnp.concatenate(
            [jnp.concatenate([ya_ref[j, _perm_rows(m, tm), :] for j in range(LRU_WIDTH // 128)], axis=1)
             for m in range(hr * i // 8, hr * (i + 1) // 8)], axis=0)
        mix = (jnp.dot(ya.astype(BF16), wm_ref[0:c0, :], preferred_element_type=F32)
               + jnp.dot(yb_ref[rows, :].astype(BF16), wm_ref[c0:c1, :], preferred_element_type=F32)
               + jnp.dot(yc_ref[rows, :].astype(BF16), wm_ref[c1:, :], preferred_element_type=F32))
        yield
        x = x_ref[rows, :] + _rms(mix, g_ref[3:4, :])
        yield
        xn = _rms(x, g_ref[4:5, :]).astype(BF16)
        yield
        for c in range(D_FF // tf):
            gate = jnp.dot(xn, wi_ref[:, c * tf:(c + 1) * tf], preferred_element_type=F32)
            up = jnp.dot(xn, wi_ref[:, D_FF + c * tf:D_FF + (c + 1) * tf], preferred_element_type=F32)
            h_ref[rows, c * tf:(c + 1) * tf] = (gate * jax.nn.sigmoid(gate) * up).astype(BF16)
            yield
        y = jnp.dot(h_ref[rows, :], wo_ref[...], preferred_element_type=F32)
        yield
        x = x + 0.5 * _rms(y, g_ref[5:6, :])
        yield
        pgate = jax.nn.sigmoid(
            jnp.dot(_rms(x, g_ref[6:7, :]).astype(BF16), wg_ref[...], preferred_element_type=F32))
        yield
        pe = jnp.dot(p_ref[rows, :].astype(BF16), wp_ref[...], preferred_element_type=F32)
        yield
        o_ref[rows, :] = x + _rms(pgate * pe, g_ref[7:8, :])
        yield

    _interleave(*[part(i) for i in range(nsplit)], skew=skew)


def _tail(x, ya, yb, yc, p, g8, w_mix, w_in, w_out, w_gate, w_proj, *, tm, tf):
    n = x.shape[0]
    row = lambda i: (i, 0)
    const = lambda shape: pl.BlockSpec(shape, lambda i: (0, 0), pipeline_mode=pl.Buffered(1))
    return pl.pallas_call(
        functools.partial(_tail_kernel, tf=tf, nsplit=2, skew=11),
        grid=(n // tm,),
        in_specs=[
            pl.BlockSpec((tm, D_MODEL), row),
            pl.BlockSpec((LRU_WIDTH // 128, tm, 128), lambda i: (0, i, 0)),
            pl.BlockSpec((tm, MLSTM_WIDTH), row),
            pl.BlockSpec((tm, RWKV_WIDTH), row),
            pl.BlockSpec((tm, PLE_DIM), row),
            const((8, D_MODEL)),
            const((D_MODEL, D_MODEL)),
            const((D_MODEL, 2 * D_FF)),
            const((D_FF, D_MODEL)),
            const((D_MODEL, D_MODEL)),
            const((PLE_DIM, D_MODEL)),
        ],
        out_specs=pl.BlockSpec((tm, D_MODEL), row),
        out_shape=jax.ShapeDtypeStruct((n, D_MODEL), F32),
        scratch_shapes=[pltpu.VMEM((tm, D_FF), BF16)],
        compiler_params=_cparams(("parallel",)),
        name="tail",
    )(x, ya, yb, yc, p, g8, w_mix, w_in, w_out, w_gate, w_proj)


def _block_diag(w):
    nh, dd, _ = w.shape
    eye = jnp.eye(nh, dtype=w.dtype)
    return (eye[:, None, :, None] * w[:, :, None, :]).reshape(nh * dd, nh * dd)


def _tile(n, pref):
    return pref if n % pref == 0 else n


def kernel(x, p, norm_g, ffn_w_in, ffn_w_out, w_in, w_out, lru_conv_w, lru_conv_b, lru_w_a, lru_b_a, lru_w_x, lru_b_x, lru_lambda, m_b_i, m_b_f, m_norm, rw_mu, rw_w0, rw_w_up, rw_a0, rw_a_up, rw_g_up, rw_k_k, rw_k_a, rw_r_k, rw_ln_w, rw_ln_b, ple_w_proj, ple_w_gate):
    bsz, t, _ = x.shape
    n = bsz * t
    depth = norm_g.shape[0]
    tm_ffn = _tile(n, 512)
    tm = _tile(n, 512)
    tt_lru = _tile(t, 512)
    tt_m = _tile(t, 1024)
    tt_r = _tile(t, 1024)
    seg = _block_diag(jnp.ones((4, HEAD_DIM, HEAD_DIM), BF16))

    xf = x.reshape(n, D_MODEL)
    for l in range(depth):
        g = norm_g[l]
        wl = w_in[l]
        o_zr = 2 * LRU_WIDTH + 4 * MLSTM_WIDTH + 2 * MLSTM_HEADS
        w_gates = wl[:, o_zr - 2 * MLSTM_HEADS:o_zr]
        wcat = jnp.concatenate(
            [wl[:, o_zr:], jnp.pad(w_gates, ((0, 0), (0, ZG_COLS - 2 * MLSTM_HEADS))),
             wl[:, 0:ZL_COLS], wl[:, ZL_COLS:ZL_COLS + ZM_COLS]], axis=1).astype(BF16)

        xf = _ffn(xf, g[0:2], ffn_w_in[l, 0].astype(BF16), ffn_w_out[l, 0].astype(BF16), tm=tm_ffn, tf=256)

        zr, zl, zm, zg, ktg = _mixer_in(xf, g[2:3], wcat, tm=tm)

        lru_pv = jnp.concatenate([lru_conv_w[l], lru_conv_b[l][None], lru_b_a[l][None], lru_b_x[l][None],
                                  lru_lambda[l][None]], axis=0)
        ya = _lru(zl.reshape(ZL_COLS // 128, bsz, t, 128), lru_pv, _block_diag(lru_w_a[l]).astype(BF16),
                  _block_diag(lru_w_x[l]).astype(BF16), tt=tt_lru)

        gate_b = jnp.concatenate([m_b_i[l], m_b_f[l]])
        bcol = jnp.broadcast_to(gate_b[:, None], (8, 128))
        brow = jnp.broadcast_to(jnp.pad(gate_b, (0, ZG_COLS - 8))[None, :], (8, ZG_COLS))
        nrm = jnp.broadcast_to(m_norm[l][None, :], (8, MLSTM_WIDTH))
        yb = _mlstm(zm.reshape(bsz, t, ZM_COLS), zg.reshape(bsz, t, ZG_COLS), ktg, bcol, brow, nrm, tt=tt_m)

        rw_pv = jnp.stack([rw_w0[l], rw_a0[l], rw_k_k[l], rw_k_a[l], rw_r_k[l].reshape(-1), rw_ln_w[l],
                           rw_ln_b[l], jnp.zeros((RWKV_WIDTH,), F32)], axis=0)
        yc = _rwkv(zr.reshape(bsz, t, ZR_COLS), rw_mu[l][None, :], rw_pv, rw_w_up[l].astype(BF16),
                   rw_a_up[l].astype(BF16), rw_g_up[l].astype(BF16), seg, tt=tt_r, sb=min(tt_r, 256))

        xf = _tail(xf, ya.reshape(LRU_WIDTH // 128, n, 128), yb.reshape(n, MLSTM_WIDTH), yc.reshape(n, RWKV_WIDTH),
                   p[l].reshape(n, PLE_DIM), g, w_out[l].astype(BF16), ffn_w_in[l, 1].astype(BF16),
                   ffn_w_out[l, 1].astype(BF16), ple_w_gate[l].astype(BF16), ple_w_proj[l].astype(BF16),
                   tm=tm_ffn, tf=256)
    return xf.reshape(bsz, t, D_MODEL)
```

```python
import functools

import jax
import jax.numpy as jnp
from jax import lax
from jax.experimental import pallas as pl
from jax.experimental.pallas import tpu as pltpu

F32 = jnp.float32
BF16 = jnp.bfloat16

D_MODEL = 1024
PLE_DIM = 256
D_FF = 2816
RMS_EPS = 1e-6
LRU_WIDTH = 384
CONV_WIDTH = 4
LRU_C = 8.0
MLSTM_HEADS = 4
MLSTM_WIDTH = 256
RWKV_HEADS = 6
RWKV_WIDTH = 384
N_RWKV_COLS = 1408
GN_EPS = 64e-5
HEAD_DIM = 64
CHUNK = 64

ZR_COLS = N_RWKV_COLS
ZL_COLS = 2 * LRU_WIDTH
ZM_COLS = 4 * MLSTM_WIDTH
ZG_COLS = 128
KT_ROWS = MLSTM_WIDTH + 8

VMEM_LIMIT = 56 * 1024 * 1024


def _cparams(sem):
    return pltpu.CompilerParams(dimension_semantics=sem, vmem_limit_bytes=VMEM_LIMIT)


def _rms(x, g):
    return x * lax.rsqrt(jnp.mean(x * x, axis=-1, keepdims=True) + RMS_EPS) * g


def _softplus(x):
    return jnp.maximum(x, 0.0) + jnp.log(1.0 + jnp.exp(-jnp.abs(x)))


def _log_sigmoid(x):
    return -_softplus(-x)


def _dot(a, b):
    return jnp.dot(a.astype(BF16), b.astype(BF16), preferred_element_type=F32)


def _dot_nt(a, b):
    return lax.dot_general(a.astype(BF16), b.astype(BF16), (((1,), (1,)), ((), ())),
                           preferred_element_type=F32)


def _interleave(*gens, skew=0):
    pending, active, rounds = list(gens), [], 0
    while pending or active:
        if pending and skew == 0:
            active, pending = active + pending, []
        elif pending and rounds % skew == 0:
            active.append(pending.pop(0))
        for gen in list(active):
            try:
                next(gen)
            except StopIteration:
                active.remove(gen)
        rounds += 1


def _perm_rows(m, tile):
    ng = tile // 8
    s, g0 = divmod(8 * m, ng)
    return pl.ds(8 * g0 + s, 8, stride=8)


def _segsum(x, seg):
    hi = x.astype(BF16)
    lo = (x - hi.astype(F32)).astype(BF16)
    w = seg.shape[0]
    head = (jnp.dot(hi[:, 0:w], seg, preferred_element_type=F32)
            + jnp.dot(lo[:, 0:w], seg, preferred_element_type=F32))
    tail = jnp.dot(jnp.concatenate([hi[:, w:], lo[:, w:]], axis=1), seg, preferred_element_type=F32)
    return jnp.concatenate([head, tail[:, 0:w // 2] + tail[:, w // 2:]], axis=1)


def _ffn_kernel(x_ref, g_ref, wi_ref, wo_ref, o_ref, h_ref, *, tf, nsplit, skew):
    hr = x_ref.shape[0] // nsplit

    def part(i):
        rows = slice(hr * i, hr * (i + 1))
        x = x_ref[rows, :]
        xn = _rms(x, g_ref[0:1, :]).astype(BF16)
        yield
        for c in range(D_FF // tf):
            gate = jnp.dot(xn, wi_ref[:, c * tf:(c + 1) * tf], preferred_element_type=F32)
            up = jnp.dot(xn, wi_ref[:, D_FF + c * tf:D_FF + (c + 1) * tf], preferred_element_type=F32)
            h_ref[rows, c * tf:(c + 1) * tf] = (gate * jax.nn.sigmoid(gate) * up).astype(BF16)
            yield
        y = jnp.dot(h_ref[rows, :], wo_ref[...], preferred_element_type=F32)
        yield
        o_ref[rows, :] = x + 0.5 * _rms(y, g_ref[1:2, :])
        yield

    _interleave(*[part(i) for i in range(nsplit)], skew=skew)


def _ffn(x, g2, w_in, w_out, *, tm, tf):
    n = x.shape[0]
    const = dict(pipeline_mode=pl.Buffered(1))
    return pl.pallas_call(
        functools.partial(_ffn_kernel, tf=tf, nsplit=2, skew=6),
        grid=(n // tm,),
        in_specs=[
            pl.BlockSpec((tm, D_MODEL), lambda i: (i, 0)),
            pl.BlockSpec((2, D_MODEL), lambda i: (0, 0)),
            pl.BlockSpec((D_MODEL, 2 * D_FF), lambda i: (0, 0), **const),
            pl.BlockSpec((D_FF, D_MODEL), lambda i: (0, 0), **const),
        ],
        out_specs=pl.BlockSpec((tm, D_MODEL), lambda i: (i, 0)),
        out_shape=jax.ShapeDtypeStruct((n, D_MODEL), F32),
        scratch_shapes=[pltpu.VMEM((tm, D_FF), BF16)],
        compiler_params=_cparams(("parallel",)),
        name="ffn",
    )(x, g2, w_in, w_out)


def _mixin_kernel(x_ref, g_ref, w_ref, zr_ref, zl_ref, zm_ref, zg_ref, kt_ref):
    xn = _rms(x_ref[...], g_ref[...]).astype(BF16)
    c0, c1, c2 = ZR_COLS + ZG_COLS, ZR_COLS + ZG_COLS + ZL_COLS, ZR_COLS + ZG_COLS + ZL_COLS + ZM_COLS
    zrg = jnp.dot(xn, w_ref[:, 0:c0], preferred_element_type=F32)
    zg = zrg[:, ZR_COLS:c0]
    zr_ref[...] = zrg[:, 0:ZR_COLS]
    zg_ref[...] = zg
    zl = jnp.dot(xn, w_ref[:, c0:c1], preferred_element_type=F32)
    for m in range(zl.shape[0] // 8):
        rows = _perm_rows(m, zl.shape[0])
        for j in range(ZL_COLS // 128):
            zl_ref[j, rows, :] = zl[8 * m:8 * m + 8, 128 * j:128 * (j + 1)]
    zm = jnp.dot(xn, w_ref[:, c1:c2], preferred_element_type=F32)
    zm_ref[...] = zm
    kt_ref[0:MLSTM_WIDTH, :] = zm[:, MLSTM_WIDTH:2 * MLSTM_WIDTH].T
    kt_ref[MLSTM_WIDTH:KT_ROWS, :] = zg.T[0:KT_ROWS - MLSTM_WIDTH, :]


def _mixer_in(x, g, wcat, *, tm):
    n = x.shape[0]
    ncols = wcat.shape[1]
    row = lambda i: (i, 0)
    return pl.pallas_call(
        _mixin_kernel,
        grid=(n // tm,),
        in_specs=[
            pl.BlockSpec((tm, D_MODEL), row),
            pl.BlockSpec((1, D_MODEL), lambda i: (0, 0)),
            pl.BlockSpec((D_MODEL, ncols), lambda i: (0, 0)),
        ],
        out_specs=[
            pl.BlockSpec((tm, ZR_COLS), row),
            pl.BlockSpec((ZL_COLS // 128, tm, 128), lambda i: (0, i, 0)),
            pl.BlockSpec((tm, ZM_COLS), row),
            pl.BlockSpec((tm, ZG_COLS), row),
            pl.BlockSpec((KT_ROWS, tm), lambda i: (0, i)),
        ],
        out_shape=[
            jax.ShapeDtypeStruct((n, ZR_COLS), F32),
            jax.ShapeDtypeStruct((ZL_COLS // 128, n, 128), F32),
            jax.ShapeDtypeStruct((n, ZM_COLS), F32),
            jax.ShapeDtypeStruct((n, ZG_COLS), F32),
            jax.ShapeDtypeStruct((KT_ROWS, n), F32),
        ],
        compiler_params=_cparams(("parallel",)),
        name="mixer_in",
    )(x, g, wcat)


def _lru_kernel(z_ref, pv_ref, wa_ref, wx_ref, o_ref, xtail_ref, h_ref, *, tt):
    t = pl.program_id(1)

    @pl.when(t == 0)
    def _():
        xtail_ref[...] = jnp.zeros_like(xtail_ref)
        h_ref[...] = jnp.zeros_like(h_ref)

    ng = tt // 8
    nslab = LRU_WIDTH // 128
    taps = CONV_WIDTH - 1
    sub = lax.broadcasted_iota(jnp.int32, (8, LRU_WIDTH), 0)

    def load(first_slab, g):
        return jnp.concatenate([z_ref[first_slab + j, 8 * g:8 * g + 8, :] for j in range(nslab)], axis=1)

    x = [load(0, g) for g in range(ng)]
    pv = pv_ref[...]

    def before(g):
        if g >= 0:
            return x[g]
        cur = pltpu.roll(x[ng + g], 1, axis=0)
        prev = pltpu.roll(xtail_ref[taps + g], 1, axis=0)
        return jnp.where(sub == 0, prev, cur)

    nhalf = 2
    parts = {}

    def front(i):
        pieces = range(ng // nhalf * i, ng // nhalf * (i + 1))
        xa = []
        for g in pieces:
            acc = pv[taps + 1:taps + 2, :] + pv[taps:taps + 1, :] * x[g]
            for d in range(1, taps + 1):
                acc = acc + pv[taps - d:taps - d + 1, :] * before(g - d)
            xa.append(acc)
        xa = jnp.concatenate(xa, axis=0)
        yield
        r = jax.nn.sigmoid(_dot(xa, wa_ref[...]) + pv[5:6, :])
        i_gate = jax.nn.sigmoid(_dot(xa, wx_ref[...]) + pv[6:7, :])
        yield
        log_a = (-LRU_C) * r * _softplus(-pv[7:8, :])
        a_i = jnp.exp(log_a)
        yield
        u_i = jnp.sqrt(1.0 - jnp.exp(2.0 * log_a)) * (i_gate * xa)
        yield
        gate = jnp.concatenate([load(nslab, g) for g in pieces], axis=0)
        parts[i] = (a_i, u_i, jax.nn.gelu(gate, approximate=True))
        yield

    _interleave(*[front(i) for i in range(nhalf)], skew=1)
    for d in range(taps):
        xtail_ref[d] = x[ng - taps + d]
    a = jnp.concatenate([parts[i][0] for i in range(nhalf)], axis=0)
    u = jnp.concatenate([parts[i][1] for i in range(nhalf)], axis=0)
    gg = jnp.concatenate([parts[i][2] for i in range(nhalf)], axis=0)

    h_loc, a_cum = [], []
    h = u[0:8, :]
    ac = a[0:8, :]
    h_loc.append(h)
    a_cum.append(ac)
    for g in range(1, ng):
        a_g = a[8 * g:8 * g + 8, :]
        h = a_g * h + u[8 * g:8 * g + 8, :]
        ac = a_g * ac
        h_loc.append(h)
        a_cum.append(ac)
    ae, he = ac, h
    for d in (1, 2, 4):
        keep = sub >= d
        a_sh = jnp.where(keep, pltpu.roll(ae, d, axis=0), 1.0)
        h_sh = jnp.where(keep, pltpu.roll(he, d, axis=0), 0.0)
        he = ae * h_sh + he
        ae = ae * a_sh
    h0 = h_ref[0:1, :]
    h_end = he + ae * h0
    h_ref[...] = jnp.broadcast_to(h_end[7:8, :], h_ref.shape)
    c_in = jnp.where(sub == 0, h0, pltpu.roll(h_end, 1, axis=0))
    for g in range(ng):
        out = (h_loc[g] + a_cum[g] * c_in) * gg[8 * g:8 * g + 8, :]
        for j in range(nslab):
            o_ref[j, 8 * g:8 * g + 8, :] = out[:, 128 * j:128 * (j + 1)]


def _lru(zl, pv, wa, wx, *, tt):
    _, b, t, _ = zl.shape
    nslab = LRU_WIDTH // 128
    full = lambda bb, ti: (0, 0)
    return pl.pallas_call(
        functools.partial(_lru_kernel, tt=tt),
        grid=(b, t // tt),
        in_specs=[
            pl.BlockSpec((2 * nslab, None, tt, 128), lambda bb, ti: (0, bb, ti, 0)),
            pl.BlockSpec((8, LRU_WIDTH), full),
            pl.BlockSpec((LRU_WIDTH, LRU_WIDTH), full),
            pl.BlockSpec((LRU_WIDTH, LRU_WIDTH), full),
        ],
        out_specs=pl.BlockSpec((nslab, None, tt, 128), lambda bb, ti: (0, bb, ti, 0)),
        out_shape=jax.ShapeDtypeStruct((nslab, b, t, 128), F32),
        scratch_shapes=[pltpu.VMEM((CONV_WIDTH - 1, 8, LRU_WIDTH), F32), pltpu.VMEM((8, LRU_WIDTH), F32)],
        compiler_params=_cparams(("parallel", "arbitrary")),
        name="rglru",
    )(zl, pv, wa, wx)


def _mlstm_kernel(q_ref, k_ref, v_ref, og_ref, gc_ref, kt_ref, bcol_ref, brow_ref, nrm_ref, o_ref,
                  cn_ref, m_ref, *, tt):
    t = pl.program_id(1)

    @pl.when(t == 0)
    def _():
        cn_ref[...] = jnp.zeros_like(cn_ref)
        m_ref[...] = jnp.zeros_like(m_ref)

    L = CHUNK
    gr = kt_ref[MLSTM_WIDTH:MLSTM_WIDTH + 8, :] + bcol_ref[:, 0:1]
    br = _log_sigmoid(gr)
    pos = lax.broadcasted_iota(jnp.int32, (8, tt), 1) & (L - 1)
    d = 1
    while d < L:
        br = br + jnp.where(pos >= d, pltpu.roll(br, d, axis=1), 0.0)
        d *= 2
    bc = _log_sigmoid(gc_ref[...] + brow_ref[0:1, :])
    posc = lax.broadcasted_iota(jnp.int32, (tt, ZG_COLS), 0) & (L - 1)
    d = 1
    while d < L:
        bc = bc + jnp.where(posc >= d, pltpu.roll(bc, d, axis=0), 0.0)
        d *= 2

    nck = tt // L
    heads = range(MLSTM_HEADS)
    pairs = range(MLSTM_HEADS // 2)
    units = [(c, h) for c in range(nck) for h in heads]
    punits = [(c, p) for c in range(nck) for p in pairs]
    rsl = [slice(c * L, (c + 1) * L) for c in range(nck)]
    psl = [slice(128 * p, 128 * p + 128) for p in pairs]
    jsl = [slice(128 * (c // 2), 128 * (c // 2) + 128) for c in range(nck)]

    lane = lax.broadcasted_iota(jnp.int32, (L, 128), 1)
    rowi = lax.broadcasted_iota(jnp.int32, (L, 128), 0)
    half0 = lane < HEAD_DIM
    causal = (lane & (HEAD_DIM - 1)) <= rowi
    lane1 = lax.broadcasted_iota(jnp.int32, (1, 128), 1)
    rmask = [lane1 < L, lane1 >= L]
    r2 = lax.broadcasted_iota(jnp.int32, (2 * L, 128), 0)
    l2 = lax.broadcasted_iota(jnp.int32, (2 * L, 128), 1)
    top = r2 < L
    diag = (r2 < L) == (l2 < HEAD_DIM)
    ones_bd = jnp.where(diag, 1.0, 0.0).astype(BF16)

    b_row = {(c, h): br[4 + h:5 + h, jsl[c]] for c, h in units}
    i_row = {(c, h): gr[h:h + 1, jsl[c]] for c, h in units}

    g = {(c, h): br[4 + h:5 + h, (c + 1) * L - 1:(c + 1) * L] for c, h in units}
    w_log = {u: jnp.where(rmask[u[0] % 2], g[u] - b_row[u] + i_row[u], -jnp.inf) for u in units}
    w_max = {u: jnp.max(w_log[u], axis=1, keepdims=True) for u in units}
    m_in, m_out = {}, {}
    for h in heads:
        m_st = m_ref[h, 0:1, 0:1]
        for c in range(nck):
            m_in[c, h] = m_st
            m_st = jnp.maximum(g[c, h] + m_st, w_max[c, h])
            m_out[c, h] = m_st
        m_ref[h] = jnp.broadcast_to(m_st, (8, 128))
    dec = {u: jnp.exp(g[u] + m_in[u] - m_out[u]) for u in units}
    wk = {u: jnp.exp(w_log[u] - m_out[u]) for u in units}

    def lanes_of_chunk(row128, c, e):
        return row128 if c % 2 == e else pltpu.roll(row128, HEAD_DIM, axis=1)

    def pair_row(d, c, p):
        return jnp.where(rmask[0], lanes_of_chunk(d[c, 2 * p], c, 0), lanes_of_chunk(d[c, 2 * p + 1], c, 1))

    def pair_tile(d, c, p):
        return jnp.where(half0, d[c, 2 * p], d[c, 2 * p + 1])

    q_p = {(c, p): (q_ref[rsl[c], psl[p]] * (HEAD_DIM ** -0.5)).astype(BF16) for c, p in punits}
    k_bd = {}
    v_bd = {}
    for c, p in punits:
        k_c = k_ref[rsl[c], psl[p]]
        v_c = v_ref[rsl[c], psl[p]]
        k_bd[c, p] = jnp.where(diag, jnp.concatenate([k_c, k_c], axis=0), 0.0).astype(BF16)
        v_bd[c, p] = jnp.concatenate([jnp.where(diag, jnp.concatenate([v_c, v_c], axis=0), 0.0).astype(BF16),
                                      ones_bd], axis=1)
    bcb = {(c, p): jnp.where(half0, jnp.broadcast_to(bc[rsl[c], 4 + 2 * p:5 + 2 * p], (L, 128)),
                             jnp.broadcast_to(bc[rsl[c], 5 + 2 * p:6 + 2 * p], (L, 128))) for c, p in punits}
    dmat = {(c, p): jnp.where(causal, bcb[c, p] - pair_row(b_row, c, p) + pair_row(i_row, c, p), -jnp.inf)
            for c, p in punits}
    m_loc = {u: jnp.where(half0, jnp.max(jnp.where(half0, dmat[u], -jnp.inf), axis=1, keepdims=True),
                          jnp.max(jnp.where(half0, -jnp.inf, dmat[u]), axis=1, keepdims=True)) for u in punits}
    p_in = {u: jnp.exp(dmat[u] - m_loc[u]) for u in punits}
    qk = {u: _dot_nt(q_p[u], k_bd[u]) for u in punits}
    nd = {u: _dot(p_in[u] * qk[u], v_bd[u]) for u in punits}
    kv = {}
    for c, p in punits:
        kw = kt_ref[psl[p], jsl[c]] * jnp.where(top, wk[c, 2 * p], wk[c, 2 * p + 1])
        vv = jnp.concatenate([v_ref[jsl[c], psl[p]].astype(BF16), jnp.ones((2 * L, 128), BF16)], axis=1)
        kv[c, p] = jnp.where(jnp.concatenate([diag, diag], axis=1), _dot(kw, vv), 0.0)

    cn_in = {}
    for p in pairs:
        cn = cn_ref[p]
        for c in range(nck):
            cn_in[c, p] = cn.astype(BF16)
            cn = jnp.concatenate([dec[c, 2 * p] * cn[0:L], dec[c, 2 * p + 1] * cn[L:2 * L]], axis=0) + kv[c, p]
        cn_ref[p] = cn

    inter = {(c, p): bcb[c, p] + pair_tile(m_in, c, p) for c, p in punits}
    mj = {u: jnp.maximum(m_loc[u], inter[u]) for u in punits}
    e_loc = {u: jnp.exp(m_loc[u] - mj[u]) for u in punits}
    e_int = {u: jnp.exp(inter[u] - mj[u]) for u in punits}
    e_neg = {u: jnp.exp(-mj[u]) for u in punits}
    qc = {u: _dot(q_p[u], cn_in[u]) for u in punits}
    num = {u: e_loc[u] * nd[u][:, 0:128] + e_int[u] * qc[u][:, 0:128] for u in punits}
    den = {u: e_loc[u] * nd[u][:, 128:256] + e_int[u] * qc[u][:, 128:256] for u in punits}
    hv = {u: num[u] / jnp.maximum(jnp.abs(den[u]), e_neg[u]) for u in punits}
    hv2 = {u: hv[u] * hv[u] for u in punits}
    ms = {u: jnp.where(half0, jnp.sum(jnp.where(half0, hv2[u], 0.0), axis=1, keepdims=True),
                       jnp.sum(jnp.where(half0, 0.0, hv2[u]), axis=1, keepdims=True)) * (1.0 / HEAD_DIM)
          for u in punits}
    for c, p in punits:
        hn = hv[c, p] * lax.rsqrt(ms[c, p] + RMS_EPS)
        o_ref[rsl[c], psl[p]] = hn * nrm_ref[0:1, psl[p]] * jax.nn.sigmoid(og_ref[rsl[c], psl[p]])


def _mlstm(zm, zg, ktg, bcol, brow, nrm, *, tt):
    b, t, _ = zm.shape
    nt = t // tt
    colblk = lambda j: pl.BlockSpec((None, tt, MLSTM_WIDTH), lambda bb, ti, j=j: (bb, ti, j))
    full = lambda bb, ti: (0, 0)
    return pl.pallas_call(
        functools.partial(_mlstm_kernel, tt=tt),
        grid=(b, nt),
        in_specs=[
            colblk(0), colblk(1), colblk(2), colblk(3),
            pl.BlockSpec((None, tt, ZG_COLS), lambda bb, ti: (bb, ti, 0)),
            pl.BlockSpec((KT_ROWS, tt), lambda bb, ti: (0, bb * nt + ti)),
            pl.BlockSpec((8, 128), full),
            pl.BlockSpec((8, 128), full),
            pl.BlockSpec((8, MLSTM_WIDTH), full),
        ],
        out_specs=pl.BlockSpec((None, tt, MLSTM_WIDTH), lambda bb, ti: (bb, ti, 0)),
        out_shape=jax.ShapeDtypeStruct((b, t, MLSTM_WIDTH), F32),
        scratch_shapes=[pltpu.VMEM((MLSTM_HEADS // 2, 2 * HEAD_DIM, 256), F32),
                        pltpu.VMEM((MLSTM_HEADS, 8, 128), F32)],
        compiler_params=_cparams(("parallel", "arbitrary")),
        name="mlstm",
    )(zm, zm, zm, zm, zg, ktg, bcol, brow, nrm)


def _rwkv_kernel(z_ref, mu_ref, pv_ref, wup_ref, aup_ref, gup_ref, seg_ref, o_ref,
                 prev_ref, st_ref, y_ref, *, tt, sb):
    t = pl.program_id(1)

    @pl.when(t == 0)
    def _():
        prev_ref[...] = jnp.zeros_like(prev_ref)
        st_ref[...] = jnp.zeros_like(st_ref)

    L = CHUNK
    W = RWKV_WIDTH
    UR = 2 * L
    SB = sb
    nrb = tt // SB
    nsub = SB // UR
    pv = pv_ref[...]
    w0, a0, k_k, k_a, r_k, ln_w, ln_b = (pv[j:j + 1, :] for j in range(7))
    seg = seg_ref[...]
    mu = mu_ref[...]
    heads = range(RWKV_HEADS)
    hsl = [slice(HEAD_DIM * h, HEAD_DIM * (h + 1)) for h in heads]

    ri = lax.broadcasted_iota(jnp.int32, (UR, UR), 0)
    ci = lax.broadcasted_iota(jnp.int32, (UR, UR), 1)
    same = (ri // L) == (ci // L)
    strict = jnp.where(same & (ri > ci), 1.0, 0.0)
    incl = jnp.where(same & (ri >= ci), 1.0, 0.0)
    eye = jnp.where(ri == ci, 1.0, 0.0)
    mask4 = jnp.concatenate([jnp.concatenate([strict, strict], axis=1),
                             jnp.concatenate([incl, incl], axis=1)], axis=0)
    posr = lax.broadcasted_iota(jnp.int32, (SB, W), 0) & (L - 1)
    row1 = lax.broadcasted_iota(jnp.int32, (SB, 1), 0)
    units = [(h, u) for u in range(nsub) for h in heads]
    nu = range(len(units))
    usl = [slice(UR * u, UR * (u + 1)) for _, u in units]

    prep, mats = {}, {}
    zst = [st_ref[h] for h in heads]

    def prepare(rb):
        rows = slice(SB * rb, SB * (rb + 1))
        z = z_ref[rows, :]
        first = prev_ref[0:1, :] if rb == 0 else z_ref[SB * rb - 1:SB * rb, :]
        zprev = jnp.where(row1 == 0, first, pltpu.roll(z, 1, axis=0))
        yield
        zs = z + (zprev - z) * mu
        yield
        r = zs[:, 0:W]
        k = zs[:, W:2 * W]
        v = zs[:, 2 * W:3 * W]
        wd = zs[:, 3 * W:3 * W + 64]
        ad = zs[:, 3 * W + 64:3 * W + 128]
        gd = zs[:, 3 * W + 128:3 * W + 256]
        log_w = -_softplus(-(w0 + _dot(jnp.tanh(wd), wup_ref[...]))) - 0.5
        yield
        ld = -jnp.exp(log_w)
        yield
        iclr = jax.nn.sigmoid(a0 + _dot(ad, aup_ref[...]))
        yield
        g = _dot(jax.nn.sigmoid(gd), gup_ref[...])
        yield
        kk = k * k_k
        ss = _segsum(kk * kk, seg)
        yield
        kk = kk / jnp.maximum(jnp.sqrt(ss), 1e-12)
        yield
        k_mod = k * (1.0 + (iclr - 1.0) * k_a)
        a_vec = -kk
        b_vec = kk * iclr
        yield
        cum = ld
        d = 1
        while d < L:
            cum = cum + jnp.where(posr >= d, pltpu.roll(cum, d, axis=0), 0.0)
            d *= 2
            yield
        cum_last = jnp.concatenate(
            [jnp.broadcast_to(cum[(c + 1) * L - 1:(c + 1) * L, :], (L, W)) for c in range(SB // L)], axis=0)
        e_neg = jnp.exp(-cum)
        yield
        e_tail = jnp.exp(cum_last - cum)
        yield
        a_t = jnp.exp(cum - ld) * a_vec
        yield
        r_t = jnp.exp(cum) * r
        yield
        b_t = e_neg * b_vec
        k_t = e_neg * k_mod
        yield
        b_h = e_tail * b_vec
        k_h = e_tail * k_mod
        yield
        bh_t, kh_t, cum_t = [], [], []
        for pair in range(RWKV_HEADS // 2):
            ps = slice(128 * pair, 128 * pair + 128)
            bp, kp, cp = b_h[:, ps].T.astype(BF16), k_h[:, ps].T.astype(BF16), cum[:, ps].T
            for e in range(2):
                es = slice(HEAD_DIM * e, HEAD_DIM * (e + 1))
                bh_t.append(bp[es])
                kh_t.append(kp[es])
                cum_t.append(cp[es])
            yield
        prep[rb] = dict(
            at=[a_t[usl[i], hsl[h]].astype(BF16) for i, (h, _) in enumerate(units)],
            rt=[r_t[usl[i], hsl[h]] for i, (h, _) in enumerate(units)],
            bt=[b_t[usl[i], hsl[h]].astype(BF16) for i, (h, _) in enumerate(units)],
            kt=[k_t[usl[i], hsl[h]].astype(BF16) for i, (h, _) in enumerate(units)],
            vu=[v[usl[i], hsl[h]].astype(BF16) for i, (h, _) in enumerate(units)],
            bh_t=bh_t, kh_t=kh_t, cum_t=cum_t,
            bonus_in=r * k_mod * r_k, v=v, g=g)
        yield

    def products(rb):
        p = prep[rb]
        at, rt, bt, kt, vu = p["at"], p["rt"], p["bt"], p["kt"], p["vu"]
        prod = [mask4 * _dot_nt(jnp.concatenate([at[i], rt[i].astype(BF16)], axis=0),
                                jnp.concatenate([bt[i], kt[i]], axis=0)) for i in nu]
        yield
        a_ab = [prod[i][0:UR, 0:UR] for i in nu]
        a_ak = [prod[i][0:UR, UR:2 * UR].astype(BF16) for i in nu]
        a_rbk = [prod[i][UR:2 * UR, :].astype(BF16) for i in nu]
        yield
        tm = [eye + a_ab[i] for i in nu]
        pw = [a_ab[i].astype(BF16) for i in nu]
        yield
        pw = [_dot(pw[i], pw[i]).astype(BF16) for i in nu]
        yield
        for _ in range(4):
            res = [_dot(pw[i], jnp.concatenate([pw[i], tm[i].astype(BF16)], axis=1)) for i in nu]
            yield
            pw = [res[i][:, 0:UR].astype(BF16) for i in nu]
            tm = [tm[i] + res[i][:, UR:2 * UR] for i in nu]
            yield
        tm = [tm[i] + _dot(pw[i], tm[i]) for i in nu]
        yield
        akv = [_dot(a_ak[i], vu[i]) for i in nu]
        yield
        tmb = [tm[i].astype(BF16) for i in nu]
        w_all = [_dot(tmb[i], at[i]).astype(BF16) for i in nu]
        yield
        u0_all = [_dot(tmb[i], akv[i]).astype(BF16) for i in nu]
        yield
        q_all = [(rt[i] + _dot(a_rbk[i][:, 0:UR], w_all[i])).astype(BF16) for i in nu]
        yield
        y0_all = [_dot(a_rbk[i], jnp.concatenate([u0_all[i], vu[i]], axis=0)) for i in nu]
        yield
        qc, hm, pcol, y0c = {}, {}, {}, {}
        for i, (h, u) in enumerate(units):
            for cc in range(UR // L):
                c = u * (UR // L) + cc
                ls = slice(cc * L, (cc + 1) * L)
                rs = slice(c * L, (c + 1) * L)
                gm = _dot(p["bh_t"][h][:, rs], w_all[i][ls]).astype(BF16)
                hm[c, h] = _dot(p["bh_t"][h][:, rs], u0_all[i][ls]) + _dot(p["kh_t"][h][:, rs], vu[i][ls])
                pcol[c, h] = jnp.exp(p["cum_t"][h][:, (c + 1) * L - 1:(c + 1) * L])
                qc[c, h] = jnp.concatenate([q_all[i][ls], gm], axis=0)
                y0c[c, h] = y0_all[i][ls]
            if h == RWKV_HEADS - 1:
                yield
        mats[rb] = dict(qc=qc, hm=hm, pcol=pcol, y0c=y0c)
        yield

    def finish(rb):
        m, p = mats[rb], prep[rb]
        for cc in range(SB // L):
            rs = slice(SB * rb + cc * L, SB * rb + (cc + 1) * L)
            for h in heads:
                qz = _dot(m["qc"][cc, h], zst[h])
                y_ref[rs, hsl[h]] = qz[0:L] + m["y0c"][cc, h]
                zst[h] = m["pcol"][cc, h] * zst[h] + qz[L:2 * L] + m["hm"][cc, h]
                yield
        rows = slice(SB * rb, SB * (rb + 1))
        y = y_ref[rows, :]
        mean = _segsum(y, seg) * (1.0 / HEAD_DIM)
        yield
        yc = y - mean
        var = _segsum(yc * yc, seg) * (1.0 / HEAD_DIM)
        yield
        yn = yc * lax.rsqrt(var + GN_EPS) * ln_w + ln_b
        yield
        bonus = _segsum(p["bonus_in"], seg) * p["v"]
        yield
        o_ref[rows, :] = (yn + bonus) * p["g"]
        yield

    _interleave(prepare(0))
    for step in range(1, nrb + 2):
        stage = []
        if step < nrb:
            stage.append(prepare(step))
        if 0 <= step - 1 < nrb:
            stage.append(products(step - 1))
        if 0 <= step - 2 < nrb:
            stage.append(finish(step - 2))
        _interleave(*stage)
    for h in heads:
        st_ref[h] = zst[h]
    prev_ref[...] = jnp.broadcast_to(z_ref[tt - 1:tt, :], prev_ref.shape)


def _rwkv(zr, mu, pv, wup, aup, gup, seg, *, tt, sb):
    b, t, _ = zr.shape
    full = lambda bb, ti: (0, 0)
    return pl.pallas_call(
        functools.partial(_rwkv_kernel, tt=tt, sb=sb),
        grid=(b, t // tt),
        in_specs=[
            pl.BlockSpec((None, tt, N_RWKV_COLS), lambda bb, ti: (bb, ti, 0)),
            pl.BlockSpec((1, N_RWKV_COLS), full),
            pl.BlockSpec((8, RWKV_WIDTH), full),
            pl.BlockSpec((64, RWKV_WIDTH), full),
            pl.BlockSpec((64, RWKV_WIDTH), full),
            pl.BlockSpec((128, RWKV_WIDTH), full),
            pl.BlockSpec((4 * HEAD_DIM, 4 * HEAD_DIM), full),
        ],
        out_specs=pl.BlockSpec((None, tt, RWKV_WIDTH), lambda bb, ti: (bb, ti, 0)),
        out_shape=jax.ShapeDtypeStruct((b, t, RWKV_WIDTH), F32),
        scratch_shapes=[
            pltpu.VMEM((8, N_RWKV_COLS), F32),
            pltpu.VMEM((RWKV_HEADS, HEAD_DIM, HEAD_DIM), F32),
            pltpu.VMEM((tt, RWKV_WIDTH), F32),
        ],
        compiler_params=_cparams(("parallel", "arbitrary")),
        name="rwkv7",
    )(zr, mu, pv, wup, aup, gup, seg)


def _tail_kernel(x_ref, ya_ref, yb_ref, yc_ref, p_ref, g_ref, wm_ref, wi_ref, wo_ref, wg_ref, wp_ref, o_ref,
                 h_ref, *, tf, nsplit, skew):
    c0, c1 = LRU_WIDTH, LRU_WIDTH + MLSTM_WIDTH
    tm = x_ref.shape[0]
    hr = tm // nsplit

    def part(i):
        rows = slice(hr * i, hr * (i + 1))
        ya = jnp.concatenate(
            [jnp.concatenate([ya_ref[j, _perm_rows(m, tm), :] for j in range(LRU_WIDTH // 128)], axis=1)
             for m in range(hr * i // 8, hr * (i + 1) // 8)], axis=0)
        mix = (jnp.dot(ya.astype(BF16), wm_ref[0:c0, :], preferred_element_type=F32)
               + jnp.dot(yb_ref[rows, :].astype(BF16), wm_ref[c0:c1, :], preferred_element_type=F32)
               + jnp.dot(yc_ref[rows, :].astype(BF16), wm_ref[c1:, :], preferred_element_type=F32))
        yield
        x = x_ref[rows, :] + _rms(mix, g_ref[3:4, :])
        yield
        xn = _rms(x, g_ref[4:5, :]).astype(BF16)
        yield
        for c in range(D_FF // tf):
            gate = jnp.dot(xn, wi_ref[:, c * tf:(c + 1) * tf], preferred_element_type=F32)
            up = jnp.dot(xn, wi_ref[:, D_FF + c * tf:D_FF + (c + 1) * tf], preferred_element_type=F32)
            h_ref[rows, c * tf:(c + 1) * tf] = (gate * jax.nn.sigmoid(gate) * up).astype(BF16)
            yield
        y = jnp.dot(h_ref[rows, :], wo_ref[...], preferred_element_type=F32)
        yield
        x = x + 0.5 * _rms(y, g_ref[5:6, :])
        yield
        pgate = jax.nn.sigmoid(
            jnp.dot(_rms(x, g_ref[6:7, :]).astype(BF16), wg_ref[...], preferred_element_type=F32))
        yield
        pe = jnp.dot(p_ref[rows, :].astype(BF16), wp_ref[...], preferred_element_type=F32)
        yield
        o_ref[rows, :] = x + _rms(pgate * pe, g_ref[7:8, :])
        yield

    _interleave(*[part(i) for i in range(nsplit)], skew=skew)


def _tail(x, ya, yb, yc, p, g8, w_mix, w_in, w_out, w_gate, w_proj, *, tm, tf):
    n = x.shape[0]
    row = lambda i: (i, 0)
    const = lambda shape: pl.BlockSpec(shape, lambda i: (0, 0), pipeline_mode=pl.Buffered(1))
    return pl.pallas_call(
        functools.partial(_tail_kernel, tf=tf, nsplit=2, skew=11),
        grid=(n // tm,),
        in_specs=[
            pl.BlockSpec((tm, D_MODEL), row),
            pl.BlockSpec((LRU_WIDTH // 128, tm, 128), lambda i: (0, i, 0)),
            pl.BlockSpec((tm, MLSTM_WIDTH), row),
            pl.BlockSpec((tm, RWKV_WIDTH), row),
            pl.BlockSpec((tm, PLE_DIM), row),
            const((8, D_MODEL)),
            const((D_MODEL, D_MODEL)),
            const((D_MODEL, 2 * D_FF)),
            const((D_FF, D_MODEL)),
            const((D_MODEL, D_MODEL)),
            const((PLE_DIM, D_MODEL)),
        ],
        out_specs=pl.BlockSpec((tm, D_MODEL), row),
        out_shape=jax.ShapeDtypeStruct((n, D_MODEL), F32),
        scratch_shapes=[pltpu.VMEM((tm, D_FF), BF16)],
        compiler_params=_cparams(("parallel",)),
        name="tail",
    )(x, ya, yb, yc, p, g8, w_mix, w_in, w_out, w_gate, w_proj)


def _block_diag(w):
    nh, dd, _ = w.shape
    eye = jnp.eye(nh, dtype=w.dtype)
    return (eye[:, None, :, None] * w[:, :, None, :]).reshape(nh * dd, nh * dd)


def _tile(n, pref):
    return pref if n % pref == 0 else n


def kernel(x, p, norm_g, ffn_w_in, ffn_w_out, w_in, w_out, lru_conv_w, lru_conv_b, lru_w_a, lru_b_a, lru_w_x, lru_b_x, lru_lambda, m_b_i, m_b_f, m_norm, rw_mu, rw_w0, rw_w_up, rw_a0, rw_a_up, rw_g_up, rw_k_k, rw_k_a, rw_r_k, rw_ln_w, rw_ln_b, ple_w_proj, ple_w_gate):
    bsz, t, _ = x.shape
    n = bsz * t
    depth = norm_g.shape[0]
    tm = _tile(n, 512)
    tt_lru = _tile(t, 512)
    tt_m = _tile(t, 1024)
    tt_r = _tile(t, 1024)
    assert tm == tt_lru and t % tt_lru == 0, (tm, tt_lru, t)
    seg = _block_diag(jnp.ones((4, HEAD_DIM, HEAD_DIM), BF16))

    xf = x.reshape(n, D_MODEL)
    for l in range(depth):
        g = norm_g[l]
        wl = w_in[l]
        o_zr = 2 * LRU_WIDTH + 4 * MLSTM_WIDTH + 2 * MLSTM_HEADS
        w_gates = wl[:, o_zr - 2 * MLSTM_HEADS:o_zr]
        wcat = jnp.concatenate(
            [wl[:, o_zr:], jnp.pad(w_gates, ((0, 0), (0, ZG_COLS - 2 * MLSTM_HEADS))),
             wl[:, 0:ZL_COLS], wl[:, ZL_COLS:ZL_COLS + ZM_COLS]], axis=1).astype(BF16)

        xf = _ffn(xf, g[0:2], ffn_w_in[l, 0].astype(BF16), ffn_w_out[l, 0].astype(BF16), tm=tm, tf=256)

        zr, zl, zm, zg, ktg = _mixer_in(xf, g[2:3], wcat, tm=tm)

        lru_pv = jnp.concatenate([lru_conv_w[l], lru_conv_b[l][None], lru_b_a[l][None], lru_b_x[l][None],
                                  lru_lambda[l][None]], axis=0)
        ya = _lru(zl.reshape(ZL_COLS // 128, bsz, t, 128), lru_pv, _block_diag(lru_w_a[l]).astype(BF16),
                  _block_diag(lru_w_x[l]).astype(BF16), tt=tt_lru)

        gate_b = jnp.concatenate([m_b_i[l], m_b_f[l]])
        bcol = jnp.broadcast_to(gate_b[:, None], (8, 128))
        brow = jnp.broadcast_to(jnp.pad(gate_b, (0, ZG_COLS - 8))[None, :], (8, ZG_COLS))
        nrm = jnp.broadcast_to(m_norm[l][None, :], (8, MLSTM_WIDTH))
        yb = _mlstm(zm.reshape(bsz, t, ZM_COLS), zg.reshape(bsz, t, ZG_COLS), ktg, bcol, brow, nrm, tt=tt_m)

        rw_pv = jnp.stack([rw_w0[l], rw_a0[l], rw_k_k[l], rw_k_a[l], rw_r_k[l].reshape(-1), rw_ln_w[l],
                           rw_ln_b[l], jnp.zeros((RWKV_WIDTH,), F32)], axis=0)
        yc = _rwkv(zr.reshape(bsz, t, ZR_COLS), rw_mu[l][None, :], rw_pv, rw_w_up[l].astype(BF16),
                   rw_a_up[l].astype(BF16), rw_g_up[l].astype(BF16), seg, tt=tt_r, sb=min(tt_r, 256))

        xf = _tail(xf, ya.reshape(LRU_WIDTH // 128, n, 128), yb.reshape(n, MLSTM_WIDTH), yc.reshape(n, RWKV_WIDTH),
                   p[l].reshape(n, PLE_DIM), g, w_out[l].astype(BF16), ffn_w_in[l, 1].astype(BF16),
                   ffn_w_out[l, 1].astype(BF16), ple_w_gate[l].astype(BF16), ple_w_proj[l].astype(BF16),
                   tm=tm, tf=256)
    return xf.reshape(bsz, t, D_MODEL)
```

```python
import functools

import jax
import jax.numpy as jnp
from jax import lax
from jax.experimental import pallas as pl
from jax.experimental.pallas import tpu as pltpu

F32 = jnp.float32
BF16 = jnp.bfloat16

D_MODEL = 1024
PLE_DIM = 256
D_FF = 2816
RMS_EPS = 1e-6
LRU_WIDTH = 384
CONV_WIDTH = 4
LRU_C = 8.0
MLSTM_HEADS = 4
MLSTM_WIDTH = 256
RWKV_HEADS = 6
RWKV_WIDTH = 384
N_RWKV_COLS = 1408
GN_EPS = 64e-5
HEAD_DIM = 64
CHUNK = 64

ZR_COLS = N_RWKV_COLS
ZL_COLS = 2 * LRU_WIDTH
ZM_COLS = 4 * MLSTM_WIDTH
ZG_COLS = 128
KT_ROWS = MLSTM_WIDTH + 8

VMEM_LIMIT = 56 * 1024 * 1024


def _cparams(sem):
    return pltpu.CompilerParams(dimension_semantics=sem, vmem_limit_bytes=VMEM_LIMIT)


def _rms(x, g):
    return x * lax.rsqrt(jnp.mean(x * x, axis=-1, keepdims=True) + RMS_EPS) * g


def _softplus(x):
    return jnp.maximum(x, 0.0) + jnp.log(1.0 + jnp.exp(-jnp.abs(x)))


def _log_sigmoid(x):
    return -_softplus(-x)


def _dot(a, b):
    return jnp.dot(a.astype(BF16), b.astype(BF16), preferred_element_type=F32)


def _dot_nt(a, b):
    return lax.dot_general(a.astype(BF16), b.astype(BF16), (((1,), (1,)), ((), ())),
                           preferred_element_type=F32)


def _interleave(*gens, skew=0):
    pending, active, rounds = list(gens), [], 0
    while pending or active:
        if pending and skew == 0:
            active, pending = active + pending, []
        elif pending and rounds % skew == 0:
            active.append(pending.pop(0))
        for gen in list(active):
            try:
                next(gen)
            except StopIteration:
                active.remove(gen)
        rounds += 1


def _perm_rows(m, tile):
    ng = tile // 8
    s, g0 = divmod(8 * m, ng)
    return pl.ds(8 * g0 + s, 8, stride=8)


def _segsum(x, seg):
    hi = x.astype(BF16)
    lo = (x - hi.astype(F32)).astype(BF16)
    w = seg.shape[0]
    head = (jnp.dot(hi[:, 0:w], seg, preferred_element_type=F32)
            + jnp.dot(lo[:, 0:w], seg, preferred_element_type=F32))
    tail = jnp.dot(jnp.concatenate([hi[:, w:], lo[:, w:]], axis=1), seg, preferred_element_type=F32)
    return jnp.concatenate([head, tail[:, 0:w // 2] + tail[:, w // 2:]], axis=1)


def _ffn_kernel(x_ref, g_ref, wi_ref, wo_ref, o_ref, h_ref, *, tf, nsplit, skew):
    hr = x_ref.shape[0] // nsplit

    def part(i):
        rows = slice(hr * i, hr * (i + 1))
        x = x_ref[rows, :]
        xn = _rms(x, g_ref[0:1, :]).astype(BF16)
        yield
        for c in range(D_FF // tf):
            gate = jnp.dot(xn, wi_ref[:, c * tf:(c + 1) * tf], preferred_element_type=F32)
            up = jnp.dot(xn, wi_ref[:, D_FF + c * tf:D_FF + (c + 1) * tf], preferred_element_type=F32)
            h_ref[rows, c * tf:(c + 1) * tf] = (gate * jax.nn.sigmoid(gate) * up).astype(BF16)
            yield
        y = jnp.dot(h_ref[rows, :], wo_ref[...], preferred_element_type=F32)
        yield
        o_ref[rows, :] = x + 0.5 * _rms(y, g_ref[1:2, :])
        yield

    _interleave(*[part(i) for i in range(nsplit)], skew=skew)


def _ffn(x, g2, w_in, w_out, *, tm, tf):
    n = x.shape[0]
    const = dict(pipeline_mode=pl.Buffered(1))
    return pl.pallas_call(
        functools.partial(_ffn_kernel, tf=tf, nsplit=2, skew=6),
        grid=(n // tm,),
        in_specs=[
            pl.BlockSpec((tm, D_MODEL), lambda i: (i, 0)),
            pl.BlockSpec((2, D_MODEL), lambda i: (0, 0)),
            pl.BlockSpec((D_MODEL, 2 * D_FF), lambda i: (0, 0), **const),
            pl.BlockSpec((D_FF, D_MODEL), lambda i: (0, 0), **const),
        ],
        out_specs=pl.BlockSpec((tm, D_MODEL), lambda i: (i, 0)),
        out_shape=jax.ShapeDtypeStruct((n, D_MODEL), F32),
        scratch_shapes=[pltpu.VMEM((tm, D_FF), BF16)],
        compiler_params=_cparams(("parallel",)),
        name="ffn",
    )(x, g2, w_in, w_out)


def _mixin_kernel(x_ref, g_ref, w_ref, zr_ref, zl_ref, zm_ref, zg_ref, kt_ref):
    xn = _rms(x_ref[...], g_ref[...]).astype(BF16)
    c0, c1, c2 = ZR_COLS + ZG_COLS, ZR_COLS + ZG_COLS + ZL_COLS, ZR_COLS + ZG_COLS + ZL_COLS + ZM_COLS
    zl = jnp.dot(xn, w_ref[:, c0:c1], preferred_element_type=F32)
    for m in range(zl.shape[0] // 8):
        rows = _perm_rows(m, zl.shape[0])
        for j in range(ZL_COLS // 128):
            zl_ref[j, rows, :] = zl[8 * m:8 * m + 8, 128 * j:128 * (j + 1)]
    zrg = jnp.dot(xn, w_ref[:, 0:c0], preferred_element_type=F32)
    zg = zrg[:, ZR_COLS:c0]
    zr_ref[...] = zrg[:, 0:ZR_COLS]
    zg_ref[...] = zg
    zm = jnp.dot(xn, w_ref[:, c1:c2], preferred_element_type=F32)
    zm_ref[...] = zm
    kt_ref[0:MLSTM_WIDTH, :] = zm[:, MLSTM_WIDTH:2 * MLSTM_WIDTH].T
    kt_ref[MLSTM_WIDTH:KT_ROWS, :] = zg.T[0:KT_ROWS - MLSTM_WIDTH, :]


def _mixer_in(x, g, wcat, *, tm):
    n = x.shape[0]
    ncols = wcat.shape[1]
    row = lambda i: (i, 0)
    return pl.pallas_call(
        _mixin_kernel,
        grid=(n // tm,),
        in_specs=[
            pl.BlockSpec((tm, D_MODEL), row),
            pl.BlockSpec((1, D_MODEL), lambda i: (0, 0)),
            pl.BlockSpec((D_MODEL, ncols), lambda i: (0, 0)),
        ],
        out_specs=[
            pl.BlockSpec((tm, ZR_COLS), row),
            pl.BlockSpec((ZL_COLS // 128, tm, 128), lambda i: (0, i, 0)),
            pl.BlockSpec((tm, ZM_COLS), row),
            pl.BlockSpec((tm, ZG_COLS), row),
            pl.BlockSpec((KT_ROWS, tm), lambda i: (0, i)),
        ],
        out_shape=[
            jax.ShapeDtypeStruct((n, ZR_COLS), F32),
            jax.ShapeDtypeStruct((ZL_COLS // 128, n, 128), F32),
            jax.ShapeDtypeStruct((n, ZM_COLS), F32),
            jax.ShapeDtypeStruct((n, ZG_COLS), F32),
            jax.ShapeDtypeStruct((KT_ROWS, n), F32),
        ],
        compiler_params=_cparams(("parallel",)),
        name="mixer_in",
    )(x, g, wcat)


def _lru_kernel(z_ref, pv_ref, wa_ref, wx_ref, o_ref, xtail_ref, h_ref, *, tt):
    t = pl.program_id(1)

    @pl.when(t == 0)
    def _():
        xtail_ref[...] = jnp.zeros_like(xtail_ref)
        h_ref[...] = jnp.zeros_like(h_ref)

    ng = tt // 8
    nslab = LRU_WIDTH // 128
    taps = CONV_WIDTH - 1
    sub = lax.broadcasted_iota(jnp.int32, (8, LRU_WIDTH), 0)

    def load(first_slab, g):
        return jnp.concatenate([z_ref[first_slab + j, 8 * g:8 * g + 8, :] for j in range(nslab)], axis=1)

    x = [load(0, g) for g in range(ng)]
    pv = pv_ref[...]

    def before(g):
        if g >= 0:
            return x[g]
        cur = pltpu.roll(x[ng + g], 1, axis=0)
        prev = pltpu.roll(xtail_ref[taps + g], 1, axis=0)
        return jnp.where(sub == 0, prev, cur)

    nhalf = 2
    parts = {}

    def front(i):
        pieces = range(ng // nhalf * i, ng // nhalf * (i + 1))
        xa = []
        for g in pieces:
            acc = pv[taps + 1:taps + 2, :] + pv[taps:taps + 1, :] * x[g]
            for d in range(1, taps + 1):
                acc = acc + pv[taps - d:taps - d + 1, :] * before(g - d)
            xa.append(acc)
        xa = jnp.concatenate(xa, axis=0)
        yield
        r = jax.nn.sigmoid(_dot(xa, wa_ref[...]) + pv[5:6, :])
        i_gate = jax.nn.sigmoid(_dot(xa, wx_ref[...]) + pv[6:7, :])
        yield
        log_a = (-LRU_C) * r * _softplus(-pv[7:8, :])
        a_i = jnp.exp(log_a)
        yield
        u_i = jnp.sqrt(1.0 - jnp.exp(2.0 * log_a)) * (i_gate * xa)
        yield
        gate = jnp.concatenate([load(nslab, g) for g in pieces], axis=0)
        parts[i] = (a_i, u_i, jax.nn.gelu(gate, approximate=True))
        yield

    _interleave(*[front(i) for i in range(nhalf)], skew=1)
    for d in range(taps):
        xtail_ref[d] = x[ng - taps + d]
    a = jnp.concatenate([parts[i][0] for i in range(nhalf)], axis=0)
    u = jnp.concatenate([parts[i][1] for i in range(nhalf)], axis=0)
    gg = jnp.concatenate([parts[i][2] for i in range(nhalf)], axis=0)

    h_loc, a_cum = [], []
    h = u[0:8, :]
    ac = a[0:8, :]
    h_loc.append(h)
    a_cum.append(ac)
    for g in range(1, ng):
        a_g = a[8 * g:8 * g + 8, :]
        h = a_g * h + u[8 * g:8 * g + 8, :]
        ac = a_g * ac
        h_loc.append(h)
        a_cum.append(ac)
    ae, he = ac, h
    for d in (1, 2, 4):
        keep = sub >= d
        a_sh = jnp.where(keep, pltpu.roll(ae, d, axis=0), 1.0)
        h_sh = jnp.where(keep, pltpu.roll(he, d, axis=0), 0.0)
        he = ae * h_sh + he
        ae = ae * a_sh
    h0 = h_ref[0:1, :]
    h_end = he + ae * h0
    h_ref[...] = jnp.broadcast_to(h_end[7:8, :], h_ref.shape)
    c_in = jnp.where(sub == 0, h0, pltpu.roll(h_end, 1, axis=0))
    for g in range(ng):
        out = (h_loc[g] + a_cum[g] * c_in) * gg[8 * g:8 * g + 8, :]
        for j in range(nslab):
            o_ref[j, 8 * g:8 * g + 8, :] = out[:, 128 * j:128 * (j + 1)]


def _lru(zl, pv, wa, wx, *, tt):
    _, b, t, _ = zl.shape
    nslab = LRU_WIDTH // 128
    full = lambda bb, ti: (0, 0)
    return pl.pallas_call(
        functools.partial(_lru_kernel, tt=tt),
        grid=(b, t // tt),
        in_specs=[
            pl.BlockSpec((2 * nslab, None, tt, 128), lambda bb, ti: (0, bb, ti, 0)),
            pl.BlockSpec((8, LRU_WIDTH), full),
            pl.BlockSpec((LRU_WIDTH, LRU_WIDTH), full),
            pl.BlockSpec((LRU_WIDTH, LRU_WIDTH), full),
        ],
        out_specs=pl.BlockSpec((nslab, None, tt, 128), lambda bb, ti: (0, bb, ti, 0)),
        out_shape=jax.ShapeDtypeStruct((nslab, b, t, 128), F32),
        scratch_shapes=[pltpu.VMEM((CONV_WIDTH - 1, 8, LRU_WIDTH), F32), pltpu.VMEM((8, LRU_WIDTH), F32)],
        compiler_params=_cparams(("parallel", "arbitrary")),
        name="rglru",
    )(zl, pv, wa, wx)


def _mlstm_kernel(q_ref, k_ref, v_ref, og_ref, gc_ref, kt_ref, bcol_ref, brow_ref, nrm_ref, o_ref,
                  cn_ref, m_ref, *, tt):
    t = pl.program_id(1)

    @pl.when(t == 0)
    def _():
        cn_ref[...] = jnp.zeros_like(cn_ref)
        m_ref[...] = jnp.zeros_like(m_ref)

    L = CHUNK
    gr = kt_ref[MLSTM_WIDTH:MLSTM_WIDTH + 8, :] + bcol_ref[:, 0:1]
    br = _log_sigmoid(gr)
    pos = lax.broadcasted_iota(jnp.int32, (8, tt), 1) & (L - 1)
    d = 1
    while d < L:
        br = br + jnp.where(pos >= d, pltpu.roll(br, d, axis=1), 0.0)
        d *= 2
    bc = _log_sigmoid(gc_ref[...] + brow_ref[0:1, :])
    posc = lax.broadcasted_iota(jnp.int32, (tt, ZG_COLS), 0) & (L - 1)
    d = 1
    while d < L:
        bc = bc + jnp.where(posc >= d, pltpu.roll(bc, d, axis=0), 0.0)
        d *= 2

    nck = tt // L
    heads = range(MLSTM_HEADS)
    pairs = range(MLSTM_HEADS // 2)
    units = [(c, h) for c in range(nck) for h in heads]
    punits = [(c, p) for c in range(nck) for p in pairs]
    rsl = [slice(c * L, (c + 1) * L) for c in range(nck)]
    psl = [slice(128 * p, 128 * p + 128) for p in pairs]
    jsl = [slice(128 * (c // 2), 128 * (c // 2) + 128) for c in range(nck)]

    lane = lax.broadcasted_iota(jnp.int32, (L, 128), 1)
    rowi = lax.broadcasted_iota(jnp.int32, (L, 128), 0)
    half0 = lane < HEAD_DIM
    causal = (lane & (HEAD_DIM - 1)) <= rowi
    lane1 = lax.broadcasted_iota(jnp.int32, (1, 128), 1)
    rmask = [lane1 < L, lane1 >= L]
    r2 = lax.broadcasted_iota(jnp.int32, (2 * L, 128), 0)
    l2 = lax.broadcasted_iota(jnp.int32, (2 * L, 128), 1)
    top = r2 < L
    diag = (r2 < L) == (l2 < HEAD_DIM)
    ones_bd = jnp.where(diag, 1.0, 0.0).astype(BF16)

    b_row = {(c, h): br[4 + h:5 + h, jsl[c]] for c, h in units}
    i_row = {(c, h): gr[h:h + 1, jsl[c]] for c, h in units}

    g = {(c, h): br[4 + h:5 + h, (c + 1) * L - 1:(c + 1) * L] for c, h in units}
    w_log = {u: jnp.where(rmask[u[0] % 2], g[u] - b_row[u] + i_row[u], -jnp.inf) for u in units}
    w_max = {u: jnp.max(w_log[u], axis=1, keepdims=True) for u in units}
    m_in, m_out = {}, {}
    for h in heads:
        m_st = m_ref[h, 0:1, 0:1]
        for c in range(nck):
            m_in[c, h] = m_st
            m_st = jnp.maximum(g[c, h] + m_st, w_max[c, h])
            m_out[c, h] = m_st
        m_ref[h] = jnp.broadcast_to(m_st, (8, 128))
    dec = {u: jnp.exp(g[u] + m_in[u] - m_out[u]) for u in units}
    wk = {u: jnp.exp(w_log[u] - m_out[u]) for u in units}

    def lanes_of_chunk(row128, c, e):
        return row128 if c % 2 == e else pltpu.roll(row128, HEAD_DIM, axis=1)

    def pair_row(d, c, p):
        return jnp.where(rmask[0], lanes_of_chunk(d[c, 2 * p], c, 0), lanes_of_chunk(d[c, 2 * p + 1], c, 1))

    def pair_tile(d, c, p):
        return jnp.where(half0, d[c, 2 * p], d[c, 2 * p + 1])

    q_p = {(c, p): (q_ref[rsl[c], psl[p]] * (HEAD_DIM ** -0.5)).astype(BF16) for c, p in punits}
    k_bd = {}
    v_bd = {}
    for c, p in punits:
        k_c = k_ref[rsl[c], psl[p]]
        v_c = v_ref[rsl[c], psl[p]]
        k_bd[c, p] = jnp.where(diag, jnp.concatenate([k_c, k_c], axis=0), 0.0).astype(BF16)
        v_bd[c, p] = jnp.concatenate([jnp.where(diag, jnp.concatenate([v_c, v_c], axis=0), 0.0).astype(BF16),
                                      ones_bd], axis=1)
    bcb = {(c, p): jnp.where(half0, jnp.broadcast_to(bc[rsl[c], 4 + 2 * p:5 + 2 * p], (L, 128)),
                             jnp.broadcast_to(bc[rsl[c], 5 + 2 * p:6 + 2 * p], (L, 128))) for c, p in punits}
    dmat = {(c, p): jnp.where(causal, bcb[c, p] - pair_row(b_row, c, p) + pair_row(i_row, c, p), -jnp.inf)
            for c, p in punits}
    m_loc = {u: jnp.where(half0, jnp.max(jnp.where(half0, dmat[u], -jnp.inf), axis=1, keepdims=True),
                          jnp.max(jnp.where(half0, -jnp.inf, dmat[u]), axis=1, keepdims=True)) for u in punits}
    p_in = {u: jnp.exp(dmat[u] - m_loc[u]) for u in punits}
    qk = {u: _dot_nt(q_p[u], k_bd[u]) for u in punits}
    nd = {u: _dot(p_in[u] * qk[u], v_bd[u]) for u in punits}
    kv = {}
    for c, p in punits:
        kw = kt_ref[psl[p], jsl[c]] * jnp.where(top, wk[c, 2 * p], wk[c, 2 * p + 1])
        vv = jnp.concatenate([v_ref[jsl[c], psl[p]].astype(BF16), jnp.ones((2 * L, 128), BF16)], axis=1)
        kv[c, p] = jnp.where(jnp.concatenate([diag, diag], axis=1), _dot(kw, vv), 0.0)

    cn_in = {}
    for p in pairs:
        cn = cn_ref[p]
        for c in range(nck):
            cn_in[c, p] = cn.astype(BF16)
            cn = jnp.concatenate([dec[c, 2 * p] * cn[0:L], dec[c, 2 * p + 1] * cn[L:2 * L]], axis=0) + kv[c, p]
        cn_ref[p] = cn

    inter = {(c, p): bcb[c, p] + pair_tile(m_in, c, p) for c, p in punits}
    mj = {u: jnp.maximum(m_loc[u], inter[u]) for u in punits}
    e_loc = {u: jnp.exp(m_loc[u] - mj[u]) for u in punits}
    e_int = {u: jnp.exp(inter[u] - mj[u]) for u in punits}
    e_neg = {u: jnp.exp(-mj[u]) for u in punits}
    qc = {u: _dot(q_p[u], cn_in[u]) for u in punits}
    num = {u: e_loc[u] * nd[u][:, 0:128] + e_int[u] * qc[u][:, 0:128] for u in punits}
    den = {u: e_loc[u] * nd[u][:, 128:256] + e_int[u] * qc[u][:, 128:256] for u in punits}
    hv = {u: num[u] / jnp.maximum(jnp.abs(den[u]), e_neg[u]) for u in punits}
    hv2 = {u: hv[u] * hv[u] for u in punits}
    ms = {u: jnp.where(half0, jnp.sum(jnp.where(half0, hv2[u], 0.0), axis=1, keepdims=True),
                       jnp.sum(jnp.where(half0, 0.0, hv2[u]), axis=1, keepdims=True)) * (1.0 / HEAD_DIM)
          for u in punits}
    for c, p in punits:
        hn = hv[c, p] * lax.rsqrt(ms[c, p] + RMS_EPS)
        o_ref[rsl[c], psl[p]] = hn * nrm_ref[0:1, psl[p]] * jax.nn.sigmoid(og_ref[rsl[c], psl[p]])


def _mlstm(zm, zg, ktg, bcol, brow, nrm, *, tt):
    b, t, _ = zm.shape
    nt = t // tt
    colblk = lambda j: pl.BlockSpec((None, tt, MLSTM_WIDTH), lambda bb, ti, j=j: (bb, ti, j))
    full = lambda bb, ti: (0, 0)
    return pl.pallas_call(
        functools.partial(_mlstm_kernel, tt=tt),
        grid=(b, nt),
        in_specs=[
            colblk(0), colblk(1), colblk(2), colblk(3),
            pl.BlockSpec((None, tt, ZG_COLS), lambda bb, ti: (bb, ti, 0)),
            pl.BlockSpec((KT_ROWS, tt), lambda bb, ti: (0, bb * nt + ti)),
            pl.BlockSpec((8, 128), full),
            pl.BlockSpec((8, 128), full),
            pl.BlockSpec((8, MLSTM_WIDTH), full),
        ],
        out_specs=pl.BlockSpec((None, tt, MLSTM_WIDTH), lambda bb, ti: (bb, ti, 0)),
        out_shape=jax.ShapeDtypeStruct((b, t, MLSTM_WIDTH), F32),
        scratch_shapes=[pltpu.VMEM((MLSTM_HEADS // 2, 2 * HEAD_DIM, 256), F32),
                        pltpu.VMEM((MLSTM_HEADS, 8, 128), F32)],
        compiler_params=_cparams(("parallel", "arbitrary")),
        name="mlstm",
    )(zm, zm, zm, zm, zg, ktg, bcol, brow, nrm)


def _rwkv_kernel(z_ref, mu_ref, pv_ref, wup_ref, aup_ref, gup_ref, seg_ref, o_ref,
                 prev_ref, st_ref, y_ref, *, tt, sb):
    t = pl.program_id(1)

    @pl.when(t == 0)
    def _():
        prev_ref[...] = jnp.zeros_like(prev_ref)
        st_ref[...] = jnp.zeros_like(st_ref)

    L = CHUNK
    W = RWKV_WIDTH
    UR = 2 * L
    SB = sb
    nrb = tt // SB
    nsub = SB // UR
    pv = pv_ref[...]
    w0, a0, k_k, k_a, r_k, ln_w, ln_b = (pv[j:j + 1, :] for j in range(7))
    seg = seg_ref[...]
    mu = mu_ref[...]
    heads = range(RWKV_HEADS)
    hsl = [slice(HEAD_DIM * h, HEAD_DIM * (h + 1)) for h in heads]

    ri = lax.broadcasted_iota(jnp.int32, (UR, UR), 0)
    ci = lax.broadcasted_iota(jnp.int32, (UR, UR), 1)
    same = (ri // L) == (ci // L)
    strict = jnp.where(same & (ri > ci), 1.0, 0.0)
    incl = jnp.where(same & (ri >= ci), 1.0, 0.0)
    eye = jnp.where(ri == ci, 1.0, 0.0)
    mask4 = jnp.concatenate([jnp.concatenate([strict, strict], axis=1),
                             jnp.concatenate([incl, incl], axis=1)], axis=0)
    posr = lax.broadcasted_iota(jnp.int32, (SB, W), 0) & (L - 1)
    row1 = lax.broadcasted_iota(jnp.int32, (SB, 1), 0)
    units = [(h, u) for u in range(nsub) for h in heads]
    nu = range(len(units))
    usl = [slice(UR * u, UR * (u + 1)) for _, u in units]

    prep, mats = {}, {}
    zst = [st_ref[h] for h in heads]

    def prepare(rb):
        rows = slice(SB * rb, SB * (rb + 1))
        z = z_ref[rows, :]
        first = prev_ref[0:1, :] if rb == 0 else z_ref[SB * rb - 1:SB * rb, :]
        zprev = jnp.where(row1 == 0, first, pltpu.roll(z, 1, axis=0))
        yield
        zs = z + (zprev - z) * mu
        yield
        r = zs[:, 0:W]
        k = zs[:, W:2 * W]
        v = zs[:, 2 * W:3 * W]
        wd = zs[:, 3 * W:3 * W + 64]
        ad = zs[:, 3 * W + 64:3 * W + 128]
        gd = zs[:, 3 * W + 128:3 * W + 256]
        log_w = -_softplus(-(w0 + _dot(jnp.tanh(wd), wup_ref[...]))) - 0.5
        yield
        ld = -jnp.exp(log_w)
        yield
        iclr = jax.nn.sigmoid(a0 + _dot(ad, aup_ref[...]))
        yield
        g = _dot(jax.nn.sigmoid(gd), gup_ref[...])
        yield
        kk = k * k_k
        ss = _segsum(kk * kk, seg)
        yield
        kk = kk / jnp.maximum(jnp.sqrt(ss), 1e-12)
        yield
        k_mod = k * (1.0 + (iclr - 1.0) * k_a)
        a_vec = -kk
        b_vec = kk * iclr
        yield
        cum = ld
        d = 1
        while d < L:
            cum = cum + jnp.where(posr >= d, pltpu.roll(cum, d, axis=0), 0.0)
            d *= 2
            yield
        cum_last = jnp.concatenate(
            [jnp.broadcast_to(cum[(c + 1) * L - 1:(c + 1) * L, :], (L, W)) for c in range(SB // L)], axis=0)
        e_neg = jnp.exp(-cum)
        yield
        e_tail = jnp.exp(cum_last - cum)
        yield
        a_t = jnp.exp(cum - ld) * a_vec
        yield
        r_t = jnp.exp(cum) * r
        yield
        b_t = e_neg * b_vec
        k_t = e_neg * k_mod
        yield
        b_h = e_tail * b_vec
        k_h = e_tail * k_mod
        yield
        bh_t, kh_t, cum_t = [], [], []
        for pair in range(RWKV_HEADS // 2):
            ps = slice(128 * pair, 128 * pair + 128)
            bp, kp, cp = b_h[:, ps].T.astype(BF16), k_h[:, ps].T.astype(BF16), cum[:, ps].T
            for e in range(2):
                es = slice(HEAD_DIM * e, HEAD_DIM * (e + 1))
                bh_t.append(bp[es])
                kh_t.append(kp[es])
                cum_t.append(cp[es])
            yield
        prep[rb] = dict(
            at=[a_t[usl[i], hsl[h]].astype(BF16) for i, (h, _) in enumerate(units)],
            rt=[r_t[usl[i], hsl[h]] for i, (h, _) in enumerate(units)],
            bt=[b_t[usl[i], hsl[h]].astype(BF16) for i, (h, _) in enumerate(units)],
            kt=[k_t[usl[i], hsl[h]].astype(BF16) for i, (h, _) in enumerate(units)],
            vu=[v[usl[i], hsl[h]].astype(BF16) for i, (h, _) in enumerate(units)],
            bh_t=bh_t, kh_t=kh_t, cum_t=cum_t,
            bonus_in=r * k_mod * r_k, v=v, g=g)
        yield

    def products(rb):
        p = prep[rb]
        at, rt, bt, kt, vu = p["at"], p["rt"], p["bt"], p["kt"], p["vu"]
        prod = [mask4 * _dot_nt(jnp.concatenate([at[i], rt[i].astype(BF16)], axis=0),
                                jnp.concatenate([bt[i], kt[i]], axis=0)) for i in nu]
        yield
        a_ab = [prod[i][0:UR, 0:UR] for i in nu]
        a_ak = [prod[i][0:UR, UR:2 * UR].astype(BF16) for i in nu]
        a_rbk = [prod[i][UR:2 * UR, :].astype(BF16) for i in nu]
        yield
        tm = [eye + a_ab[i] for i in nu]
        pw = [a_ab[i].astype(BF16) for i in nu]
        yield
        pw = [_dot(pw[i], pw[i]).astype(BF16) for i in nu]
        yield
        for _ in range(4):
            res = [_dot(pw[i], jnp.concatenate([pw[i], tm[i].astype(BF16)], axis=1)) for i in nu]
            yield
            pw = [res[i][:, 0:UR].astype(BF16) for i in nu]
            tm = [tm[i] + res[i][:, UR:2 * UR] for i in nu]
            yield
        tm = [tm[i] + _dot(pw[i], tm[i]) for i in nu]
        yield
        akv = [_dot(a_ak[i], vu[i]) for i in nu]
        yield
        tmb = [tm[i].astype(BF16) for i in nu]
        w_all = [_dot(tmb[i], at[i]).astype(BF16) for i in nu]
        yield
        u0_all = [_dot(tmb[i], akv[i]).astype(BF16) for i in nu]
        yield
        q_all = [(rt[i] + _dot(a_rbk[i][:, 0:UR], w_all[i])).astype(BF16) for i in nu]
        yield
        y0_all = [_dot(a_rbk[i], jnp.concatenate([u0_all[i], vu[i]], axis=0)) for i in nu]
        yield
        qc, hm, pcol, y0c = {}, {}, {}, {}
        for i, (h, u) in enumerate(units):
            for cc in range(UR // L):
                c = u * (UR // L) + cc
                ls = slice(cc * L, (cc + 1) * L)
                rs = slice(c * L, (c + 1) * L)
                gm = _dot(p["bh_t"][h][:, rs], w_all[i][ls]).astype(BF16)
                hm[c, h] = _dot(p["bh_t"][h][:, rs], u0_all[i][ls]) + _dot(p["kh_t"][h][:, rs], vu[i][ls])
                pcol[c, h] = jnp.exp(p["cum_t"][h][:, (c + 1) * L - 1:(c + 1) * L])
                qc[c, h] = jnp.concatenate([q_all[i][ls], gm], axis=0)
                y0c[c, h] = y0_all[i][ls]
            if h == RWKV_HEADS - 1:
                yield
        mats[rb] = dict(qc=qc, hm=hm, pcol=pcol, y0c=y0c)
        yield

    def finish(rb):
        m, p = mats[rb], prep[rb]
        for cc in range(SB // L):
            rs = slice(SB * rb + cc * L, SB * rb + (cc + 1) * L)
            for h in heads:
                qz = _dot(m["qc"][cc, h], zst[h])
                y_ref[rs, hsl[h]] = qz[0:L] + m["y0c"][cc, h]
                zst[h] = m["pcol"][cc, h] * zst[h] + qz[L:2 * L] + m["hm"][cc, h]
                yield
        rows = slice(SB * rb, SB * (rb + 1))
        y = y_ref[rows, :]
        mean = _segsum(y, seg) * (1.0 / HEAD_DIM)
        yield
        yc = y - mean
        var = _segsum(yc * yc, seg) * (1.0 / HEAD_DIM)
        yield
        yn = yc * lax.rsqrt(var + GN_EPS) * ln_w + ln_b
        yield
        bonus = _segsum(p["bonus_in"], seg) * p["v"]
        yield
        o_ref[rows, :] = (yn + bonus) * p["g"]
        yield

    _interleave(prepare(0))
    for step in range(1, nrb + 2):
        stage = []
        if step < nrb:
            stage.append(prepare(step))
        if 0 <= step - 1 < nrb:
            stage.append(products(step - 1))
        if 0 <= step - 2 < nrb:
            stage.append(finish(step - 2))
        _interleave(*stage)
    for h in heads:
        st_ref[h] = zst[h]
    prev_ref[...] = jnp.broadcast_to(z_ref[tt - 1:tt, :], prev_ref.shape)


def _rwkv(zr, mu, pv, wup, aup, gup, seg, *, tt, sb):
    b, t, _ = zr.shape
    full = lambda bb, ti: (0, 0)
    return pl.pallas_call(
        functools.partial(_rwkv_kernel, tt=tt, sb=sb),
        grid=(b, t // tt),
        in_specs=[
            pl.BlockSpec((None, tt, N_RWKV_COLS), lambda bb, ti: (bb, ti, 0)),
            pl.BlockSpec((1, N_RWKV_COLS), full),
            pl.BlockSpec((8, RWKV_WIDTH), full),
            pl.BlockSpec((64, RWKV_WIDTH), full),
            pl.BlockSpec((64, RWKV_WIDTH), full),
            pl.BlockSpec((128, RWKV_WIDTH), full),
            pl.BlockSpec((4 * HEAD_DIM, 4 * HEAD_DIM), full),
        ],
        out_specs=pl.BlockSpec((None, tt, RWKV_WIDTH), lambda bb, ti: (bb, ti, 0)),
        out_shape=jax.ShapeDtypeStruct((b, t, RWKV_WIDTH), F32),
        scratch_shapes=[
            pltpu.VMEM((8, N_RWKV_COLS), F32),
            pltpu.VMEM((RWKV_HEADS, HEAD_DIM, HEAD_DIM), F32),
            pltpu.VMEM((tt, RWKV_WIDTH), F32),
        ],
        compiler_params=_cparams(("parallel", "arbitrary")),
        name="rwkv7",
    )(zr, mu, pv, wup, aup, gup, seg)


def _tail_kernel(x_ref, ya_ref, yb_ref, yc_ref, p_ref, g_ref, wm_ref, wi_ref, wo_ref, wg_ref, wp_ref, o_ref,
                 h_ref, *, tf, nsplit, skew):
    c0, c1 = LRU_WIDTH, LRU_WIDTH + MLSTM_WIDTH
    tm = x_ref.shape[0]
    hr = tm // nsplit

    def part(i):
        rows = slice(hr * i, hr * (i + 1))
        ya = jnp.concatenate(
            [jnp.concatenate([ya_ref[j, _perm_rows(m, tm), :] for j in range(LRU_WIDTH // 128)], axis=1)
             for m in range(hr * i // 8, hr * (i + 1) // 8)], axis=0)
        mix = (jnp.dot(ya.astype(BF16), wm_ref[0:c0, :], preferred_element_type=F32)
               + jnp.dot(yb_ref[rows, :].astype(BF16), wm_ref[c0:c1, :], preferred_element_type=F32)
               + jnp.dot(yc_ref[rows, :].astype(BF16), wm_ref[c1:, :], preferred_element_type=F32))
        yield
        x = x_ref[rows, :] + _rms(mix, g_ref[3:4, :])
        yield
        xn = _rms(x, g_ref[4:5, :]).astype(BF16)
        yield
        for c in range(D_FF // tf):
            gate = jnp.dot(xn, wi_ref[:, c * tf:(c + 1) * tf], preferred_element_type=F32)
            up = jnp.dot(xn, wi_ref[:, D_FF + c * tf:D_FF + (c + 1) * tf], preferred_element_type=F32)
            h_ref[rows, c * tf:(c + 1) * tf] = (gate * jax.nn.sigmoid(gate) * up).astype(BF16)
            yield
        y = jnp.dot(h_ref[rows, :], wo_ref[...], preferred_element_type=F32)
        yield
        x = x + 0.5 * _rms(y, g_ref[5:6, :])
        yield
        pgate = jax.nn.sigmoid(
            jnp.dot(_rms(x, g_ref[6:7, :]).astype(BF16), wg_ref[...], preferred_element_type=F32))
        yield
        pe = jnp.dot(p_ref[rows, :].astype(BF16), wp_ref[...], preferred_element_type=F32)
        yield
        o_ref[rows, :] = x + _rms(pgate * pe, g_ref[7:8, :])
        yield

    _interleave(*[part(i) for i in range(nsplit)], skew=skew)


def _tail(x, ya, yb, yc, p, g8, w_mix, w_in, w_out, w_gate, w_proj, *, tm, tf):
    n = x.shape[0]
    row = lambda i: (i, 0)
    const = lambda shape: pl.BlockSpec(shape, lambda i: (0, 0), pipeline_mode=pl.Buffered(1))
    return pl.pallas_call(
        functools.partial(_tail_kernel, tf=tf, nsplit=2, skew=11),
        grid=(n // tm,),
        in_specs=[
            pl.BlockSpec((tm, D_MODEL), row),
            pl.BlockSpec((LRU_WIDTH // 128, tm, 128), lambda i: (0, i, 0)),
            pl.BlockSpec((tm, MLSTM_WIDTH), row),
            pl.BlockSpec((tm, RWKV_WIDTH), row),
            pl.BlockSpec((tm, PLE_DIM), row),
            const((8, D_MODEL)),
            const((D_MODEL, D_MODEL)),
            const((D_MODEL, 2 * D_FF)),
            const((D_FF, D_MODEL)),
            const((D_MODEL, D_MODEL)),
            const((PLE_DIM, D_MODEL)),
        ],
        out_specs=pl.BlockSpec((tm, D_MODEL), row),
        out_shape=jax.ShapeDtypeStruct((n, D_MODEL), F32),
        scratch_shapes=[pltpu.VMEM((tm, D_FF), BF16)],
        compiler_params=_cparams(("parallel",)),
        name="tail",
    )(x, ya, yb, yc, p, g8, w_mix, w_in, w_out, w_gate, w_proj)


def _block_diag(w):
    nh, dd, _ = w.shape
    eye = jnp.eye(nh, dtype=w.dtype)
    return (eye[:, None, :, None] * w[:, :, None, :]).reshape(nh * dd, nh * dd)


def _tile(n, pref):
    return pref if n % pref == 0 else n


def kernel(x, p, norm_g, ffn_w_in, ffn_w_out, w_in, w_out, lru_conv_w, lru_conv_b, lru_w_a, lru_b_a, lru_w_x, lru_b_x, lru_lambda, m_b_i, m_b_f, m_norm, rw_mu, rw_w0, rw_w_up, rw_a0, rw_a_up, rw_g_up, rw_k_k, rw_k_a, rw_r_k, rw_ln_w, rw_ln_b, ple_w_proj, ple_w_gate):
    bsz, t, _ = x.shape
    n = bsz * t
    depth = norm_g.shape[0]
    tm = _tile(n, 512)
    tt_lru = _tile(t, 512)
    tt_m = _tile(t, 1024)
    tt_r = _tile(t, 1024)
    assert tm == tt_lru and t % tt_lru == 0, (tm, tt_lru, t)
    seg = _block_diag(jnp.ones((4, HEAD_DIM, HEAD_DIM), BF16))

    xf = x.reshape(n, D_MODEL)
    for l in range(depth):
        g = norm_g[l]
        wl = w_in[l]
        o_zr = 2 * LRU_WIDTH + 4 * MLSTM_WIDTH + 2 * MLSTM_HEADS
        w_gates = wl[:, o_zr - 2 * MLSTM_HEADS:o_zr]
        wcat = jnp.concatenate(
            [wl[:, o_zr:], jnp.pad(w_gates, ((0, 0), (0, ZG_COLS - 2 * MLSTM_HEADS))),
             wl[:, 0:ZL_COLS], wl[:, ZL_COLS:ZL_COLS + ZM_COLS]], axis=1).astype(BF16)

        xf = _ffn(xf, g[0:2], ffn_w_in[l, 0].astype(BF16), ffn_w_out[l, 0].astype(BF16), tm=tm, tf=256)

        zr, zl, zm, zg, ktg = _mixer_in(xf, g[2:3], wcat, tm=tm)

        lru_pv = jnp.concatenate([lru_conv_w[l], lru_conv_b[l][None], lru_b_a[l][None], lru_b_x[l][None],
                                  lru_lambda[l][None]], axis=0)
        ya = _lru(zl.reshape(ZL_COLS // 128, bsz, t, 128), lru_pv, _block_diag(lru_w_a[l]).astype(BF16),
                  _block_diag(lru_w_x[l]).astype(BF16), tt=tt_lru)

        gate_b = jnp.concatenate([m_b_i[l], m_b_f[l]])
        bcol = jnp.broadcast_to(gate_b[:, None], (8, 128))
        brow = jnp.broadcast_to(jnp.pad(gate_b, (0, ZG_COLS - 8))[None, :], (8, ZG_COLS))
        nrm = jnp.broadcast_to(m_norm[l][None, :], (8, MLSTM_WIDTH))
        yb = _mlstm(zm.reshape(bsz, t, ZM_COLS), zg.reshape(bsz, t, ZG_COLS), ktg, bcol, brow, nrm, tt=tt_m)

        rw_pv = jnp.stack([rw_w0[l], rw_a0[l], rw_k_k[l], rw_k_a[l], rw_r_k[l].reshape(-1), rw_ln_w[l],
                           rw_ln_b[l], jnp.zeros((RWKV_WIDTH,), F32)], axis=0)
        yc = _rwkv(zr.reshape(bsz, t, ZR_COLS), rw_mu[l][None, :], rw_pv, rw_w_up[l].astype(BF16),
                   rw_a_up[l].astype(BF16), rw_g_up[l].astype(BF16), seg, tt=tt_r, sb=min(tt_r, 256))

        xf = _tail(xf, ya.reshape(LRU_WIDTH // 128, n, 128), yb.reshape(n, MLSTM_WIDTH), yc.reshape(n, RWKV_WIDTH),
                   p[l].reshape(n, PLE_DIM), g, w_out[l].astype(BF16), ffn_w_in[l, 1].astype(BF16),
                   ffn_w_out[l, 1].astype(BF16), ple_w_gate[l].astype(BF16), ple_w_proj[l].astype(BF16),
                   tm=tm, tf=256)
    return xf.reshape(bsz, t, D_MODEL)
```
